```python
import jax
import jax.numpy as jnp
from jax import lax
import numpy as np

D_MODEL = 1024
BATCH = 2
SEQ = 8192
DEPTH = 2
DEC_BATCH = 128
DEC_SEQ = 4
PAST_LEN = 8192
PAGE_SIZE = 128

N_A_LAYERS = DEPTH // 2
N_B_LAYERS = DEPTH - N_A_LAYERS
D_FF = ((8 * D_MODEL // 3 + 127) // 128) * 128
CONV_W = 3
N_HEADS = D_MODEL // 128
NOPE_DIM = 128
ROPE_DIM = 64
V_DIM = 128
Q_RANK = D_MODEL // 2
KV_RANK = D_MODEL // 4
Q_BLOCK = 128
ROPE_THETA = 10000.0
EPS = 1e-6
SCALE = (NOPE_DIM + ROPE_DIM) ** -0.5
NEG = -1e30

kernel_name = 'hybrid_shortconv_mla_yoco_step'


def rmsnorm(x, g):
    xf = x.astype(jnp.float32)
    y = xf * lax.rsqrt(jnp.mean(xf * xf, axis=-1, keepdims=True) + EPS)
    return (y * g.astype(jnp.float32)).astype(x.dtype)


def swiglu(x, w_gu, w_down):
    gate, up = jnp.split(x @ w_gu, 2, axis=-1)
    return (jax.nn.silu(gate) * up) @ w_down


def rope(x, pos):
    half = x.shape[-1] // 2
    freqs = ROPE_THETA ** (-jnp.arange(half, dtype=jnp.float32) / half)
    ang = pos.astype(jnp.float32)[:, None] * freqs
    bshape = (pos.shape[0],) + (1,) * (x.ndim - 3) + (half,)
    cos = jnp.cos(ang).reshape(bshape)
    sin = jnp.sin(ang).reshape(bshape)
    xf = x.astype(jnp.float32)
    x1, x2 = xf[..., :half], xf[..., half:]
    return jnp.concatenate([x1 * cos - x2 * sin, x2 * cos + x1 * sin], axis=-1).astype(x.dtype)


def short_conv_mixer(xn, prev, w_in, w_conv, w_out):
    b, c, h = jnp.split(xn @ w_in, 3, axis=-1)
    u = c * h
    ext = jnp.concatenate([prev.astype(u.dtype), u], axis=1)
    s = u.shape[1]
    y = w_conv[0] * ext[:, 0:s]
    for j in range(1, CONV_W):
        y = y + w_conv[j] * ext[:, j:j + s]
    return (b * y) @ w_out, ext[:, s:]


def shared_kv(h, pos, norm_kv_in, w_dkv, kv_norm, k_pe_norm):
    ckv = rmsnorm(h, norm_kv_in) @ w_dkv
    lat = rmsnorm(ckv[..., :KV_RANK], kv_norm)
    kpe = rope(rmsnorm(ckv[..., KV_RANK:], k_pe_norm), pos)
    return lat, kpe


def expand_kv(lat, w_uk, w_uv, k_nope_norm):
    lead = lat.shape[:-1]
    kn = rmsnorm((lat @ w_uk).reshape(lead + (N_HEADS, NOPE_DIM)), k_nope_norm)
    v = (lat @ w_uv).reshape(lead + (N_HEADS, V_DIM))
    return kn, v


def mla_queries(xn, pos, w_dq, q_norm, w_uq, q_nope_norm, q_pe_norm):
    cq = rmsnorm(xn @ w_dq, q_norm)
    q = (cq @ w_uq).reshape(xn.shape[:-1] + (N_HEADS, NOPE_DIM + ROPE_DIM))
    qn = rmsnorm(q[..., :NOPE_DIM], q_nope_norm)
    qp = rope(rmsnorm(q[..., NOPE_DIM:], q_pe_norm), pos)
    return qn, qp


def attend(qn, qp, kn, kp, v, q_pos, k_pos):
    s = (jnp.einsum('bqhd,bkhd->bhqk', qn, kn)
         + jnp.einsum('bqhr,bkr->bhqk', qp, kp)).astype(jnp.float32) * SCALE
    s = jnp.where(q_pos[:, None] >= k_pos[None, :], s, NEG)
    p = jax.nn.softmax(s, axis=-1).astype(v.dtype)
    return jnp.einsum('bhqk,bkhd->bqhd', p, v)


def forward(x, pos, conv_prev, attn_core, p):
    conv_states = []
    lat = None
    kpe = None
    for l in range(DEPTH):
        x = x + 0.5 * swiglu(rmsnorm(x, p['norm_ffn1'][l]), p['w_ffn1_gu'][l], p['w_ffn1_down'][l])
        xn = rmsnorm(x, p['norm_mix'][l])
        if l < N_A_LAYERS:
            mix, st = short_conv_mixer(xn, conv_prev[l], p['w_conv_in'][l], p['conv_w'][l], p['w_conv_out'][l])
            conv_states.append(st)
        else:
            j = l - N_A_LAYERS
            qn, qp = mla_queries(xn, pos, p['w_dq'][j], p['q_norm'][j], p['w_uq'][j],
                                 p['q_nope_norm'][j], p['q_pe_norm'][j])
            o = attn_core(qn, qp, lat, kpe)
            mix = o.reshape(o.shape[:2] + (N_HEADS * V_DIM,)) @ p['w_o'][j]
        x = x + mix
        x = x + 0.5 * swiglu(rmsnorm(x, p['norm_ffn2'][l]), p['w_ffn2_gu'][l], p['w_ffn2_down'][l])
        if l == N_A_LAYERS - 1:
            lat, kpe = shared_kv(x, pos, p['norm_kv_in'], p['w_dkv'], p['kv_norm'], p['k_pe_norm'])
    return x, jnp.stack(conv_states), lat, kpe


def setup_inputs(seed: int = 0) -> dict:
    key = jax.random.key(seed)
    ks = list(jax.random.split(key, 40))
    f32 = jnp.float32

    def w(shape, fan_in):
        return jax.random.normal(ks.pop(), shape, f32) * (fan_in ** -0.5)

    def g(shape):
        return 1.0 + 0.1 * jax.random.normal(ks.pop(), shape, f32)

    n_pages = PAST_LEN // PAGE_SIZE
    n_phys = (DEC_BATCH * n_pages * 5) // 4
    perm = jax.random.permutation(ks.pop(), n_phys)
    page_table = perm[:DEC_BATCH * n_pages].reshape(DEC_BATCH, n_pages).astype(jnp.int32)
    return {
        'x_prompt': jax.random.normal(ks.pop(), (BATCH, SEQ, D_MODEL), f32),
        'x_sample': jax.random.normal(ks.pop(), (DEC_BATCH, DEC_SEQ, D_MODEL), f32),
        'cache_latent': jax.random.normal(ks.pop(), (n_phys, PAGE_SIZE, KV_RANK), f32),
        'cache_kpe': jax.random.normal(ks.pop(), (n_phys, PAGE_SIZE, ROPE_DIM), f32),
        'state_conv': jax.random.normal(ks.pop(), (N_A_LAYERS, DEC_BATCH, CONV_W - 1, D_MODEL), f32),
        'page_table': page_table,
        'norm_ffn1': g((DEPTH, D_MODEL)),
        'w_ffn1_gu': w((DEPTH, D_MODEL, 2 * D_FF), D_MODEL),
        'w_ffn1_down': w((DEPTH, D_FF, D_MODEL), D_FF),
        'norm_mix': g((DEPTH, D_MODEL)),
        'norm_ffn2': g((DEPTH, D_MODEL)),
        'w_ffn2_gu': w((DEPTH, D_MODEL, 2 * D_FF), D_MODEL),
        'w_ffn2_down': w((DEPTH, D_FF, D_MODEL), D_FF),
        'w_conv_in': w((N_A_LAYERS, D_MODEL, 3 * D_MODEL), D_MODEL),
        'conv_w': w((N_A_LAYERS, CONV_W, D_MODEL), CONV_W),
        'w_conv_out': w((N_A_LAYERS, D_MODEL, D_MODEL), D_MODEL),
        'w_dq': w((N_B_LAYERS, D_MODEL, Q_RANK), D_MODEL),
        'q_norm': g((N_B_LAYERS, Q_RANK)),
        'w_uq': w((N_B_LAYERS, Q_RANK, N_HEADS * (NOPE_DIM + ROPE_DIM)), Q_RANK),
        'q_nope_norm': g((N_B_LAYERS, NOPE_DIM)),
        'q_pe_norm': g((N_B_LAYERS, ROPE_DIM)),
        'w_o': w((N_B_LAYERS, N_HEADS * V_DIM, D_MODEL), N_HEADS * V_DIM),
        'norm_kv_in': g((D_MODEL,)),
        'w_dkv': w((D_MODEL, KV_RANK + ROPE_DIM), D_MODEL),
        'kv_norm': g((KV_RANK,)),
        'k_pe_norm': g((ROPE_DIM,)),
        'w_uk': w((KV_RANK, N_HEADS * NOPE_DIM), KV_RANK),
        'w_uv': w((KV_RANK, N_HEADS * V_DIM), KV_RANK),
        'k_nope_norm': g((NOPE_DIM,)),
    }


def reference(x_prompt, x_sample, cache_latent, cache_kpe, state_conv, page_table,
              norm_ffn1, w_ffn1_gu, w_ffn1_down, norm_mix, norm_ffn2, w_ffn2_gu, w_ffn2_down,
              w_conv_in, conv_w, w_conv_out,
              w_dq, q_norm, w_uq, q_nope_norm, q_pe_norm, w_o,
              norm_kv_in, w_dkv, kv_norm, k_pe_norm, w_uk, w_uv, k_nope_norm):
    p = dict(norm_ffn1=norm_ffn1, w_ffn1_gu=w_ffn1_gu, w_ffn1_down=w_ffn1_down,
             norm_mix=norm_mix, norm_ffn2=norm_ffn2, w_ffn2_gu=w_ffn2_gu, w_ffn2_down=w_ffn2_down,
             w_conv_in=w_conv_in, conv_w=conv_w, w_conv_out=w_conv_out,
             w_dq=w_dq, q_norm=q_norm, w_uq=w_uq, q_nope_norm=q_nope_norm, q_pe_norm=q_pe_norm,
             w_o=w_o, norm_kv_in=norm_kv_in, w_dkv=w_dkv, kv_norm=kv_norm, k_pe_norm=k_pe_norm)

    def prompt_attn(qn, qp, lat, kpe):
        kn, v = expand_kv(lat, w_uk, w_uv, k_nope_norm)
        s_len = qn.shape[1]
        pos = jnp.arange(s_len)
        outs = []
        for i in range(s_len // Q_BLOCK):
            lo, hi = i * Q_BLOCK, (i + 1) * Q_BLOCK
            outs.append(attend(qn[:, lo:hi], qp[:, lo:hi], kn[:, :hi], kpe[:, :hi], v[:, :hi],
                               pos[lo:hi], pos[:hi]))
        return jnp.concatenate(outs, axis=1)

    def sample_attn(qn, qp, lat_new, kpe_new):
        s_new = qn.shape[1]
        n_past = page_table.shape[1] * PAGE_SIZE
        q_pos = n_past + jnp.arange(s_new)
        k_pos = jnp.arange(n_past + s_new)

        def one_seq(args):
            qn_b, qp_b, pt_b, lat_b, kpe_b = args
            lat_all = jnp.concatenate([cache_latent[pt_b].reshape(n_past, KV_RANK), lat_b], axis=0)
            kpe_all = jnp.concatenate([cache_kpe[pt_b].reshape(n_past, ROPE_DIM), kpe_b], axis=0)
            kn, v = expand_kv(lat_all, w_uk, w_uv, k_nope_norm)
            return attend(qn_b[None], qp_b[None], kn[None], kpe_all[None], v[None], q_pos, k_pos)[0]

        return lax.map(one_seq, (qn, qp, page_table, lat_new, kpe_new))

    pos_prompt = jnp.arange(x_prompt.shape[1])
    pos_sample = page_table.shape[1] * PAGE_SIZE + jnp.arange(x_sample.shape[1])
    conv_zero = jnp.zeros((N_A_LAYERS, x_prompt.shape[0], CONV_W - 1, D_MODEL), x_prompt.dtype)

    y_prompt, conv_p, lat_p, kpe_p = forward(x_prompt, pos_prompt, conv_zero, prompt_attn, p)
    y_sample, conv_s, lat_s, kpe_s = forward(x_sample, pos_sample, state_conv, sample_attn, p)
    return (y_prompt, y_sample, conv_p, conv_s, lat_p, kpe_p, lat_s, kpe_s)
```

```python
import functools

import jax
import jax.numpy as jnp
from jax import lax
from jax.experimental import pallas as pl
from jax.experimental.pallas import tpu as pltpu

F32 = jnp.float32
BF16 = jnp.bfloat16

D_MODEL = 1024
D_FF = 2816
N_HEADS = 8
NOPE_DIM = 128
ROPE_DIM = 64
V_DIM = 128
Q_RANK = 512
KV_RANK = 256
CONV_W = 3
PAGE_SIZE = 128
ROPE_THETA = 10000.0
EPS = 1e-6
SCALE = (NOPE_DIM + ROPE_DIM) ** -0.5
NEG = -1e30

LANES = 128
SUBLANES = 8
TM = 512
FF_CHUNKS = 2
FC = D_FF // FF_CHUNKS
TQ = 512
KEY_CHUNK = 1024
VMEM_LIMIT = 56 * 1024 * 1024

_NT = (((1,), (1,)), ((), ()))


def _const_spec(shape):
    zeros = (0,) * len(shape)
    return pl.BlockSpec(shape, lambda *_: zeros, pipeline_mode=pl.Buffered(1))


def _rms(x, g):
    ms = jnp.mean(x * x, axis=-1, keepdims=True)
    return x * lax.rsqrt(ms + EPS) * g


def _dot(a, b):
    return jnp.dot(a, b, preferred_element_type=F32)


def _ffn(x, g, wgu_ref, wd_ref):
    xn = _rms(x, g).astype(BF16)
    acc = None
    for c in range(FF_CHUNKS):
        h = _dot(xn, wgu_ref[c])
        gate = h[:, :FC]
        up = h[:, FC:]
        act = (gate * jax.nn.sigmoid(gate) * up).astype(BF16)
        part = _dot(act, wd_ref[c])
        acc = part if acc is None else acc + part
    return x + 0.5 * acc


def _rope_norm(a, b, ga, gb, cos_t, sin_t):
    ms = jnp.sum(a * a, axis=-1, keepdims=True) * (1.0 / ROPE_DIM)
    return (a * ga * cos_t + b * gb * sin_t) * lax.rsqrt(ms + EPS)


def _ffn_kernel(x_ref, g_ref, wgu_ref, wd_ref, o_ref):
    o_ref[...] = _ffn(x_ref[...], g_ref[...], wgu_ref, wd_ref)


def _mixer_kernel(tiles_per_seq, is_sample, *refs):
    if is_sample:
        (x_ref, g_ref, win_ref, cw_ref, wout_ref, p1_ref, p2_ref,
         o_ref, u_ref, ubuf) = refs
    else:
        x_ref, g_ref, win_ref, cw_ref, wout_ref, o_ref, u_ref, ubuf = refs
    i = pl.program_id(0)
    x = x_ref[...]
    xn = _rms(x, g_ref[...]).astype(BF16)
    bch = _dot(xn, win_ref[...])
    b = bch[:, :D_MODEL]
    u = bch[:, D_MODEL:2 * D_MODEL] * bch[:, 2 * D_MODEL:]

    @pl.when(i % tiles_per_seq == 0)
    def _():
        ubuf[0:SUBLANES, :] = jnp.zeros((SUBLANES, D_MODEL), F32)

    ubuf[SUBLANES:SUBLANES + TM, :] = u
    s1 = ubuf[SUBLANES - 1:SUBLANES - 1 + TM, :]
    s2 = ubuf[SUBLANES - 2:SUBLANES - 2 + TM, :]
    if is_sample:
        t = lax.broadcasted_iota(jnp.int32, (TM, 1), 0) % 4
        s1 = jnp.where(t >= 1, s1, p1_ref[...])
        s2 = jnp.where(t >= 2, s2, p2_ref[...])
    cw = cw_ref[...]
    y = cw[0:1, :] * s2 + cw[1:2, :] * s1 + cw[2:3, :] * u
    o_ref[...] = x + _dot((b * y).astype(BF16), wout_ref[...])
    if is_sample:
        u_ref[...] = u
    else:
        ubuf[0:SUBLANES, :] = u[TM - SUBLANES:, :]
        u_ref[0] = u[TM - SUBLANES:, :]


def _ffn_kv_kernel(x_ref, g_ref, wgu_ref, wd_ref, gkv_ref, wdkv_ref, kvn_ref,
                   ga_ref, gb_ref, cos_ref, sin_ref, wukv_ref, gkn_ref,
                   o_ref, lat_ref, kpe_ref, kpeb_ref, kn_ref, v_ref):
    x = _ffn(x_ref[...], g_ref[...], wgu_ref, wd_ref)
    o_ref[...] = x
    hk = _rms(x, gkv_ref[...]).astype(BF16)
    ckv = _dot(hk, wdkv_ref[...])
    lat = _rms(ckv[:, :KV_RANK], kvn_ref[...])
    lat_ref[...] = lat
    kpe = _rope_norm(ckv[:, KV_RANK:KV_RANK + LANES], ckv[:, KV_RANK + LANES:],
                     ga_ref[...], gb_ref[...], cos_ref[...], sin_ref[...])
    kpe_ref[...] = kpe[:, :ROPE_DIM]
    kpeb_ref[...] = kpe.astype(BF16)
    ekv = _dot(lat.astype(BF16), wukv_ref[...])
    gkn = gkn_ref[...]
    for h in range(N_HEADS):
        e = ekv[:, h * NOPE_DIM:(h + 1) * NOPE_DIM]
        kn_ref[:, h * NOPE_DIM:(h + 1) * NOPE_DIM] = _rms(e, gkn).astype(BF16)
    v_ref[...] = ekv[:, N_HEADS * NOPE_DIM:].astype(BF16)


def _ffn_q_kernel(x_ref, g_ref, wgu_ref, wd_ref, gmix_ref, wdq_ref, qn_ref,
                  wuq_ref, gqn_ref, ga_ref, gb_ref, cos_ref, sin_ref,
                  o_ref, q_ref):
    x = _ffn(x_ref[...], g_ref[...], wgu_ref, wd_ref)
    o_ref[...] = x
    xn = _rms(x, gmix_ref[...]).astype(BF16)
    cq = _rms(_dot(xn, wdq_ref[...]), qn_ref[...]).astype(BF16)
    q = _dot(cq, wuq_ref[...])
    gqn = gqn_ref[...]
    ga = ga_ref[...]
    gb = gb_ref[...]
    cos_t = cos_ref[...]
    sin_t = sin_ref[...]
    hw = N_HEADS * LANES
    for h in range(N_HEADS):
        qn = _rms(q[:, h * LANES:(h + 1) * LANES], gqn) * SCALE
        qp = _rope_norm(q[:, hw + h * LANES:hw + (h + 1) * LANES],
                        q[:, 2 * hw + h * LANES:2 * hw + (h + 1) * LANES],
                        ga, gb, cos_t, sin_t) * SCALE
        q_ref[:, 2 * h * LANES:(2 * h + 1) * LANES] = qn.astype(BF16)
        q_ref[:, (2 * h + 1) * LANES:(2 * h + 2) * LANES] = qp.astype(BF16)


def _out_ffn_kernel(x_ref, a_ref, wo_ref, g_ref, wgu_ref, wd_ref, o_ref):
    x = x_ref[...] + _dot(a_ref[...], wo_ref[...])
    o_ref[...] = _ffn(x, g_ref[...], wgu_ref, wd_ref)


def _prompt_attn_kernel(q_ref, kn_ref, kpe_ref, v_ref, o_ref, kcat):
    qi = pl.program_id(2)

    @pl.when(qi == 0)
    def _():
        kcat[:, :LANES] = kn_ref[...]
        kcat[:, LANES:] = kpe_ref[...]

    q = q_ref[...]

    def block(ki, carry, masked):
        m, l, acc = carry
        off = pl.multiple_of(ki * TQ, TQ)
        k = kcat[pl.ds(off, TQ), :]
        v = v_ref[pl.ds(off, TQ), :]
        s = lax.dot_general(q, k, _NT, preferred_element_type=F32)
        if masked:
            row = lax.broadcasted_iota(jnp.int32, (TQ, TQ), 0)
            col = lax.broadcasted_iota(jnp.int32, (TQ, TQ), 1)
            s = jnp.where(row >= col, s, NEG)
        m_new = jnp.maximum(m, jnp.max(s, axis=-1, keepdims=True))
        alpha = jnp.exp(m - m_new)
        p = jnp.exp(s - m_new)
        l = alpha * l + jnp.sum(p, axis=-1, keepdims=True)
        acc = alpha * acc + _dot(p.astype(BF16), v)
        return m_new, l, acc

    init = (jnp.full((TQ, 1), NEG, F32), jnp.zeros((TQ, 1), F32),
            jnp.zeros((TQ, V_DIM), F32))
    carry = lax.fori_loop(0, qi, lambda ki, c: block(ki, c, False), init)
    m, l, acc = block(qi, carry, True)
    o_ref[...] = (acc / l).astype(BF16)


def _sample_attn_kernel(n_pages, pt_ref, qa_ref, latn_ref, kpen_ref, wukt_ref,
                        lat_hbm, kpe_hbm, ctx_ref,
                        latbuf, kpebuf, latbf, sbuf, sems):
    n_seq = pl.num_programs(0)
    b = pl.program_id(0)
    slot = b % 2
    n_past = n_pages * PAGE_SIZE

    def page_copies(seq, slot_, j):
        pg = pt_ref[seq * n_pages + j]
        row = pl.multiple_of(j * PAGE_SIZE, PAGE_SIZE)
        return (
            pltpu.make_async_copy(lat_hbm.at[pg], latbuf.at[slot_, pl.ds(row, PAGE_SIZE)],
                                  sems.at[0, slot_]),
            pltpu.make_async_copy(kpe_hbm.at[pg], kpebuf.at[slot_, pl.ds(row, PAGE_SIZE)],
                                  sems.at[1, slot_]),
        )

    def start_fetch(seq, slot_):
        def body(j, _):
            for cp in page_copies(seq, slot_, j):
                cp.start()
            return 0
        lax.fori_loop(0, n_pages, body, 0)

    def wait_fetch(seq, slot_):
        def body(j, _):
            for cp in page_copies(seq, slot_, j):
                cp.wait()
            return 0
        lax.fori_loop(0, n_pages, body, 0)

    @pl.when(b == 0)
    def _():
        start_fetch(0, 0)

    @pl.when(b + 1 < n_seq)
    def _():
        start_fetch(b + 1, 1 - slot)

    wait_fetch(b, slot)

    qa = qa_ref[0]
    q_lat = qa[:, :KV_RANK]
    q_pe = qa[:, KV_RANK:KV_RANK + ROPE_DIM]
    wukt = wukt_ref[...]
    n_q = qa.shape[0]

    def scores(latc, kpec):
        n = latc.shape[0]
        e = lax.dot_general(wukt, latc, _NT, preferred_element_type=F32)
        e3 = e.reshape(N_HEADS, NOPE_DIM, n)
        r = lax.rsqrt(jnp.sum(e3 * e3, axis=1) * (1.0 / NOPE_DIM) + EPS)
        raw = lax.dot_general(q_lat, latc, _NT, preferred_element_type=F32)
        pe = lax.dot_general(q_pe, kpec, _NT, preferred_element_type=F32)
        raw3 = raw.reshape(n_q // N_HEADS, N_HEADS, n) * r[None]
        return raw3.reshape(n_q, n) + pe

    def chunk(c, _):
        off = pl.multiple_of(c * KEY_CHUNK, KEY_CHUNK)
        latc = latbuf[slot, pl.ds(off, KEY_CHUNK), :].astype(BF16)
        kpec = kpebuf[slot, pl.ds(off, KEY_CHUNK), :].astype(BF16)
        latbf[pl.ds(off, KEY_CHUNK), :] = latc
        sbuf[:, pl.ds(off, KEY_CHUNK)] = scores(latc, kpec)
        return 0

    lax.fori_loop(0, n_past // KEY_CHUNK, chunk, 0)

    pad = jnp.zeros((LANES - SUBLANES, KV_RANK), F32)
    latc = jnp.concatenate([latn_ref[0], pad], axis=0).astype(BF16)
    kpec = jnp.concatenate([kpen_ref[0], pad[:, :ROPE_DIM]], axis=0).astype(BF16)
    latbf[n_past:n_past + LANES, :] = latc
    s_new = scores(latc, kpec)
    t_q = lax.broadcasted_iota(jnp.int32, (n_q, LANES), 0) // N_HEADS
    j_k = lax.broadcasted_iota(jnp.int32, (n_q, LANES), 1)
    sbuf[:, n_past:n_past + LANES] = jnp.where(j_k <= t_q, s_new, NEG)

    s = sbuf[...]
    m = jnp.max(s, axis=-1, keepdims=True)
    p = jnp.exp(s - m)
    l = jnp.sum(p, axis=-1, keepdims=True)
    ctx = _dot(p.astype(BF16), latbf[...])
    ctx_ref[0] = ctx / l


def _absorb_kernel(q_ref, gkn_ref, wukt_ref, o_ref):
    gkn = gkn_ref[...]
    w = KV_RANK + LANES
    for h in range(N_HEADS):
        qn = (q_ref[:, 2 * h * LANES:(2 * h + 1) * LANES].astype(F32) * gkn).astype(BF16)
        o_ref[:, h * w:h * w + KV_RANK] = _dot(
            qn, wukt_ref[h * NOPE_DIM:(h + 1) * NOPE_DIM, :]).astype(BF16)
        o_ref[:, h * w + KV_RANK:(h + 1) * w] = q_ref[:, (2 * h + 1) * LANES:(2 * h + 2) * LANES]


def _ctx_out_kernel(c_ref, wuv_ref, o_ref):
    for h in range(N_HEADS):
        c = c_ref[:, h * KV_RANK:(h + 1) * KV_RANK].astype(BF16)
        o_ref[:, h * V_DIM:(h + 1) * V_DIM] = _dot(
            c, wuv_ref[:, h * V_DIM:(h + 1) * V_DIM]).astype(BF16)


def _params(n_axes=1):
    return pltpu.CompilerParams(dimension_semantics=("arbitrary",) * n_axes,
                                vmem_limit_bytes=VMEM_LIMIT)


def _row_spec(width):
    return pl.BlockSpec((TM, width), lambda i: (i, 0))


def _table_spec(n_rows):
    tiles = n_rows // TM
    return pl.BlockSpec((TM, LANES), lambda i: (i % tiles, 0))


def _ffn_specs():
    return [_const_spec((1, D_MODEL)), _const_spec((FF_CHUNKS, D_MODEL, 2 * FC)),
            _const_spec((FF_CHUNKS, FC, D_MODEL))]


def _ffn_call(x, g, wgu, wd):
    t = x.shape[0]
    return pl.pallas_call(
        _ffn_kernel, grid=(t // TM,),
        in_specs=[_row_spec(D_MODEL)] + _ffn_specs(),
        out_specs=_row_spec(D_MODEL),
        out_shape=jax.ShapeDtypeStruct((t, D_MODEL), F32),
        compiler_params=_params(), name="ffn")(x, g, wgu, wd)


def _mixer_call(x, g, win, cw, wout, tiles_per_seq, prev=None):
    t = x.shape[0]
    n_tiles = t // TM
    is_sample = prev is not None
    in_specs = [_row_spec(D_MODEL), _const_spec((1, D_MODEL)),
                _const_spec((D_MODEL, 3 * D_MODEL)), _const_spec((CONV_W, D_MODEL)),
                _const_spec((D_MODEL, D_MODEL))]
    args = [x, g, win, cw, wout]
    if is_sample:
        in_specs += [_row_spec(D_MODEL), _row_spec(D_MODEL)]
        args += list(prev)
        u_spec = _row_spec(D_MODEL)
        u_shape = jax.ShapeDtypeStruct((t, D_MODEL), F32)
    else:
        u_spec = pl.BlockSpec((1, SUBLANES, D_MODEL), lambda i: (i, 0, 0))
        u_shape = jax.ShapeDtypeStruct((n_tiles, SUBLANES, D_MODEL), F32)
    return pl.pallas_call(
        functools.partial(_mixer_kernel, tiles_per_seq, is_sample), grid=(n_tiles,),
        in_specs=in_specs,
        out_specs=[_row_spec(D_MODEL), u_spec],
        out_shape=[jax.ShapeDtypeStruct((t, D_MODEL), F32), u_shape],
        scratch_shapes=[pltpu.VMEM((TM + SUBLANES, D_MODEL), F32)],
        compiler_params=_params(), name="mixer")(*args)


def _ffn_kv_call(x, g, wgu, wd, gkv, wdkv, kvn, ga, gb, cos_t, sin_t, wukv, gkn):
    t = x.shape[0]
    in_specs = ([_row_spec(D_MODEL)] + _ffn_specs() + [
        _const_spec((1, D_MODEL)), _const_spec((D_MODEL, KV_RANK + 2 * LANES)),
        _const_spec((1, KV_RANK)), _const_spec((1, LANES)), _const_spec((1, LANES)),
        _table_spec(cos_t.shape[0]), _table_spec(sin_t.shape[0]),
        _const_spec((KV_RANK, 2 * D_MODEL)), _const_spec((1, NOPE_DIM))])
    out_specs = [_row_spec(D_MODEL), _row_spec(KV_RANK), _row_spec(ROPE_DIM),
                 _row_spec(LANES), _row_spec(D_MODEL), _row_spec(D_MODEL)]
    out_shape = [jax.ShapeDtypeStruct((t, D_MODEL), F32),
                 jax.ShapeDtypeStruct((t, KV_RANK), F32),
                 jax.ShapeDtypeStruct((t, ROPE_DIM), F32),
                 jax.ShapeDtypeStruct((t, LANES), BF16),
                 jax.ShapeDtypeStruct((t, D_MODEL), BF16),
                 jax.ShapeDtypeStruct((t, D_MODEL), BF16)]
    return pl.pallas_call(
        _ffn_kv_kernel, grid=(t // TM,), in_specs=in_specs, out_specs=out_specs,
        out_shape=out_shape, compiler_params=_params(), name="ffn_kv")(
            x, g, wgu, wd, gkv, wdkv, kvn, ga, gb, cos_t, sin_t, wukv, gkn)


def _ffn_q_call(x, g, wgu, wd, gmix, wdq, qnorm, wuq, gqn, ga, gb, cos_t, sin_t):
    t = x.shape[0]
    in_specs = ([_row_spec(D_MODEL)] + _ffn_specs() + [
        _const_spec((1, D_MODEL)), _const_spec((D_MODEL, Q_RANK)),
        _const_spec((1, Q_RANK)), _const_spec((Q_RANK, 3 * N_HEADS * LANES)),
        _const_spec((1, NOPE_DIM)), _const_spec((1, LANES)), _const_spec((1, LANES)),
        _table_spec(cos_t.shape[0]), _table_spec(sin_t.shape[0])])
    return pl.pallas_call(
        _ffn_q_kernel, grid=(t // TM,), in_specs=in_specs,
        out_specs=[_row_spec(D_MODEL), _row_spec(2 * N_HEADS * LANES)],
        out_shape=[jax.ShapeDtypeStruct((t, D_MODEL), F32),
                   jax.ShapeDtypeStruct((t, 2 * N_HEADS * LANES), BF16)],
        compiler_params=_params(), name="ffn_q")(
            x, g, wgu, wd, gmix, wdq, qnorm, wuq, gqn, ga, gb, cos_t, sin_t)


def _out_ffn_call(x, a, wo, g, wgu, wd):
    t = x.shape[0]
    return pl.pallas_call(
        _out_ffn_kernel, grid=(t // TM,),
        in_specs=[_row_spec(D_MODEL), _row_spec(D_MODEL),
                  _const_spec((D_MODEL, D_MODEL))] + _ffn_specs(),
        out_specs=_row_spec(D_MODEL),
        out_shape=jax.ShapeDtypeStruct((t, D_MODEL), F32),
        compiler_params=_params(), name="out_ffn")(x, a, wo, g, wgu, wd)


def _prompt_attn_call(q, kn, kpe, v, batch, seq):
    nq = seq // TQ
    return pl.pallas_call(
        _prompt_attn_kernel, grid=(batch, N_HEADS, nq),
        in_specs=[pl.BlockSpec((TQ, 2 * LANES), lambda b, h, i: (b * nq + i, h)),
                  pl.BlockSpec((seq, NOPE_DIM), lambda b, h, i: (b, h)),
                  pl.BlockSpec((seq, LANES), lambda b, h, i: (b, 0)),
                  pl.BlockSpec((seq, V_DIM), lambda b, h, i: (b, h))],
        out_specs=pl.BlockSpec((TQ, V_DIM), lambda b, h, i: (b * nq + i, h)),
        out_shape=jax.ShapeDtypeStruct((batch * seq, N_HEADS * V_DIM), BF16),
        scratch_shapes=[pltpu.VMEM((seq, 2 * LANES), BF16)],
        compiler_params=_params(3), name="prompt_attn")(q, kn, kpe, v)


def _sample_attn_call(page_table, qa, lat_new, kpe_new, wukt, cache_latent, cache_kpe):
    n_seq, n_pages = page_table.shape
    n_q = qa.shape[1]
    n_keys = n_pages * PAGE_SIZE
    grid_spec = pltpu.PrefetchScalarGridSpec(
        num_scalar_prefetch=1, grid=(n_seq,),
        in_specs=[pl.BlockSpec((1, n_q, KV_RANK + LANES), lambda b, pt: (b, 0, 0)),
                  pl.BlockSpec((1, SUBLANES, KV_RANK), lambda b, pt: (b, 0, 0)),
                  pl.BlockSpec((1, SUBLANES, ROPE_DIM), lambda b, pt: (b, 0, 0)),
                  pl.BlockSpec((N_HEADS * NOPE_DIM, KV_RANK), lambda b, pt: (0, 0),
                               pipeline_mode=pl.Buffered(1)),
                  pl.BlockSpec(memory_space=pl.ANY),
                  pl.BlockSpec(memory_space=pl.ANY)],
        out_specs=pl.BlockSpec((1, n_q, KV_RANK), lambda b, pt: (b, 0, 0)),
        scratch_shapes=[pltpu.VMEM((2, n_keys, KV_RANK), F32),
                        pltpu.VMEM((2, n_keys, ROPE_DIM), F32),
                        pltpu.VMEM((n_keys + LANES, KV_RANK), BF16),
                        pltpu.VMEM((n_q, n_keys + LANES), F32),
                        pltpu.SemaphoreType.DMA((2, 2))])
    return pl.pallas_call(
        functools.partial(_sample_attn_kernel, n_pages), grid_spec=grid_spec,
        out_shape=jax.ShapeDtypeStruct((n_seq, n_q, KV_RANK), F32),
        compiler_params=_params(), name="sample_attn")(
            page_table.reshape(-1), qa, lat_new, kpe_new, wukt, cache_latent, cache_kpe)


def _absorb_call(q, gkn, wukt):
    t = q.shape[0]
    return pl.pallas_call(
        _absorb_kernel, grid=(t // TM,),
        in_specs=[_row_spec(2 * N_HEADS * LANES), _const_spec((1, NOPE_DIM)),
                  _const_spec((N_HEADS * NOPE_DIM, KV_RANK))],
        out_specs=_row_spec(N_HEADS * (KV_RANK + LANES)),
        out_shape=jax.ShapeDtypeStruct((t, N_HEADS * (KV_RANK + LANES)), BF16),
        compiler_params=_params(), name="absorb")(q, gkn, wukt)


def _ctx_out_call(ctx, wuv):
    t = ctx.shape[0]
    return pl.pallas_call(
        _ctx_out_kernel, grid=(t // TM,),
        in_specs=[_row_spec(N_HEADS * KV_RANK),
                  _const_spec((KV_RANK, N_HEADS * V_DIM))],
        out_specs=_row_spec(N_HEADS * V_DIM),
        out_shape=jax.ShapeDtypeStruct((t, N_HEADS * V_DIM), BF16),
        compiler_params=_params(), name="ctx_out")(ctx, wuv)


def _rope_tables(pos):
    half = ROPE_DIM // 2
    freqs = ROPE_THETA ** (-jnp.arange(half, dtype=F32) / half)
    ang = pos.astype(F32)[:, None] * freqs
    cos = jnp.cos(ang)
    sin = jnp.sin(ang)
    zero = jnp.zeros((pos.shape[0], LANES - ROPE_DIM), F32)
    return (jnp.concatenate([cos, cos, zero], axis=1),
            jnp.concatenate([-sin, sin, zero], axis=1))


def _swap_halves(a):
    half = a.shape[-1] // 2
    return jnp.concatenate([a[..., half:], a[..., :half]], axis=-1)


def _pad_lanes(a):
    pad = [(0, 0)] * (a.ndim - 1) + [(0, LANES - a.shape[-1])]
    return jnp.pad(a, pad)


def _ffn_weights(w_gu, w_down):
    gate = w_gu[:, :D_FF].reshape(D_MODEL, FF_CHUNKS, FC)
    up = w_gu[:, D_FF:].reshape(D_MODEL, FF_CHUNKS, FC)
    wgu = jnp.concatenate([gate, up], axis=2).transpose(1, 0, 2).astype(BF16)
    return wgu, w_down.reshape(FF_CHUNKS, FC, D_MODEL).astype(BF16)


def kernel(x_prompt, x_sample, cache_latent, cache_kpe, state_conv, page_table, norm_ffn1, w_ffn1_gu, w_ffn1_down, norm_mix, norm_ffn2, w_ffn2_gu, w_ffn2_down, w_conv_in, conv_w, w_conv_out, w_dq, q_norm, w_uq, q_nope_norm, q_pe_norm, w_o, norm_kv_in, w_dkv, kv_norm, k_pe_norm, w_uk, w_uv, k_nope_norm):
    batch, seq, _ = x_prompt.shape
    n_dec, dec_seq, _ = x_sample.shape
    n_pages = page_table.shape[1]
    assert seq % TM == 0 and (n_dec * dec_seq) % TM == 0 and dec_seq == 4

    row = lambda a: a.reshape(1, -1)
    ffn1 = [_ffn_weights(w_ffn1_gu[l], w_ffn1_down[l]) for l in range(2)]
    ffn2 = [_ffn_weights(w_ffn2_gu[l], w_ffn2_down[l]) for l in range(2)]
    win = w_conv_in[0].astype(BF16)
    wout = w_conv_out[0].astype(BF16)
    w_pe = w_dkv[:, KV_RANK:]
    wdkv = jnp.concatenate([w_dkv[:, :KV_RANK], _pad_lanes(w_pe),
                            _pad_lanes(_swap_halves(w_pe))], axis=1).astype(BF16)
    wukv = jnp.concatenate([w_uk, w_uv], axis=1).astype(BF16)
    wukt = w_uk.T.astype(BF16)
    wuv = w_uv.astype(BF16)
    wdq = w_dq[0].astype(BF16)
    wuq3 = w_uq[0].reshape(Q_RANK, N_HEADS, NOPE_DIM + ROPE_DIM)
    wq_pe = wuq3[:, :, NOPE_DIM:]
    wuq = jnp.concatenate([
        wuq3[:, :, :NOPE_DIM].reshape(Q_RANK, -1),
        _pad_lanes(wq_pe).reshape(Q_RANK, -1),
        _pad_lanes(_swap_halves(wq_pe)).reshape(Q_RANK, -1)], axis=1).astype(BF16)
    wo = w_o[0].astype(BF16)
    kga = row(_pad_lanes(k_pe_norm))
    kgb = row(_pad_lanes(_swap_halves(k_pe_norm)))
    qga = row(_pad_lanes(q_pe_norm[0]))
    qgb = row(_pad_lanes(_swap_halves(q_pe_norm[0])))

    cos_p, sin_p = _rope_tables(jnp.arange(seq))
    pos_s = n_pages * PAGE_SIZE + jnp.arange(n_dec * dec_seq) % dec_seq
    cos_s, sin_s = _rope_tables(pos_s)

    prev = state_conv[0]
    zrow = jnp.zeros((n_dec, 1, D_MODEL), F32)
    p1 = jnp.concatenate([prev[:, 1:2], zrow, zrow, zrow], axis=1).reshape(-1, D_MODEL)
    p2 = jnp.concatenate([prev[:, 0:1], prev[:, 1:2], zrow, zrow], axis=1).reshape(-1, D_MODEL)

    def dense(x, tiles_per_seq, cos_t, sin_t, prev_rows):
        x = _ffn_call(x, row(norm_ffn1[0]), *ffn1[0])
        x, u = _mixer_call(x, row(norm_mix[0]), win, conv_w[0], wout, tiles_per_seq, prev_rows)
        x, lat, kpe, kpeb, kn, v = _ffn_kv_call(
            x, row(norm_ffn2[0]), *ffn2[0], row(norm_kv_in), wdkv, row(kv_norm),
            kga, kgb, cos_t, sin_t, wukv, row(k_nope_norm))
        x, q = _ffn_q_call(
            x, row(norm_ffn1[1]), *ffn1[1], row(norm_mix[1]), wdq, row(q_norm[0]), wuq,
            row(q_nope_norm[0]), qga, qgb, cos_t, sin_t)
        return x, u, lat, kpe, kpeb, kn, v, q

    xp, up, lat_p, kpe_p, kpeb_p, kn_p, v_p, q_p = dense(
        x_prompt.reshape(batch * seq, D_MODEL), seq // TM, cos_p, sin_p, None)
    o_p = _prompt_attn_call(q_p, kn_p, kpeb_p, v_p, batch, seq)
    y_p = _out_ffn_call(xp, o_p, wo, row(norm_ffn2[1]), *ffn2[1])
    conv_p = up.reshape(batch, seq // TM, SUBLANES, D_MODEL)[:, -1, SUBLANES - 2:][None]

    xs, us, lat_s, kpe_s, _, _, _, q_s = dense(
        x_sample.reshape(n_dec * dec_seq, D_MODEL), 1, cos_s, sin_s, (p1, p2))
    qa = _absorb_call(q_s, row(k_nope_norm), wukt)
    qa = qa.reshape(n_dec, dec_seq * N_HEADS, KV_RANK + LANES)
    pad_rows = ((0, 0), (0, SUBLANES - dec_seq), (0, 0))
    lat_new = jnp.pad(lat_s.reshape(n_dec, dec_seq, KV_RANK), pad_rows)
    kpe_new = jnp.pad(kpe_s.reshape(n_dec, dec_seq, ROPE_DIM), pad_rows)
    ctx = _sample_attn_call(page_table, qa, lat_new, kpe_new, wukt, cache_latent, cache_kpe)
    o_s = _ctx_out_call(ctx.reshape(n_dec * dec_seq, N_HEADS * KV_RANK), wuv)
    y_s = _out_ffn_call(xs, o_s, wo, row(norm_ffn2[1]), *ffn2[1])
    conv_s = us.reshape(n_dec, dec_seq, D_MODEL)[:, dec_seq - (CONV_W - 1):][None]

    return (y_p.reshape(batch, seq, D_MODEL),
            y_s.reshape(n_dec, dec_seq, D_MODEL),
            conv_p, conv_s,
            lat_p.reshape(batch, seq, KV_RANK),
            kpe_p.reshape(batch, seq, ROPE_DIM),
            lat_s.reshape(n_dec, dec_seq, KV_RANK),
            kpe_s.reshape(n_dec, dec_seq, ROPE_DIM))
```

```python
import functools

import jax
import jax.numpy as jnp
from jax import lax
from jax.experimental import pallas as pl
from jax.experimental.pallas import tpu as pltpu

F32 = jnp.float32
BF16 = jnp.bfloat16

D_MODEL = 1024
D_FF = 2816
N_HEADS = 8
NOPE_DIM = 128
ROPE_DIM = 64
V_DIM = 128
Q_RANK = 512
KV_RANK = 256
CONV_W = 3
PAGE_SIZE = 128
ROPE_THETA = 10000.0
EPS = 1e-6
SCALE = (NOPE_DIM + ROPE_DIM) ** -0.5
LOG2E = 1.4426950408889634
Q_SCALE = SCALE * LOG2E
V_ROWS = V_DIM + 16
NEG = -1e30

LANES = 128
SUBLANES = 8
TM = 512
FF_CHUNKS = 2
FC = D_FF // FF_CHUNKS
TQ = 512
HEADS_PER_STEP = 2
KEY_CHUNK = 1024
VMEM_LIMIT = 56 * 1024 * 1024

_NT = (((1,), (1,)), ((), ()))


def _const_spec(shape):
    zeros = (0,) * len(shape)
    return pl.BlockSpec(shape, lambda *_: zeros, pipeline_mode=pl.Buffered(1))


def _rms(x, g):
    ms = jnp.mean(x * x, axis=-1, keepdims=True)
    return x * lax.rsqrt(ms + EPS) * g


def _dot(a, b):
    return jnp.dot(a, b, preferred_element_type=F32)


def _ffn(x, g, wgu_ref, wd_ref):
    xn = _rms(x, g).astype(BF16)
    acc = None
    for c in range(FF_CHUNKS):
        h = _dot(xn, wgu_ref[c])
        gate = h[:, :FC]
        up = h[:, FC:]
        act = (gate * jax.nn.sigmoid(gate) * up).astype(BF16)
        part = _dot(act, wd_ref[c])
        acc = part if acc is None else acc + part
    return x + 0.5 * acc


def _rope_norm(a, b, ga, gb, cos_t, sin_t):
    ms = jnp.sum(a * a, axis=-1, keepdims=True) * (1.0 / ROPE_DIM)
    return (a * ga * cos_t + b * gb * sin_t) * lax.rsqrt(ms + EPS)


def _ffn_kernel(x_ref, g_ref, wgu_ref, wd_ref, o_ref):
    o_ref[...] = _ffn(x_ref[...], g_ref[...], wgu_ref, wd_ref)


def _mixer_kernel(tiles_per_seq, is_sample, *refs):
    if is_sample:
        (x_ref, g_ref, win_ref, cw_ref, wout_ref, p1_ref, p2_ref,
         o_ref, u_ref, ubuf) = refs
    else:
        x_ref, g_ref, win_ref, cw_ref, wout_ref, o_ref, u_ref, ubuf = refs
    i = pl.program_id(0)
    x = x_ref[...]
    xn = _rms(x, g_ref[...]).astype(BF16)
    bch = _dot(xn, win_ref[...])
    b = bch[:, :D_MODEL]
    u = bch[:, D_MODEL:2 * D_MODEL] * bch[:, 2 * D_MODEL:]

    @pl.when(i % tiles_per_seq == 0)
    def _():
        ubuf[0:SUBLANES, :] = jnp.zeros((SUBLANES, D_MODEL), F32)

    ubuf[SUBLANES:SUBLANES + TM, :] = u
    s1 = ubuf[SUBLANES - 1:SUBLANES - 1 + TM, :]
    s2 = ubuf[SUBLANES - 2:SUBLANES - 2 + TM, :]
    if is_sample:
        t = lax.broadcasted_iota(jnp.int32, (TM, 1), 0) % 4
        s1 = jnp.where(t >= 1, s1, p1_ref[...])
        s2 = jnp.where(t >= 2, s2, p2_ref[...])
    cw = cw_ref[...]
    y = cw[0:1, :] * s2 + cw[1:2, :] * s1 + cw[2:3, :] * u
    o_ref[...] = x + _dot((b * y).astype(BF16), wout_ref[...])
    if is_sample:
        u_ref[...] = u
    else:
        ubuf[0:SUBLANES, :] = u[TM - SUBLANES:, :]
        u_ref[0] = u[TM - SUBLANES:, :]


def _ffn_kv_kernel(x_ref, g_ref, wgu_ref, wd_ref, gkv_ref, wdkv_ref, kvn_ref,
                   ga_ref, gb_ref, cos_ref, sin_ref, wuk_ref, wuvt_ref, gkn_ref,
                   o_ref, lat_ref, kpe_ref, kpeb_ref, kn_ref, vt_ref):
    x = _ffn(x_ref[...], g_ref[...], wgu_ref, wd_ref)
    o_ref[...] = x
    hk = _rms(x, gkv_ref[...]).astype(BF16)
    ckv = _dot(hk, wdkv_ref[...])
    lat = _rms(ckv[:, :KV_RANK], kvn_ref[...])
    lat_ref[...] = lat
    kpe = _rope_norm(ckv[:, KV_RANK:KV_RANK + LANES], ckv[:, KV_RANK + LANES:],
                     ga_ref[...], gb_ref[...], cos_ref[...], sin_ref[...])
    kpe_ref[...] = kpe[:, :ROPE_DIM]
    kpeb_ref[...] = kpe.astype(BF16)
    latb = lat.astype(BF16)
    ek = _dot(latb, wuk_ref[...])
    gkn = gkn_ref[...]
    for h in range(N_HEADS):
        e = ek[:, h * NOPE_DIM:(h + 1) * NOPE_DIM]
        kn_ref[:, h * NOPE_DIM:(h + 1) * NOPE_DIM] = _rms(e, gkn).astype(BF16)
    vt = lax.dot_general(wuvt_ref[...], latb, _NT, preferred_element_type=F32).astype(BF16)
    ones = jnp.ones((V_ROWS - V_DIM, vt.shape[1]), BF16)
    for h in range(N_HEADS):
        vt_ref[h * V_ROWS:h * V_ROWS + V_DIM, :] = vt[h * V_DIM:(h + 1) * V_DIM, :]
        vt_ref[h * V_ROWS + V_DIM:(h + 1) * V_ROWS, :] = ones


def _ffn_q_kernel(x_ref, g_ref, wgu_ref, wd_ref, gmix_ref, wdq_ref, qn_ref,
                  wuq_ref, gqn_ref, ga_ref, gb_ref, cos_ref, sin_ref,
                  o_ref, q_ref):
    x = _ffn(x_ref[...], g_ref[...], wgu_ref, wd_ref)
    o_ref[...] = x
    xn = _rms(x, gmix_ref[...]).astype(BF16)
    cq = _rms(_dot(xn, wdq_ref[...]), qn_ref[...]).astype(BF16)
    q = _dot(cq, wuq_ref[...])
    gqn = gqn_ref[...]
    ga = ga_ref[...]
    gb = gb_ref[...]
    cos_t = cos_ref[...]
    sin_t = sin_ref[...]
    hw = N_HEADS * LANES
    for h in range(N_HEADS):
        qn = _rms(q[:, h * LANES:(h + 1) * LANES], gqn) * Q_SCALE
        qp = _rope_norm(q[:, hw + h * LANES:hw + (h + 1) * LANES],
                        q[:, 2 * hw + h * LANES:2 * hw + (h + 1) * LANES],
                        ga, gb, cos_t, sin_t) * Q_SCALE
        q_ref[:, 2 * h * LANES:(2 * h + 1) * LANES] = qn.astype(BF16)
        q_ref[:, (2 * h + 1) * LANES:(2 * h + 2) * LANES] = qp.astype(BF16)


def _out_ffn_kernel(x_ref, a_ref, wo_ref, g_ref, wgu_ref, wd_ref, o_ref):
    x = x_ref[...] + _dot(a_ref[...], wo_ref[...])
    o_ref[...] = _ffn(x, g_ref[...], wgu_ref, wd_ref)


def _prompt_attn_kernel(q_ref, kn_ref, kpe_ref, vt_ref, o_ref, kcat):
    qi = pl.program_id(2)

    @pl.when(qi == 0)
    def _():
        for j in range(HEADS_PER_STEP):
            kcat[j, :, :LANES] = kn_ref[:, j * NOPE_DIM:(j + 1) * NOPE_DIM]
            kcat[j, :, LANES:] = kpe_ref[...]

    qs = [q_ref[:, 2 * j * LANES:2 * (j + 1) * LANES] for j in range(HEADS_PER_STEP)]

    def scores(ki, masked):
        off = pl.multiple_of(ki * TQ, TQ)
        out = []
        for j in range(HEADS_PER_STEP):
            s = lax.dot_general(kcat[j, pl.ds(off, TQ), :], qs[j], _NT,
                                preferred_element_type=F32)
            if masked:
                key = lax.broadcasted_iota(jnp.int32, (TQ, TQ), 0)
                qry = lax.broadcasted_iota(jnp.int32, (TQ, TQ), 1)
                s = jnp.where(qry >= key, s, NEG)
            out.append((s, jnp.max(s, axis=0, keepdims=True)))
        return tuple(out)

    def accumulate(ki, stats, sc):
        off = pl.multiple_of(ki * TQ, TQ)
        out = []
        for j in range(HEADS_PER_STEP):
            m, acc = stats[j]
            s, smax = sc[j]
            vt = vt_ref[j * V_ROWS:(j + 1) * V_ROWS, pl.ds(off, TQ)]
            m_new = jnp.maximum(m, smax)
            alpha = jnp.exp2(m - m_new)
            p = jnp.exp2((s - m_new).astype(BF16))
            acc = alpha * acc + _dot(vt, p)
            out.append((m_new, acc))
        return tuple(out)

    def body(ki, carry):
        stats, sc = carry
        sc_next = scores(ki, False)
        prev = jnp.where(ki == 0, qi, ki - 1)
        return accumulate(prev, stats, sc), sc_next

    init = tuple((jnp.full((1, TQ), NEG, F32), jnp.zeros((V_ROWS, TQ), F32))
                 for _ in range(HEADS_PER_STEP))
    stats, sc = lax.fori_loop(0, qi, body, (init, scores(qi, True)))
    stats = accumulate(jnp.maximum(qi - 1, 0), stats, sc)
    for j in range(HEADS_PER_STEP):
        acc = stats[j][1]
        o = acc[:V_DIM] / acc[V_DIM:V_DIM + 1]
        o_ref[:, j * V_DIM:(j + 1) * V_DIM] = o.T.astype(BF16)


def _sample_attn_kernel(n_pages, pt_ref, qa_ref, latn_ref, kpen_ref, wukt_ref,
                        lat_hbm, kpet_hbm, ctx_ref,
                        latbuf, kpebuf, latbf, sbuf, wq, sems):
    n_seq = pl.num_programs(0)
    b = pl.program_id(0)
    slot = b % 2
    n_past = n_pages * PAGE_SIZE
    n_chunks = n_past // KEY_CHUNK
    pages_per_chunk = n_pages // n_chunks
    n_w = N_HEADS * NOPE_DIM

    def page_copies(seq, slot_, j):
        pg = pt_ref[seq * n_pages + j]
        row = pl.multiple_of(j * PAGE_SIZE, PAGE_SIZE)
        return (
            pltpu.make_async_copy(lat_hbm.at[pg], latbuf.at[slot_, pl.ds(row, PAGE_SIZE)],
                                  sems.at[0, slot_]),
            pltpu.make_async_copy(kpet_hbm.at[pg], kpebuf.at[slot_, :, pl.ds(row, PAGE_SIZE)],
                                  sems.at[1, slot_]),
        )

    def start_pages(seq, slot_, first, count):
        for j in range(count):
            for cp in page_copies(seq, slot_, first + j):
                cp.start()

    def wait_fetch(seq, slot_):
        def body(j, _):
            for cp in page_copies(seq, slot_, j):
                cp.wait()
            return 0
        lax.fori_loop(0, n_pages, body, 0)

    @pl.when(b == 0)
    def _():
        def first_fetch(c, _):
            start_pages(0, 0, c * pages_per_chunk, pages_per_chunk)
            return 0
        lax.fori_loop(0, n_chunks, first_fetch, 0)
        wq[0:n_w, :] = wukt_ref[...]

    wait_fetch(b, slot)

    qa = qa_ref[0]
    n_q = qa.shape[0]
    wq[n_w:n_w + n_q, :] = qa[:, :KV_RANK]
    q_pe = qa[:, KV_RANK:KV_RANK + ROPE_DIM]
    wq_all = wq[...]

    def scores(latc, pe):
        n = latc.shape[0]
        e = lax.dot_general(wq_all, latc, _NT, preferred_element_type=F32)
        e3 = e[:n_w].reshape(N_HEADS, NOPE_DIM, n)
        r = lax.rsqrt(jnp.sum(e3 * e3, axis=1) * (1.0 / NOPE_DIM) + EPS)
        raw3 = e[n_w:].reshape(n_q // N_HEADS, N_HEADS, n) * r[None]
        return raw3.reshape(n_q, n) + pe

    def chunk(c, _):
        @pl.when(b + 1 < n_seq)
        def _():
            start_pages(b + 1, 1 - slot, c * pages_per_chunk, pages_per_chunk)

        off = pl.multiple_of(c * KEY_CHUNK, KEY_CHUNK)
        latc = latbuf[slot, pl.ds(off, KEY_CHUNK), :].astype(BF16)
        kpec = kpebuf[slot, :, pl.ds(off, KEY_CHUNK)].astype(BF16)
        latbf[pl.ds(off, KEY_CHUNK), :] = latc
        sbuf[:, pl.ds(off, KEY_CHUNK)] = scores(latc, _dot(q_pe, kpec))
        return 0

    lax.fori_loop(0, n_chunks, chunk, 0, unroll=2)

    pad = jnp.zeros((LANES - SUBLANES, KV_RANK), F32)
    latc = jnp.concatenate([latn_ref[0], pad], axis=0).astype(BF16)
    kpec = jnp.concatenate([kpen_ref[0], pad[:, :ROPE_DIM]], axis=0).astype(BF16)
    latbf[n_past:n_past + LANES, :] = latc
    s_new = scores(latc, lax.dot_general(q_pe, kpec, _NT, preferred_element_type=F32))
    t_q = lax.broadcasted_iota(jnp.int32, (n_q, LANES), 0) // N_HEADS
    j_k = lax.broadcasted_iota(jnp.int32, (n_q, LANES), 1)
    sbuf[:, n_past:n_past + LANES] = jnp.where(j_k <= t_q, s_new, NEG)

    s = sbuf[...]
    m = jnp.max(s, axis=-1, keepdims=True)
    p = jnp.exp2(s - m)
    l = jnp.sum(p, axis=-1, keepdims=True)
    ctx = _dot(p.astype(BF16), latbf[...])
    ctx_ref[0] = ctx / l


def _absorb_kernel(q_ref, gkn_ref, wukt_ref, o_ref):
    gkn = gkn_ref[...]
    w = KV_RANK + LANES
    for h in range(N_HEADS):
        qn = (q_ref[:, 2 * h * LANES:(2 * h + 1) * LANES].astype(F32) * gkn).astype(BF16)
        o_ref[:, h * w:h * w + KV_RANK] = _dot(
            qn, wukt_ref[h * NOPE_DIM:(h + 1) * NOPE_DIM, :]).astype(BF16)
        o_ref[:, h * w + KV_RANK:(h + 1) * w] = q_ref[:, (2 * h + 1) * LANES:(2 * h + 2) * LANES]


def _ctx_out_kernel(c_ref, wuv_ref, o_ref):
    for h in range(N_HEADS):
        c = c_ref[:, h * KV_RANK:(h + 1) * KV_RANK].astype(BF16)
        o_ref[:, h * V_DIM:(h + 1) * V_DIM] = _dot(
            c, wuv_ref[:, h * V_DIM:(h + 1) * V_DIM]).astype(BF16)


def _params(n_axes=1):
    return pltpu.CompilerParams(dimension_semantics=("arbitrary",) * n_axes,
                                vmem_limit_bytes=VMEM_LIMIT)


def _row_spec(width):
    return pl.BlockSpec((TM, width), lambda i: (i, 0))


def _table_spec(n_rows):
    tiles = n_rows // TM
    return pl.BlockSpec((TM, LANES), lambda i: (i % tiles, 0))


def _ffn_specs():
    return [_const_spec((1, D_MODEL)), _const_spec((FF_CHUNKS, D_MODEL, 2 * FC)),
            _const_spec((FF_CHUNKS, FC, D_MODEL))]


def _ffn_call(x, g, wgu, wd):
    t = x.shape[0]
    return pl.pallas_call(
        _ffn_kernel, grid=(t // TM,),
        in_specs=[_row_spec(D_MODEL)] + _ffn_specs(),
        out_specs=_row_spec(D_MODEL),
        out_shape=jax.ShapeDtypeStruct((t, D_MODEL), F32),
        compiler_params=_params(), name="ffn")(x, g, wgu, wd)


def _mixer_call(x, g, win, cw, wout, tiles_per_seq, prev=None):
    t = x.shape[0]
    n_tiles = t // TM
    is_sample = prev is not None
    in_specs = [_row_spec(D_MODEL), _const_spec((1, D_MODEL)),
                _const_spec((D_MODEL, 3 * D_MODEL)), _const_spec((CONV_W, D_MODEL)),
                _const_spec((D_MODEL, D_MODEL))]
    args = [x, g, win, cw, wout]
    if is_sample:
        in_specs += [_row_spec(D_MODEL), _row_spec(D_MODEL)]
        args += list(prev)
        u_spec = _row_spec(D_MODEL)
        u_shape = jax.ShapeDtypeStruct((t, D_MODEL), F32)
    else:
        u_spec = pl.BlockSpec((1, SUBLANES, D_MODEL), lambda i: (i, 0, 0))
        u_shape = jax.ShapeDtypeStruct((n_tiles, SUBLANES, D_MODEL), F32)
    return pl.pallas_call(
        functools.partial(_mixer_kernel, tiles_per_seq, is_sample), grid=(n_tiles,),
        in_specs=in_specs,
        out_specs=[_row_spec(D_MODEL), u_spec],
        out_shape=[jax.ShapeDtypeStruct((t, D_MODEL), F32), u_shape],
        scratch_shapes=[pltpu.VMEM((TM + SUBLANES, D_MODEL), F32)],
        compiler_params=_params(), name="mixer")(*args)


def _ffn_kv_call(x, g, wgu, wd, gkv, wdkv, kvn, ga, gb, cos_t, sin_t, wuk, wuvt, gkn):
    t = x.shape[0]
    in_specs = ([_row_spec(D_MODEL)] + _ffn_specs() + [
        _const_spec((1, D_MODEL)), _const_spec((D_MODEL, KV_RANK + 2 * LANES)),
        _const_spec((1, KV_RANK)), _const_spec((1, LANES)), _const_spec((1, LANES)),
        _table_spec(cos_t.shape[0]), _table_spec(sin_t.shape[0]),
        _const_spec((KV_RANK, D_MODEL)), _const_spec((D_MODEL, KV_RANK)),
        _const_spec((1, NOPE_DIM))])
    out_specs = [_row_spec(D_MODEL), _row_spec(KV_RANK), _row_spec(ROPE_DIM),
                 _row_spec(LANES), _row_spec(D_MODEL),
                 pl.BlockSpec((N_HEADS * V_ROWS, TM), lambda i: (0, i))]
    out_shape = [jax.ShapeDtypeStruct((t, D_MODEL), F32),
                 jax.ShapeDtypeStruct((t, KV_RANK), F32),
                 jax.ShapeDtypeStruct((t, ROPE_DIM), F32),
                 jax.ShapeDtypeStruct((t, LANES), BF16),
                 jax.ShapeDtypeStruct((t, D_MODEL), BF16),
                 jax.ShapeDtypeStruct((N_HEADS * V_ROWS, t), BF16)]
    return pl.pallas_call(
        _ffn_kv_kernel, grid=(t // TM,), in_specs=in_specs, out_specs=out_specs,
        out_shape=out_shape, compiler_params=_params(), name="ffn_kv")(
            x, g, wgu, wd, gkv, wdkv, kvn, ga, gb, cos_t, sin_t, wuk, wuvt, gkn)


def _ffn_q_call(x, g, wgu, wd, gmix, wdq, qnorm, wuq, gqn, ga, gb, cos_t, sin_t):
    t = x.shape[0]
    in_specs = ([_row_spec(D_MODEL)] + _ffn_specs() + [
        _const_spec((1, D_MODEL)), _const_spec((D_MODEL, Q_RANK)),
        _const_spec((1, Q_RANK)), _const_spec((Q_RANK, 3 * N_HEADS * LANES)),
        _const_spec((1, NOPE_DIM)), _const_spec((1, LANES)), _const_spec((1, LANES)),
        _table_spec(cos_t.shape[0]), _table_spec(sin_t.shape[0])])
    return pl.pallas_call(
        _ffn_q_kernel, grid=(t // TM,), in_specs=in_specs,
        out_specs=[_row_spec(D_MODEL), _row_spec(2 * N_HEADS * LANES)],
        out_shape=[jax.ShapeDtypeStruct((t, D_MODEL), F32),
                   jax.ShapeDtypeStruct((t, 2 * N_HEADS * LANES), BF16)],
        compiler_params=_params(), name="ffn_q")(
            x, g, wgu, wd, gmix, wdq, qnorm, wuq, gqn, ga, gb, cos_t, sin_t)


def _out_ffn_call(x, a, wo, g, wgu, wd):
    t = x.shape[0]
    return pl.pallas_call(
        _out_ffn_kernel, grid=(t // TM,),
        in_specs=[_row_spec(D_MODEL), _row_spec(D_MODEL),
                  _const_spec((D_MODEL, D_MODEL))] + _ffn_specs(),
        out_specs=_row_spec(D_MODEL),
        out_shape=jax.ShapeDtypeStruct((t, D_MODEL), F32),
        compiler_params=_params(), name="out_ffn")(x, a, wo, g, wgu, wd)


def _prompt_attn_call(q, kn, kpe, vt, batch, seq):
    nq = seq // TQ
    hps = HEADS_PER_STEP
    return pl.pallas_call(
        _prompt_attn_kernel, grid=(batch, N_HEADS // hps, nq),
        in_specs=[pl.BlockSpec((TQ, hps * 2 * LANES), lambda b, h, i: (b * nq + i, h)),
                  pl.BlockSpec((seq, hps * NOPE_DIM), lambda b, h, i: (b, h)),
                  pl.BlockSpec((seq, LANES), lambda b, h, i: (b, 0)),
                  pl.BlockSpec((hps * V_ROWS, seq), lambda b, h, i: (h, b))],
        out_specs=pl.BlockSpec((TQ, hps * V_DIM), lambda b, h, i: (b * nq + i, h)),
        out_shape=jax.ShapeDtypeStruct((batch * seq, N_HEADS * V_DIM), BF16),
        scratch_shapes=[pltpu.VMEM((hps, seq, 2 * LANES), BF16)],
        compiler_params=_params(3), name="prompt_attn")(q, kn, kpe, vt)


def _sample_attn_call(page_table, qa, lat_new, kpe_new, wukt, cache_latent, cache_kpe):
    n_seq, n_pages = page_table.shape
    n_q = qa.shape[1]
    n_keys = n_pages * PAGE_SIZE
    grid_spec = pltpu.PrefetchScalarGridSpec(
        num_scalar_prefetch=1, grid=(n_seq,),
        in_specs=[pl.BlockSpec((1, n_q, KV_RANK + LANES), lambda b, pt: (b, 0, 0)),
                  pl.BlockSpec((1, SUBLANES, KV_RANK), lambda b, pt: (b, 0, 0)),
                  pl.BlockSpec((1, SUBLANES, ROPE_DIM), lambda b, pt: (b, 0, 0)),
                  pl.BlockSpec((N_HEADS * NOPE_DIM, KV_RANK), lambda b, pt: (0, 0),
                               pipeline_mode=pl.Buffered(1)),
                  pl.BlockSpec(memory_space=pl.ANY),
                  pl.BlockSpec(memory_space=pl.ANY)],
        out_specs=pl.BlockSpec((1, n_q, KV_RANK), lambda b, pt: (b, 0, 0)),
        scratch_shapes=[pltpu.VMEM((2, n_keys, KV_RANK), F32),
                        pltpu.VMEM((2, ROPE_DIM, n_keys), F32),
                        pltpu.VMEM((n_keys + LANES, KV_RANK), BF16),
                        pltpu.VMEM((n_q, n_keys + LANES), F32),
                        pltpu.VMEM((N_HEADS * NOPE_DIM + n_q, KV_RANK), BF16),
                        pltpu.SemaphoreType.DMA((2, 2))])
    return pl.pallas_call(
        functools.partial(_sample_attn_kernel, n_pages), grid_spec=grid_spec,
        out_shape=jax.ShapeDtypeStruct((n_seq, n_q, KV_RANK), F32),
        compiler_params=_params(), name="sample_attn")(
            page_table.reshape(-1), qa, lat_new, kpe_new, wukt, cache_latent,
            jnp.swapaxes(cache_kpe, 1, 2))


def _absorb_call(q, gkn, wukt):
    t = q.shape[0]
    return pl.pallas_call(
        _absorb_kernel, grid=(t // TM,),
        in_specs=[_row_spec(2 * N_HEADS * LANES), _const_spec((1, NOPE_DIM)),
                  _const_spec((N_HEADS * NOPE_DIM, KV_RANK))],
        out_specs=_row_spec(N_HEADS * (KV_RANK + LANES)),
        out_shape=jax.ShapeDtypeStruct((t, N_HEADS * (KV_RANK + LANES)), BF16),
        compiler_params=_params(), name="absorb")(q, gkn, wukt)


def _ctx_out_call(ctx, wuv):
    t = ctx.shape[0]
    return pl.pallas_call(
        _ctx_out_kernel, grid=(t // TM,),
        in_specs=[_row_spec(N_HEADS * KV_RANK),
                  _const_spec((KV_RANK, N_HEADS * V_DIM))],
        out_specs=_row_spec(N_HEADS * V_DIM),
        out_shape=jax.ShapeDtypeStruct((t, N_HEADS * V_DIM), BF16),
        compiler_params=_params(), name="ctx_out")(ctx, wuv)


def _rope_tables(pos):
    half = ROPE_DIM // 2
    freqs = ROPE_THETA ** (-jnp.arange(half, dtype=F32) / half)
    ang = pos.astype(F32)[:, None] * freqs
    cos = jnp.cos(ang)
    sin = jnp.sin(ang)
    zero = jnp.zeros((pos.shape[0], LANES - ROPE_DIM), F32)
    return (jnp.concatenate([cos, cos, zero], axis=1),
            jnp.concatenate([-sin, sin, zero], axis=1))


def _swap_halves(a):
    half = a.shape[-1] // 2
    return jnp.concatenate([a[..., half:], a[..., :half]], axis=-1)


def _pad_lanes(a):
    pad = [(0, 0)] * (a.ndim - 1) + [(0, LANES - a.shape[-1])]
    return jnp.pad(a, pad)


def _ffn_weights(w_gu, w_down):
    gate = w_gu[:, :D_FF].reshape(D_MODEL, FF_CHUNKS, FC)
    up = w_gu[:, D_FF:].reshape(D_MODEL, FF_CHUNKS, FC)
    wgu = jnp.concatenate([gate, up], axis=2).transpose(1, 0, 2).astype(BF16)
    return wgu, w_down.reshape(FF_CHUNKS, FC, D_MODEL).astype(BF16)


def kernel(x_prompt, x_sample, cache_latent, cache_kpe, state_conv, page_table, norm_ffn1, w_ffn1_gu, w_ffn1_down, norm_mix, norm_ffn2, w_ffn2_gu, w_ffn2_down, w_conv_in, conv_w, w_conv_out, w_dq, q_norm, w_uq, q_nope_norm, q_pe_norm, w_o, norm_kv_in, w_dkv, kv_norm, k_pe_norm, w_uk, w_uv, k_nope_norm):
    batch, seq, _ = x_prompt.shape
    n_dec, dec_seq, _ = x_sample.shape
    n_pages = page_table.shape[1]
    assert seq % TM == 0 and (n_dec * dec_seq) % TM == 0 and dec_seq == 4

    row = lambda a: a.reshape(1, -1)
    ffn1 = [_ffn_weights(w_ffn1_gu[l], w_ffn1_down[l]) for l in range(2)]
    ffn2 = [_ffn_weights(w_ffn2_gu[l], w_ffn2_down[l]) for l in range(2)]
    win = w_conv_in[0].astype(BF16)
    wout = w_conv_out[0].astype(BF16)
    w_pe = w_dkv[:, KV_RANK:]
    wdkv = jnp.concatenate([w_dkv[:, :KV_RANK], _pad_lanes(w_pe),
                            _pad_lanes(_swap_halves(w_pe))], axis=1).astype(BF16)
    wuk = w_uk.astype(BF16)
    wuvt = w_uv.T.astype(BF16)
    wukt = w_uk.T.astype(BF16)
    wuv = w_uv.astype(BF16)
    wdq = w_dq[0].astype(BF16)
    wuq3 = w_uq[0].reshape(Q_RANK, N_HEADS, NOPE_DIM + ROPE_DIM)
    wq_pe = wuq3[:, :, NOPE_DIM:]
    wuq = jnp.concatenate([
        wuq3[:, :, :NOPE_DIM].reshape(Q_RANK, -1),
        _pad_lanes(wq_pe).reshape(Q_RANK, -1),
        _pad_lanes(_swap_halves(wq_pe)).reshape(Q_RANK, -1)], axis=1).astype(BF16)
    wo = w_o[0].astype(BF16)
    kga = row(_pad_lanes(k_pe_norm))
    kgb = row(_pad_lanes(_swap_halves(k_pe_norm)))
    qga = row(_pad_lanes(q_pe_norm[0]))
    qgb = row(_pad_lanes(_swap_halves(q_pe_norm[0])))

    cos_p, sin_p = _rope_tables(jnp.arange(seq))
    pos_s = n_pages * PAGE_SIZE + jnp.arange(n_dec * dec_seq) % dec_seq
    cos_s, sin_s = _rope_tables(pos_s)

    prev = state_conv[0]
    zrow = jnp.zeros((n_dec, 1, D_MODEL), F32)
    p1 = jnp.concatenate([prev[:, 1:2], zrow, zrow, zrow], axis=1).reshape(-1, D_MODEL)
    p2 = jnp.concatenate([prev[:, 0:1], prev[:, 1:2], zrow, zrow], axis=1).reshape(-1, D_MODEL)

    def dense(x, tiles_per_seq, cos_t, sin_t, prev_rows):
        x = _ffn_call(x, row(norm_ffn1[0]), *ffn1[0])
        x, u = _mixer_call(x, row(norm_mix[0]), win, conv_w[0], wout, tiles_per_seq, prev_rows)
        x, lat, kpe, kpeb, kn, v = _ffn_kv_call(
            x, row(norm_ffn2[0]), *ffn2[0], row(norm_kv_in), wdkv, row(kv_norm),
            kga, kgb, cos_t, sin_t, wuk, wuvt, row(k_nope_norm))
        x, q = _ffn_q_call(
            x, row(norm_ffn1[1]), *ffn1[1], row(norm_mix[1]), wdq, row(q_norm[0]), wuq,
            row(q_nope_norm[0]), qga, qgb, cos_t, sin_t)
        return x, u, lat, kpe, kpeb, kn, v, q

    xp, up, lat_p, kpe_p, kpeb_p, kn_p, v_p, q_p = dense(
        x_prompt.reshape(batch * seq, D_MODEL), seq // TM, cos_p, sin_p, None)
    o_p = _prompt_attn_call(q_p, kn_p, kpeb_p, v_p, batch, seq)
    y_p = _out_ffn_call(xp, o_p, wo, row(norm_ffn2[1]), *ffn2[1])
    conv_p = up.reshape(batch, seq // TM, SUBLANES, D_MODEL)[:, -1, SUBLANES - 2:][None]

    xs, us, lat_s, kpe_s, _, _, _, q_s = dense(
        x_sample.reshape(n_dec * dec_seq, D_MODEL), 1, cos_s, sin_s, (p1, p2))
    qa = _absorb_call(q_s, row(k_nope_norm), wukt)
    qa = qa.reshape(n_dec, dec_seq * N_HEADS, KV_RANK + LANES)
    pad_rows = ((0, 0), (0, SUBLANES - dec_seq), (0, 0))
    lat_new = jnp.pad(lat_s.reshape(n_dec, dec_seq, KV_RANK), pad_rows)
    kpe_new = jnp.pad(kpe_s.reshape(n_dec, dec_seq, ROPE_DIM), pad_rows)
    ctx = _sample_attn_call(page_table, qa, lat_new, kpe_new, wukt, cache_latent, cache_kpe)
    o_s = _ctx_out_call(ctx.reshape(n_dec * dec_seq, N_HEADS * KV_RANK), wuv)
    y_s = _out_ffn_call(xs, o_s, wo, row(norm_ffn2[1]), *ffn2[1])
    conv_s = us.reshape(n_dec, dec_seq, D_MODEL)[:, dec_seq - (CONV_W - 1):][None]

    return (y_p.reshape(batch, seq, D_MODEL),
            y_s.reshape(n_dec, dec_seq, D_MODEL),
            conv_p, conv_s,
            lat_p.reshape(batch, seq, KV_RANK),
            kpe_p.reshape(batch, seq, ROPE_DIM),
            lat_s.reshape(n_dec, dec_seq, KV_RANK),
            kpe_s.reshape(n_dec, dec_seq, ROPE_DIM))
```

```python
import functools

import jax
import jax.numpy as jnp
from jax import lax
from jax.experimental import pallas as pl
from jax.experimental.pallas import tpu as pltpu

F32 = jnp.float32
BF16 = jnp.bfloat16

D_MODEL = 1024
D_FF = 2816
N_HEADS = 8
NOPE_DIM = 128
ROPE_DIM = 64
V_DIM = 128
Q_RANK = 512
KV_RANK = 256
CONV_W = 3
PAGE_SIZE = 128
ROPE_THETA = 10000.0
EPS = 1e-6
SCALE = (NOPE_DIM + ROPE_DIM) ** -0.5
LOG2E = 1.4426950408889634
Q_SCALE = SCALE * LOG2E
V_ROWS = V_DIM + 16
NEG = -1e30

LANES = 128
SUBLANES = 8
TM = 512
FF_CHUNKS = 1
FC = D_FF // FF_CHUNKS
TQ = 512
HEADS_PER_STEP = 2
KEY_CHUNK = 1024
VMEM_LIMIT = 56 * 1024 * 1024

_NT = (((1,), (1,)), ((), ()))


def _const_spec(shape):
    zeros = (0,) * len(shape)
    return pl.BlockSpec(shape, lambda *_: zeros, pipeline_mode=pl.Buffered(1))


def _rms(x, g):
    ms = jnp.mean(x * x, axis=-1, keepdims=True)
    return x * lax.rsqrt(ms + EPS) * g


def _dot(a, b):
    return jnp.dot(a, b, preferred_element_type=F32)


def _ffn(x, g, wgu_ref, wd_ref):
    xn = _rms(x, g).astype(BF16)
    acc = None
    for c in range(FF_CHUNKS):
        gate = _dot(xn, wgu_ref[:, c * FC:(c + 1) * FC])
        up = _dot(xn, wgu_ref[:, D_FF + c * FC:D_FF + (c + 1) * FC])
        act = (gate * jax.nn.sigmoid(gate) * up).astype(BF16)
        part = _dot(act, wd_ref[c * FC:(c + 1) * FC, :])
        acc = part if acc is None else acc + part
    return x + 0.5 * acc


def _rope_norm(a, b, ga, gb, cos_t, sin_t):
    ms = jnp.sum(a * a, axis=-1, keepdims=True) * (1.0 / ROPE_DIM)
    return (a * ga * cos_t + b * gb * sin_t) * lax.rsqrt(ms + EPS)


def _ffn_kernel(x_ref, g_ref, wgu_ref, wd_ref, o_ref):
    o_ref[...] = _ffn(x_ref[...], g_ref[...], wgu_ref, wd_ref)


def _mixer_kernel(tiles_per_seq, is_sample, *refs):
    if is_sample:
        (x_ref, g_ref, win_ref, cw_ref, wout_ref, p1_ref, p2_ref,
         o_ref, u_ref, ubuf) = refs
    else:
        x_ref, g_ref, win_ref, cw_ref, wout_ref, o_ref, u_ref, ubuf = refs
    i = pl.program_id(0)
    x = x_ref[...]
    xn = _rms(x, g_ref[...]).astype(BF16)
    bch = _dot(xn, win_ref[...])
    b = bch[:, :D_MODEL]
    u = bch[:, D_MODEL:2 * D_MODEL] * bch[:, 2 * D_MODEL:]

    @pl.when(i % tiles_per_seq == 0)
    def _():
        ubuf[0:SUBLANES, :] = jnp.zeros((SUBLANES, D_MODEL), F32)

    ubuf[SUBLANES:SUBLANES + TM, :] = u
    s1 = ubuf[SUBLANES - 1:SUBLANES - 1 + TM, :]
    s2 = ubuf[SUBLANES - 2:SUBLANES - 2 + TM, :]
    if is_sample:
        t = lax.broadcasted_iota(jnp.int32, (TM, 1), 0) % 4
        s1 = jnp.where(t >= 1, s1, p1_ref[...])
        s2 = jnp.where(t >= 2, s2, p2_ref[...])
    cw = cw_ref[...]
    y = cw[0:1, :] * s2 + cw[1:2, :] * s1 + cw[2:3, :] * u
    o_ref[...] = x + _dot((b * y).astype(BF16), wout_ref[...])
    if is_sample:
        u_ref[...] = u
    else:
        ubuf[0:SUBLANES, :] = u[TM - SUBLANES:, :]
        u_ref[0] = u[TM - SUBLANES:, :]


def _ffn_kv_kernel(x_ref, g_ref, wgu_ref, wd_ref, gkv_ref, wdkv_ref, kvn_ref,
                   ga_ref, gb_ref, cos_ref, sin_ref, wuk_ref, wuvt_ref, gkn_ref,
                   o_ref, lat_ref, kpe_ref, kpeb_ref, kn_ref, vt_ref):
    x = _ffn(x_ref[...], g_ref[...], wgu_ref, wd_ref)
    o_ref[...] = x
    hk = _rms(x, gkv_ref[...]).astype(BF16)
    ckv = _dot(hk, wdkv_ref[...])
    lat = _rms(ckv[:, :KV_RANK], kvn_ref[...])
    lat_ref[...] = lat
    kpe = _rope_norm(ckv[:, KV_RANK:KV_RANK + LANES], ckv[:, KV_RANK + LANES:],
                     ga_ref[...], gb_ref[...], cos_ref[...], sin_ref[...])
    kpe_ref[...] = kpe[:, :ROPE_DIM]
    kpeb_ref[...] = kpe.astype(BF16)
    latb = lat.astype(BF16)
    ek = _dot(latb, wuk_ref[...])
    gkn = gkn_ref[...]
    for h in range(N_HEADS):
        e = ek[:, h * NOPE_DIM:(h + 1) * NOPE_DIM]
        kn_ref[:, h * NOPE_DIM:(h + 1) * NOPE_DIM] = _rms(e, gkn).astype(BF16)
    vt = lax.dot_general(wuvt_ref[...], latb, _NT, preferred_element_type=F32).astype(BF16)
    ones = jnp.ones((V_ROWS - V_DIM, vt.shape[1]), BF16)
    for h in range(N_HEADS):
        vt_ref[h * V_ROWS:h * V_ROWS + V_DIM, :] = vt[h * V_DIM:(h + 1) * V_DIM, :]
        vt_ref[h * V_ROWS + V_DIM:(h + 1) * V_ROWS, :] = ones


def _ffn_q_kernel(x_ref, g_ref, wgu_ref, wd_ref, gmix_ref, wdq_ref, qn_ref,
                  wuq_ref, gqn_ref, ga_ref, gb_ref, cos_ref, sin_ref,
                  o_ref, q_ref):
    x = _ffn(x_ref[...], g_ref[...], wgu_ref, wd_ref)
    o_ref[...] = x
    xn = _rms(x, gmix_ref[...]).astype(BF16)
    cq = _rms(_dot(xn, wdq_ref[...]), qn_ref[...]).astype(BF16)
    q = _dot(cq, wuq_ref[...])
    gqn = gqn_ref[...]
    ga = ga_ref[...]
    gb = gb_ref[...]
    cos_t = cos_ref[...]
    sin_t = sin_ref[...]
    hw = N_HEADS * LANES
    for h in range(N_HEADS):
        qn = _rms(q[:, h * LANES:(h + 1) * LANES], gqn) * Q_SCALE
        qp = _rope_norm(q[:, hw + h * LANES:hw + (h + 1) * LANES],
                        q[:, 2 * hw + h * LANES:2 * hw + (h + 1) * LANES],
                        ga, gb, cos_t, sin_t) * Q_SCALE
        q_ref[:, 2 * h * LANES:(2 * h + 1) * LANES] = qn.astype(BF16)
        q_ref[:, (2 * h + 1) * LANES:(2 * h + 2) * LANES] = qp.astype(BF16)


def _out_ffn_kernel(x_ref, a_ref, wo_ref, g_ref, wgu_ref, wd_ref, o_ref):
    x = x_ref[...] + _dot(a_ref[...], wo_ref[...])
    o_ref[...] = _ffn(x, g_ref[...], wgu_ref, wd_ref)


def _prompt_attn_kernel(q_ref, kn_ref, kpe_ref, vt_ref, o_ref, kcat,
                        sa_ref, sb_ref, smaxa_ref, smaxb_ref, m_scr, acc_scr):
    qi = pl.program_id(2)

    @pl.when(qi == 0)
    def _():
        for j in range(HEADS_PER_STEP):
            kcat[j, :, :LANES] = kn_ref[:, j * NOPE_DIM:(j + 1) * NOPE_DIM]
            kcat[j, :, LANES:] = kpe_ref[...]

    qs = [q_ref[:, 2 * j * LANES:2 * (j + 1) * LANES] for j in range(HEADS_PER_STEP)]

    def scores(slot, ki, masked):
        s_ref, smax_ref = slot
        off = pl.multiple_of(ki * TQ, TQ)
        for j in range(HEADS_PER_STEP):
            s = lax.dot_general(kcat[j, pl.ds(off, TQ), :], qs[j], _NT,
                                preferred_element_type=F32)
            if masked:
                key = lax.broadcasted_iota(jnp.int32, (TQ, TQ), 0)
                qry = lax.broadcasted_iota(jnp.int32, (TQ, TQ), 1)
                s = jnp.where(qry >= key, s, NEG)
            s_ref[j] = s
            smax_ref[j] = jnp.max(s, axis=0, keepdims=True)

    def accumulate(slot, ki):
        s_ref, smax_ref = slot
        off = pl.multiple_of(ki * TQ, TQ)
        for j in range(HEADS_PER_STEP):
            m = m_scr[j]
            vt = vt_ref[j * V_ROWS:(j + 1) * V_ROWS, pl.ds(off, TQ)]
            m_new = jnp.maximum(m, smax_ref[j])
            alpha = jnp.exp2(m - m_new)
            p = jnp.exp2((s_ref[j] - m_new).astype(BF16))
            acc_scr[j] = alpha * acc_scr[j] + _dot(vt, p)
            m_scr[j] = m_new

    for j in range(HEADS_PER_STEP):
        m_scr[j] = jnp.full((1, TQ), NEG, F32)
        acc_scr[j] = jnp.zeros((V_ROWS, TQ), F32)

    slot_a = (sa_ref, smaxa_ref)
    slot_b = (sb_ref, smaxb_ref)
    n_pairs = qi // 2
    scores(slot_a, qi, True)

    def pair(t, _):
        scores(slot_b, 2 * t, False)
        accumulate(slot_a, jnp.where(t == 0, qi, 2 * t - 1))
        scores(slot_a, 2 * t + 1, False)
        accumulate(slot_b, 2 * t)
        return 0

    lax.fori_loop(0, n_pairs, pair, 0)
    in_a = jnp.where(n_pairs == 0, qi, 2 * n_pairs - 1)

    @pl.when(qi % 2 == 1)
    def _():
        scores(slot_b, qi - 1, False)
        accumulate(slot_a, in_a)
        accumulate(slot_b, qi - 1)

    @pl.when(qi % 2 == 0)
    def _():
        accumulate(slot_a, in_a)

    for j in range(HEADS_PER_STEP):
        acc = acc_scr[j]
        o = acc[:V_DIM] / acc[V_DIM:V_DIM + 1]
        o_ref[:, j * V_DIM:(j + 1) * V_DIM] = o.T.astype(BF16)


def _sample_attn_kernel(n_pages, pt_ref, qa_ref, qn_ref, knn_ref, latn_ref, kpen_ref,
                        wukt_ref, lat_hbm, kpet_hbm, ctx_ref,
                        latbuf, kpebuf, latbf, sbuf, wq, sems):
    n_seq = pl.num_programs(0)
    b = pl.program_id(0)
    slot = b % 2
    n_past = n_pages * PAGE_SIZE
    n_chunks = n_past // KEY_CHUNK
    pages_per_chunk = n_pages // n_chunks
    n_w = N_HEADS * NOPE_DIM

    def page_copies(seq, slot_, j):
        pg = pt_ref[seq * n_pages + j]
        row = pl.multiple_of(j * PAGE_SIZE, PAGE_SIZE)
        return (
            pltpu.make_async_copy(lat_hbm.at[pg], latbuf.at[slot_, pl.ds(row, PAGE_SIZE)],
                                  sems.at[0, slot_]),
            pltpu.make_async_copy(kpet_hbm.at[pg], kpebuf.at[slot_, :, pl.ds(row, PAGE_SIZE)],
                                  sems.at[1, slot_]),
        )

    def start_pages(seq, slot_, first, count):
        for j in range(count):
            for cp in page_copies(seq, slot_, first + j):
                cp.start()

    def wait_fetch(slot_):
        for j in range(n_pages):
            for cp in page_copies(0, slot_, j):
                cp.wait()

    @pl.when(b == 0)
    def _():
        def first_fetch(c, _):
            start_pages(0, 0, c * pages_per_chunk, pages_per_chunk)
            return 0
        lax.fori_loop(0, n_chunks, first_fetch, 0)
        wq[0:n_w, :] = wukt_ref[...]

    wait_fetch(slot)

    qa = qa_ref[0]
    n_q = qa.shape[0]
    wq[n_w:n_w + n_q, :] = qa[:, :KV_RANK]
    q_pe = qa[:, KV_RANK:KV_RANK + ROPE_DIM]
    wq_all = wq[...]

    def scores(latc, pe):
        n = latc.shape[0]
        e = lax.dot_general(wq_all, latc, _NT, preferred_element_type=F32)
        e3 = e[:n_w].reshape(N_HEADS, NOPE_DIM, n)
        r = lax.rsqrt(jnp.sum(e3 * e3, axis=1) * (1.0 / NOPE_DIM) + EPS)
        raw3 = e[n_w:].reshape(n_q // N_HEADS, N_HEADS, n) * r[None]
        return raw3.reshape(n_q, n) + pe

    def chunk(c, _):
        start_pages((b + 1) % n_seq, 1 - slot, c * pages_per_chunk, pages_per_chunk)

        off = pl.multiple_of(c * KEY_CHUNK, KEY_CHUNK)
        latc = latbuf[slot, pl.ds(off, KEY_CHUNK), :].astype(BF16)
        kpec = kpebuf[slot, :, pl.ds(off, KEY_CHUNK)].astype(BF16)
        latbf[pl.ds(off, KEY_CHUNK), :] = latc
        sbuf[:, pl.ds(off, KEY_CHUNK)] = scores(latc, _dot(q_pe, kpec))
        return 0

    lax.fori_loop(0, n_chunks, chunk, 0, unroll=True)

    pad = jnp.zeros((LANES - SUBLANES, KV_RANK), F32)
    latbf[n_past:n_past + LANES, :] = jnp.concatenate([latn_ref[0], pad], axis=0).astype(BF16)
    kpec = jnp.concatenate([kpen_ref[0], pad[:, :ROPE_DIM]], axis=0).astype(BF16)
    s_new = lax.dot_general(q_pe, kpec, _NT, preferred_element_type=F32)
    cross = lax.dot_general(qn_ref[0], knn_ref[0], _NT, preferred_element_type=F32)
    n_new = knn_ref.shape[1] // N_HEADS
    row = lax.broadcasted_iota(jnp.int32, cross.shape, 0)
    col = lax.broadcasted_iota(jnp.int32, cross.shape, 1)
    same_head = row % N_HEADS == col % N_HEADS
    t_q = lax.broadcasted_iota(jnp.int32, (n_q, LANES), 0) // N_HEADS
    j_k = lax.broadcasted_iota(jnp.int32, (n_q, LANES), 1)
    for j in range(n_new):
        pick = jnp.where(same_head & (col // N_HEADS == j), cross, 0.0)
        s_new = s_new + jnp.where(j_k == j, jnp.sum(pick, axis=-1, keepdims=True), 0.0)
    sbuf[:, n_past:n_past + LANES] = jnp.where(j_k <= t_q, s_new, NEG)

    s = sbuf[...]
    m = jnp.max(s, axis=-1, keepdims=True)
    p = jnp.exp2(s - m)
    l = jnp.sum(p, axis=-1, keepdims=True)
    p = p.astype(BF16)
    half = n_past // 2
    ctx = _dot(p[:, :half], latbf[0:half, :]) + _dot(p[:, half:], latbf[half:, :])
    ctx_ref[0] = ctx / l

    @pl.when(b == n_seq - 1)
    def _():
        wait_fetch(1 - slot)


def _absorb_kernel(q_ref, gkn_ref, wukt_ref, o_ref):
    gkn = gkn_ref[...]
    w = KV_RANK + LANES
    for h in range(N_HEADS):
        qn = (q_ref[:, 2 * h * LANES:(2 * h + 1) * LANES].astype(F32) * gkn).astype(BF16)
        o_ref[:, h * w:h * w + KV_RANK] = _dot(
            qn, wukt_ref[h * NOPE_DIM:(h + 1) * NOPE_DIM, :]).astype(BF16)
        o_ref[:, h * w + KV_RANK:(h + 1) * w] = q_ref[:, (2 * h + 1) * LANES:(2 * h + 2) * LANES]


def _ctx_out_kernel(c_ref, wuv_ref, o_ref):
    for h in range(N_HEADS):
        c = c_ref[:, h * KV_RANK:(h + 1) * KV_RANK].astype(BF16)
        o_ref[:, h * V_DIM:(h + 1) * V_DIM] = _dot(
            c, wuv_ref[:, h * V_DIM:(h + 1) * V_DIM]).astype(BF16)


def _params(n_axes=1):
    return pltpu.CompilerParams(dimension_semantics=("arbitrary",) * n_axes,
                                vmem_limit_bytes=VMEM_LIMIT)


def _row_spec(width):
    return pl.BlockSpec((TM, width), lambda i: (i, 0))


def _table_spec(n_rows):
    tiles = n_rows // TM
    return pl.BlockSpec((TM, LANES), lambda i: (i % tiles, 0))


def _ffn_specs(layer):
    pick = lambda *_: (layer, 0, 0)
    return [_const_spec((1, D_MODEL)),
            pl.BlockSpec((None, D_MODEL, 2 * D_FF), pick, pipeline_mode=pl.Buffered(1)),
            pl.BlockSpec((None, D_FF, D_MODEL), pick, pipeline_mode=pl.Buffered(1))]


def _ffn_call(x, g, wgu, wd, layer):
    t = x.shape[0]
    return pl.pallas_call(
        _ffn_kernel, grid=(t // TM,),
        in_specs=[_row_spec(D_MODEL)] + _ffn_specs(layer),
        out_specs=_row_spec(D_MODEL),
        out_shape=jax.ShapeDtypeStruct((t, D_MODEL), F32),
        compiler_params=_params(), name="ffn")(x, g, wgu, wd)


def _mixer_call(x, g, win, cw, wout, tiles_per_seq, prev=None):
    t = x.shape[0]
    n_tiles = t // TM
    is_sample = prev is not None
    in_specs = [_row_spec(D_MODEL), _const_spec((1, D_MODEL)),
                _const_spec((D_MODEL, 3 * D_MODEL)), _const_spec((CONV_W, D_MODEL)),
                _const_spec((D_MODEL, D_MODEL))]
    args = [x, g, win, cw, wout]
    if is_sample:
        in_specs += [_row_spec(D_MODEL), _row_spec(D_MODEL)]
        args += list(prev)
        u_spec = _row_spec(D_MODEL)
        u_shape = jax.ShapeDtypeStruct((t, D_MODEL), F32)
    else:
        u_spec = pl.BlockSpec((1, SUBLANES, D_MODEL), lambda i: (i, 0, 0))
        u_shape = jax.ShapeDtypeStruct((n_tiles, SUBLANES, D_MODEL), F32)
    return pl.pallas_call(
        functools.partial(_mixer_kernel, tiles_per_seq, is_sample), grid=(n_tiles,),
        in_specs=in_specs,
        out_specs=[_row_spec(D_MODEL), u_spec],
        out_shape=[jax.ShapeDtypeStruct((t, D_MODEL), F32), u_shape],
        scratch_shapes=[pltpu.VMEM((TM + SUBLANES, D_MODEL), F32)],
        compiler_params=_params(), name="mixer")(*args)


def _ffn_kv_call(x, g, wgu, wd, layer, gkv, wdkv, kvn, ga, gb, cos_t, sin_t, wuk, wuvt, gkn):
    t = x.shape[0]
    in_specs = ([_row_spec(D_MODEL)] + _ffn_specs(layer) + [
        _const_spec((1, D_MODEL)), _const_spec((D_MODEL, KV_RANK + 2 * LANES)),
        _const_spec((1, KV_RANK)), _const_spec((1, LANES)), _const_spec((1, LANES)),
        _table_spec(cos_t.shape[0]), _table_spec(sin_t.shape[0]),
        _const_spec((KV_RANK, D_MODEL)), _const_spec((D_MODEL, KV_RANK)),
        _const_spec((1, NOPE_DIM))])
    out_specs = [_row_spec(D_MODEL), _row_spec(KV_RANK), _row_spec(ROPE_DIM),
                 _row_spec(LANES), _row_spec(D_MODEL),
                 pl.BlockSpec((N_HEADS * V_ROWS, TM), lambda i: (0, i))]
    out_shape = [jax.ShapeDtypeStruct((t, D_MODEL), F32),
                 jax.ShapeDtypeStruct((t, KV_RANK), F32),
                 jax.ShapeDtypeStruct((t, ROPE_DIM), F32),
                 jax.ShapeDtypeStruct((t, LANES), BF16),
                 jax.ShapeDtypeStruct((t, D_MODEL), BF16),
                 jax.ShapeDtypeStruct((N_HEADS * V_ROWS, t), BF16)]
    return pl.pallas_call(
        _ffn_kv_kernel, grid=(t // TM,), in_specs=in_specs, out_specs=out_specs,
        out_shape=out_shape, compiler_params=_params(), name="ffn_kv")(
            x, g, wgu, wd, gkv, wdkv, kvn, ga, gb, cos_t, sin_t, wuk, wuvt, gkn)


def _ffn_q_call(x, g, wgu, wd, layer, gmix, wdq, qnorm, wuq, gqn, ga, gb, cos_t, sin_t):
    t = x.shape[0]
    in_specs = ([_row_spec(D_MODEL)] + _ffn_specs(layer) + [
        _const_spec((1, D_MODEL)), _const_spec((D_MODEL, Q_RANK)),
        _const_spec((1, Q_RANK)), _const_spec((Q_RANK, 3 * N_HEADS * LANES)),
        _const_spec((1, NOPE_DIM)), _const_spec((1, LANES)), _const_spec((1, LANES)),
        _table_spec(cos_t.shape[0]), _table_spec(sin_t.shape[0])])
    return pl.pallas_call(
        _ffn_q_kernel, grid=(t // TM,), in_specs=in_specs,
        out_specs=[_row_spec(D_MODEL), _row_spec(2 * N_HEADS * LANES)],
        out_shape=[jax.ShapeDtypeStruct((t, D_MODEL), F32),
                   jax.ShapeDtypeStruct((t, 2 * N_HEADS * LANES), BF16)],
        compiler_params=_params(), name="ffn_q")(
            x, g, wgu, wd, gmix, wdq, qnorm, wuq, gqn, ga, gb, cos_t, sin_t)


def _out_ffn_call(x, a, wo, g, wgu, wd, layer):
    t = x.shape[0]
    return pl.pallas_call(
        _out_ffn_kernel, grid=(t // TM,),
        in_specs=[_row_spec(D_MODEL), _row_spec(D_MODEL),
                  _const_spec((D_MODEL, D_MODEL))] + _ffn_specs(layer),
        out_specs=_row_spec(D_MODEL),
        out_shape=jax.ShapeDtypeStruct((t, D_MODEL), F32),
        compiler_params=_params(), name="out_ffn")(x, a, wo, g, wgu, wd)


def _prompt_attn_call(q, kn, kpe, vt, batch, seq):
    nq = seq // TQ
    hps = HEADS_PER_STEP
    return pl.pallas_call(
        _prompt_attn_kernel, grid=(batch, N_HEADS // hps, nq),
        in_specs=[pl.BlockSpec((TQ, hps * 2 * LANES), lambda b, h, i: (b * nq + i, h)),
                  pl.BlockSpec((seq, hps * NOPE_DIM), lambda b, h, i: (b, h)),
                  pl.BlockSpec((seq, LANES), lambda b, h, i: (b, 0)),
                  pl.BlockSpec((hps * V_ROWS, seq), lambda b, h, i: (h, b))],
        out_specs=pl.BlockSpec((TQ, hps * V_DIM), lambda b, h, i: (b * nq + i, h)),
        out_shape=jax.ShapeDtypeStruct((batch * seq, N_HEADS * V_DIM), BF16),
        scratch_shapes=[pltpu.VMEM((hps, seq, 2 * LANES), BF16),
                        pltpu.VMEM((hps, TQ, TQ), F32), pltpu.VMEM((hps, TQ, TQ), F32),
                        pltpu.VMEM((hps, 1, TQ), F32), pltpu.VMEM((hps, 1, TQ), F32),
                        pltpu.VMEM((hps, 1, TQ), F32), pltpu.VMEM((hps, V_ROWS, TQ), F32)],
        compiler_params=_params(3), name="prompt_attn")(q, kn, kpe, vt)


def _sample_attn_call(page_table, qa, qn, kn_new, lat_new, kpe_new, wukt,
                      cache_latent, cache_kpe):
    n_seq, n_pages = page_table.shape
    n_q = qa.shape[1]
    n_keys = n_pages * PAGE_SIZE
    grid_spec = pltpu.PrefetchScalarGridSpec(
        num_scalar_prefetch=1, grid=(n_seq,),
        in_specs=[pl.BlockSpec((1, n_q, KV_RANK + LANES), lambda b, pt: (b, 0, 0)),
                  pl.BlockSpec((1, n_q, NOPE_DIM), lambda b, pt: (b, 0, 0)),
                  pl.BlockSpec((1, kn_new.shape[1], NOPE_DIM), lambda b, pt: (b, 0, 0)),
                  pl.BlockSpec((1, SUBLANES, KV_RANK), lambda b, pt: (b, 0, 0)),
                  pl.BlockSpec((1, SUBLANES, ROPE_DIM), lambda b, pt: (b, 0, 0)),
                  pl.BlockSpec((N_HEADS * NOPE_DIM, KV_RANK), lambda b, pt: (0, 0),
                               pipeline_mode=pl.Buffered(1)),
                  pl.BlockSpec(memory_space=pl.ANY),
                  pl.BlockSpec(memory_space=pl.ANY)],
        out_specs=pl.BlockSpec((1, n_q, KV_RANK), lambda b, pt: (b, 0, 0)),
        scratch_shapes=[pltpu.VMEM((2, n_keys, KV_RANK), F32),
                        pltpu.VMEM((2, ROPE_DIM, n_keys), F32),
                        pltpu.VMEM((n_keys + LANES, KV_RANK), BF16),
                        pltpu.VMEM((n_q, n_keys + LANES), F32),
                        pltpu.VMEM((N_HEADS * NOPE_DIM + n_q, KV_RANK), BF16),
                        pltpu.SemaphoreType.DMA((2, 2))])
    return pl.pallas_call(
        functools.partial(_sample_attn_kernel, n_pages), grid_spec=grid_spec,
        out_shape=jax.ShapeDtypeStruct((n_seq, n_q, KV_RANK), F32),
        compiler_params=_params(), name="sample_attn")(
            page_table.reshape(-1), qa, qn, kn_new, lat_new, kpe_new, wukt, cache_latent,
            jnp.swapaxes(cache_kpe, 1, 2))


def _absorb_call(q, gkn, wukt):
    t = q.shape[0]
    return pl.pallas_call(
        _absorb_kernel, grid=(t // TM,),
        in_specs=[_row_spec(2 * N_HEADS * LANES), _const_spec((1, NOPE_DIM)),
                  _const_spec((N_HEADS * NOPE_DIM, KV_RANK))],
        out_specs=_row_spec(N_HEADS * (KV_RANK + LANES)),
        out_shape=jax.ShapeDtypeStruct((t, N_HEADS * (KV_RANK + LANES)), BF16),
        compiler_params=_params(), name="absorb")(q, gkn, wukt)


def _ctx_out_call(ctx, wuv):
    t = ctx.shape[0]
    return pl.pallas_call(
        _ctx_out_kernel, grid=(t // TM,),
        in_specs=[_row_spec(N_HEADS * KV_RANK),
                  _const_spec((KV_RANK, N_HEADS * V_DIM))],
        out_specs=_row_spec(N_HEADS * V_DIM),
        out_shape=jax.ShapeDtypeStruct((t, N_HEADS * V_DIM), BF16),
        compiler_params=_params(), name="ctx_out")(ctx, wuv)


def _rope_tables(pos):
    half = ROPE_DIM // 2
    freqs = ROPE_THETA ** (-jnp.arange(half, dtype=F32) / half)
    ang = pos.astype(F32)[:, None] * freqs
    cos = jnp.cos(ang)
    sin = jnp.sin(ang)
    zero = jnp.zeros((pos.shape[0], LANES - ROPE_DIM), F32)
    return (jnp.concatenate([cos, cos, zero], axis=1),
            jnp.concatenate([-sin, sin, zero], axis=1))


def _swap_halves(a):
    half = a.shape[-1] // 2
    return jnp.concatenate([a[..., half:], a[..., :half]], axis=-1)


def _pad_lanes(a):
    pad = [(0, 0)] * (a.ndim - 1) + [(0, LANES - a.shape[-1])]
    return jnp.pad(a, pad)


def kernel(x_prompt, x_sample, cache_latent, cache_kpe, state_conv, page_table, norm_ffn1, w_ffn1_gu, w_ffn1_down, norm_mix, norm_ffn2, w_ffn2_gu, w_ffn2_down, w_conv_in, conv_w, w_conv_out, w_dq, q_norm, w_uq, q_nope_norm, q_pe_norm, w_o, norm_kv_in, w_dkv, kv_norm, k_pe_norm, w_uk, w_uv, k_nope_norm):
    batch, seq, _ = x_prompt.shape
    n_dec, dec_seq, _ = x_sample.shape
    n_pages = page_table.shape[1]
    assert seq % TM == 0 and (n_dec * dec_seq) % TM == 0 and dec_seq == 4

    row = lambda a: a.reshape(1, -1)
    ffn1 = (w_ffn1_gu.astype(BF16), w_ffn1_down.astype(BF16))
    ffn2 = (w_ffn2_gu.astype(BF16), w_ffn2_down.astype(BF16))
    win = w_conv_in[0].astype(BF16)
    wout = w_conv_out[0].astype(BF16)
    w_pe = w_dkv[:, KV_RANK:]
    wdkv = jnp.concatenate([w_dkv[:, :KV_RANK], _pad_lanes(w_pe),
                            _pad_lanes(_swap_halves(w_pe))], axis=1).astype(BF16)
    wuk = w_uk.astype(BF16)
    wuvt = w_uv.T.astype(BF16)
    wukt = w_uk.T.astype(BF16)
    wuv = w_uv.astype(BF16)
    wdq = w_dq[0].astype(BF16)
    wuq3 = w_uq[0].reshape(Q_RANK, N_HEADS, NOPE_DIM + ROPE_DIM)
    wq_pe = wuq3[:, :, NOPE_DIM:]
    wuq = jnp.concatenate([
        wuq3[:, :, :NOPE_DIM].reshape(Q_RANK, -1),
        _pad_lanes(wq_pe).reshape(Q_RANK, -1),
        _pad_lanes(_swap_halves(wq_pe)).reshape(Q_RANK, -1)], axis=1).astype(BF16)
    wo = w_o[0].astype(BF16)
    kga = row(_pad_lanes(k_pe_norm))
    kgb = row(_pad_lanes(_swap_halves(k_pe_norm)))
    qga = row(_pad_lanes(q_pe_norm[0]))
    qgb = row(_pad_lanes(_swap_halves(q_pe_norm[0])))

    cos_p, sin_p = _rope_tables(jnp.arange(seq))
    pos_s = n_pages * PAGE_SIZE + jnp.arange(n_dec * dec_seq) % dec_seq
    cos_s, sin_s = _rope_tables(pos_s)

    prev = state_conv[0]
    zrow = jnp.zeros((n_dec, 1, D_MODEL), F32)
    p1 = jnp.concatenate([prev[:, 1:2], zrow, zrow, zrow], axis=1).reshape(-1, D_MODEL)
    p2 = jnp.concatenate([prev[:, 0:1], prev[:, 1:2], zrow, zrow], axis=1).reshape(-1, D_MODEL)

    def dense(x, tiles_per_seq, cos_t, sin_t, prev_rows):
        x = _ffn_call(x, row(norm_ffn1[0]), *ffn1, 0)
        x, u = _mixer_call(x, row(norm_mix[0]), win, conv_w[0], wout, tiles_per_seq, prev_rows)
        x, lat, kpe, kpeb, kn, v = _ffn_kv_call(
            x, row(norm_ffn2[0]), *ffn2, 0, row(norm_kv_in), wdkv, row(kv_norm),
            kga, kgb, cos_t, sin_t, wuk, wuvt, row(k_nope_norm))
        x, q = _ffn_q_call(
            x, row(norm_ffn1[1]), *ffn1, 1, row(norm_mix[1]), wdq, row(q_norm[0]), wuq,
            row(q_nope_norm[0]), qga, qgb, cos_t, sin_t)
        return x, u, lat, kpe, kpeb, kn, v, q

    xp, up, lat_p, kpe_p, kpeb_p, kn_p, v_p, q_p = dense(
        x_prompt.reshape(batch * seq, D_MODEL), seq // TM, cos_p, sin_p, None)
    o_p = _prompt_attn_call(q_p, kn_p, kpeb_p, v_p, batch, seq)
    y_p = _out_ffn_call(xp, o_p, wo, row(norm_ffn2[1]), *ffn2, 1)
    conv_p = up.reshape(batch, seq // TM, SUBLANES, D_MODEL)[:, -1, SUBLANES - 2:][None]

    xs, us, lat_s, kpe_s, _, kn_s, _, q_s = dense(
        x_sample.reshape(n_dec * dec_seq, D_MODEL), 1, cos_s, sin_s, (p1, p2))
    qa = _absorb_call(q_s, row(k_nope_norm), wukt)
    qa = qa.reshape(n_dec, dec_seq * N_HEADS, KV_RANK + LANES)
    pad_rows = ((0, 0), (0, SUBLANES - dec_seq), (0, 0))
    lat_new = jnp.pad(lat_s.reshape(n_dec, dec_seq, KV_RANK), pad_rows)
    kpe_new = jnp.pad(kpe_s.reshape(n_dec, dec_seq, ROPE_DIM), pad_rows)
    qn_s = q_s.reshape(n_dec, dec_seq * N_HEADS, 2 * LANES)[:, :, :NOPE_DIM]
    kn_new = kn_s.reshape(n_dec, dec_seq * N_HEADS, NOPE_DIM)
    ctx = _sample_attn_call(page_table, qa, qn_s, kn_new, lat_new, kpe_new, wukt,
                            cache_latent, cache_kpe)
    o_s = _ctx_out_call(ctx.reshape(n_dec * dec_seq, N_HEADS * KV_RANK), wuv)
    y_s = _out_ffn_call(xs, o_s, wo, row(norm_ffn2[1]), *ffn2, 1)
    conv_s = us.reshape(n_dec, dec_seq, D_MODEL)[:, dec_seq - (CONV_W - 1):][None]

    return (y_p.reshape(batch, seq, D_MODEL),
            y_s.reshape(n_dec, dec_seq, D_MODEL),
            conv_p, conv_s,
            lat_p.reshape(batch, seq, KV_RANK),
            kpe_p.reshape(batch, seq, ROPE_DIM),
            lat_s.reshape(n_dec, dec_seq, KV_RANK),
            kpe_s.reshape(n_dec, dec_seq, ROPE_DIM))
```

```python
import functools

import jax
import jax.numpy as jnp
from jax import lax
from jax.experimental import pallas as pl
from jax.experimental.pallas import tpu as pltpu

F32 = jnp.float32
BF16 = jnp.bfloat16

D_MODEL = 1024
D_FF = 2816
N_HEADS = 8
NOPE_DIM = 128
ROPE_DIM = 64
V_DIM = 128
Q_RANK = 512
KV_RANK = 256
CONV_W = 3
PAGE_SIZE = 128
ROPE_THETA = 10000.0
EPS = 1e-6
SCALE = (NOPE_DIM + ROPE_DIM) ** -0.5
LOG2E = 1.4426950408889634
Q_SCALE = SCALE * LOG2E
V_ROWS = V_DIM + 16
NEG = -1e30

LANES = 128
SUBLANES = 8
TM = 512
FF_CHUNKS = 1
FC = D_FF // FF_CHUNKS
TQ = 512
HEADS_PER_STEP = 4
KEY_CHUNK = 1024
VMEM_LIMIT = 56 * 1024 * 1024

_NT = (((1,), (1,)), ((), ()))


def _const_spec(shape):
    zeros = (0,) * len(shape)
    return pl.BlockSpec(shape, lambda *_: zeros, pipeline_mode=pl.Buffered(1))


def _rms(x, g):
    ms = jnp.mean(x * x, axis=-1, keepdims=True)
    return x * lax.rsqrt(ms + EPS) * g


def _dot(a, b):
    return jnp.dot(a, b, preferred_element_type=F32)


def _ffn(x, g, wgu_ref, wd_ref):
    xn = _rms(x, g).astype(BF16)
    acc = None
    for c in range(FF_CHUNKS):
        gate = _dot(xn, wgu_ref[:, c * FC:(c + 1) * FC])
        up = _dot(xn, wgu_ref[:, D_FF + c * FC:D_FF + (c + 1) * FC])
        act = (gate * jax.nn.sigmoid(gate) * up).astype(BF16)
        part = _dot(act, wd_ref[c * FC:(c + 1) * FC, :])
        acc = part if acc is None else acc + part
    return x + 0.5 * acc


def _rope_norm(a, b, ga, gb, cos_t, sin_t):
    ms = jnp.sum(a * a, axis=-1, keepdims=True) * (1.0 / ROPE_DIM)
    return (a * ga * cos_t + b * gb * sin_t) * lax.rsqrt(ms + EPS)


def _ffn_kernel(x_ref, g_ref, wgu_ref, wd_ref, o_ref):
    o_ref[...] = _ffn(x_ref[...], g_ref[...], wgu_ref, wd_ref)


def _mixer_kernel(tiles_per_seq, is_sample, *refs):
    if is_sample:
        (x_ref, g_ref, win_ref, cw_ref, wout_ref, p1_ref, p2_ref,
         o_ref, u_ref, ubuf) = refs
    else:
        x_ref, g_ref, win_ref, cw_ref, wout_ref, o_ref, u_ref, ubuf = refs
    i = pl.program_id(0)
    x = x_ref[...]
    xn = _rms(x, g_ref[...]).astype(BF16)
    bch = _dot(xn, win_ref[...])
    b = bch[:, :D_MODEL]
    u = bch[:, D_MODEL:2 * D_MODEL] * bch[:, 2 * D_MODEL:]

    @pl.when(i % tiles_per_seq == 0)
    def _():
        ubuf[0:SUBLANES, :] = jnp.zeros((SUBLANES, D_MODEL), F32)

    ubuf[SUBLANES:SUBLANES + TM, :] = u
    s1 = ubuf[SUBLANES - 1:SUBLANES - 1 + TM, :]
    s2 = ubuf[SUBLANES - 2:SUBLANES - 2 + TM, :]
    if is_sample:
        t = lax.broadcasted_iota(jnp.int32, (TM, 1), 0) % 4
        s1 = jnp.where(t >= 1, s1, p1_ref[...])
        s2 = jnp.where(t >= 2, s2, p2_ref[...])
    cw = cw_ref[...]
    y = cw[0:1, :] * s2 + cw[1:2, :] * s1 + cw[2:3, :] * u
    o_ref[...] = x + _dot((b * y).astype(BF16), wout_ref[...])
    if is_sample:
        u_ref[...] = u
    else:
        ubuf[0:SUBLANES, :] = u[TM - SUBLANES:, :]
        u_ref[0] = u[TM - SUBLANES:, :]


def _ffn_kv_kernel(x_ref, g_ref, wgu_ref, wd_ref, gkv_ref, wdkv_ref, kvn_ref,
                   ga_ref, gb_ref, cos_ref, sin_ref, wuk_ref, wuvt_ref, gkn_ref,
                   o_ref, lat_ref, kpe_ref, k_ref, vt_ref):
    x = _ffn(x_ref[...], g_ref[...], wgu_ref, wd_ref)
    o_ref[...] = x
    hk = _rms(x, gkv_ref[...]).astype(BF16)
    ckv = _dot(hk, wdkv_ref[...])
    lat = _rms(ckv[:, :KV_RANK], kvn_ref[...])
    lat_ref[...] = lat
    kpe = _rope_norm(ckv[:, KV_RANK:KV_RANK + LANES], ckv[:, KV_RANK + LANES:],
                     ga_ref[...], gb_ref[...], cos_ref[...], sin_ref[...])
    kpe_ref[...] = kpe[:, :ROPE_DIM]
    kpeb = kpe.astype(BF16)
    latb = lat.astype(BF16)
    ek = _dot(latb, wuk_ref[...])
    gkn = gkn_ref[...]
    for h in range(N_HEADS):
        e = ek[:, h * NOPE_DIM:(h + 1) * NOPE_DIM]
        k_ref[:, 2 * h * LANES:(2 * h + 1) * LANES] = _rms(e, gkn).astype(BF16)
        k_ref[:, (2 * h + 1) * LANES:(2 * h + 2) * LANES] = kpeb
    vt = lax.dot_general(wuvt_ref[...], latb, _NT, preferred_element_type=F32).astype(BF16)
    ones = jnp.ones((V_ROWS - V_DIM, vt.shape[1]), BF16)
    for h in range(N_HEADS):
        vt_ref[h * V_ROWS:h * V_ROWS + V_DIM, :] = vt[h * V_DIM:(h + 1) * V_DIM, :]
        vt_ref[h * V_ROWS + V_DIM:(h + 1) * V_ROWS, :] = ones


def _ffn_q_kernel(x_ref, g_ref, wgu_ref, wd_ref, gmix_ref, wdq_ref, qn_ref,
                  wuq_ref, gqn_ref, ga_ref, gb_ref, cos_ref, sin_ref,
                  o_ref, q_ref):
    x = _ffn(x_ref[...], g_ref[...], wgu_ref, wd_ref)
    o_ref[...] = x
    xn = _rms(x, gmix_ref[...]).astype(BF16)
    cq = _rms(_dot(xn, wdq_ref[...]), qn_ref[...]).astype(BF16)
    q = _dot(cq, wuq_ref[...])
    gqn = gqn_ref[...]
    ga = ga_ref[...]
    gb = gb_ref[...]
    cos_t = cos_ref[...]
    sin_t = sin_ref[...]
    hw = N_HEADS * LANES
    for h in range(N_HEADS):
        qn = _rms(q[:, h * LANES:(h + 1) * LANES], gqn) * Q_SCALE
        qp = _rope_norm(q[:, hw + h * LANES:hw + (h + 1) * LANES],
                        q[:, 2 * hw + h * LANES:2 * hw + (h + 1) * LANES],
                        ga, gb, cos_t, sin_t) * Q_SCALE
        q_ref[:, 2 * h * LANES:(2 * h + 1) * LANES] = qn.astype(BF16)
        q_ref[:, (2 * h + 1) * LANES:(2 * h + 2) * LANES] = qp.astype(BF16)


def _out_ffn_kernel(x_ref, a_ref, wo_ref, g_ref, wgu_ref, wd_ref, o_ref):
    x = x_ref[...] + _dot(a_ref[...], wo_ref[...])
    o_ref[...] = _ffn(x, g_ref[...], wgu_ref, wd_ref)


def _prompt_attn_kernel(q_ref, k_ref, vt_ref, o_ref,
                        sa_ref, sb_ref, smaxa_ref, smaxb_ref, m_scr, acc_scr):
    qi = pl.program_id(2)
    qs =[q_ref[:, 2 * j * LANES:2 * (j + 1) * LANES] for j in range(HEADS_PER_STEP)]

    def scores(slot, ki, masked):
        s_ref, smax_ref = slot
        off = pl.multiple_of(ki * TQ, TQ)
        for j in range(HEADS_PER_STEP):
            s = lax.dot_general(k_ref[pl.ds(off, TQ), 2 * j * LANES:2 * (j + 1) * LANES],
                                qs[j], _NT,
                                preferred_element_type=F32)
            if masked:
                key = lax.broadcasted_iota(jnp.int32, (TQ, TQ), 0)
                qry = lax.broadcasted_iota(jnp.int32, (TQ, TQ), 1)
                s = jnp.where(qry >= key, s, NEG)
            s_ref[j] = s
            smax_ref[j] = jnp.max(s, axis=0, keepdims=True)

    def accumulate(slot, ki):
        s_ref, smax_ref = slot
        off = pl.multiple_of(ki * TQ, TQ)
        for j in range(HEADS_PER_STEP):
            m = m_scr[j]
            vt = vt_ref[j * V_ROWS:(j + 1) * V_ROWS, pl.ds(off, TQ)]
            m_new = jnp.maximum(m, smax_ref[j])
            alpha = jnp.exp2(m - m_new)
            p = jnp.exp2((s_ref[j] - m_new).astype(BF16))
            acc_scr[j] = alpha * acc_scr[j] + _dot(vt, p)
            m_scr[j] = m_new

    for j in range(HEADS_PER_STEP):
        m_scr[j] = jnp.full((1, TQ), NEG, F32)
        acc_scr[j] = jnp.zeros((V_ROWS, TQ), F32)

    slot_a = (sa_ref, smaxa_ref)
    slot_b = (sb_ref, smaxb_ref)
    n_pairs = qi // 2
    scores(slot_a, qi, True)

    def pair(t, _):
        scores(slot_b, 2 * t, False)
        accumulate(slot_a, jnp.where(t == 0, qi, 2 * t - 1))
        scores(slot_a, 2 * t + 1, False)
        accumulate(slot_b, 2 * t)
        return 0

    lax.fori_loop(0, n_pairs, pair, 0)
    in_a = jnp.where(n_pairs == 0, qi, 2 * n_pairs - 1)

    @pl.when(qi % 2 == 1)
    def _():
        scores(slot_b, qi - 1, False)
        accumulate(slot_a, in_a)
        accumulate(slot_b, qi - 1)

    @pl.when(qi % 2 == 0)
    def _():
        accumulate(slot_a, in_a)

    for j in range(HEADS_PER_STEP):
        acc = acc_scr[j]
        o = acc[:V_DIM] / acc[V_DIM:V_DIM + 1]
        o_ref[:, j * V_DIM:(j + 1) * V_DIM] = o.T.astype(BF16)


def _sample_attn_kernel(n_pages, pt_ref, qa_ref, qn_ref, knn_ref, latn_ref, kpen_ref,
                        wukt_ref, lat_hbm, kpet_hbm, ctx_ref,
                        latbuf, kpebuf, wq, sems):
    n_seq = pl.num_programs(0)
    b = pl.program_id(0)
    slot = b % 2
    n_past = n_pages * PAGE_SIZE
    n_chunks = n_past // KEY_CHUNK
    pages_per_chunk = n_pages // n_chunks
    n_w = N_HEADS * NOPE_DIM

    def page_copies(seq, slot_, j):
        pg = pt_ref[seq * n_pages + j]
        row = pl.multiple_of(j * PAGE_SIZE, PAGE_SIZE)
        return (
            pltpu.make_async_copy(lat_hbm.at[pg], latbuf.at[slot_, pl.ds(row, PAGE_SIZE)],
                                  sems.at[0, slot_]),
            pltpu.make_async_copy(kpet_hbm.at[pg], kpebuf.at[slot_, :, pl.ds(row, PAGE_SIZE)],
                                  sems.at[1, slot_]),
        )

    def start_pages(seq, slot_, first, count):
        for j in range(count):
            for cp in page_copies(seq, slot_, first + j):
                cp.start()

    def wait_fetch(slot_):
        for j in range(n_pages):
            for cp in page_copies(0, slot_, j):
                cp.wait()

    @pl.when(b == 0)
    def _():
        def first_fetch(c, _):
            start_pages(0, 0, c * pages_per_chunk, pages_per_chunk)
            return 0
        lax.fori_loop(0, n_chunks, first_fetch, 0)
        wq[0:n_w, :] = wukt_ref[...]

    wait_fetch(slot)

    qa = qa_ref[0]
    n_q = qa.shape[0]
    wq[n_w:n_w + n_q, :] = qa[:, :KV_RANK]
    q_pe = qa[:, KV_RANK:KV_RANK + ROPE_DIM]
    wq_all = wq[...]

    def scores(latc, pe):
        n = latc.shape[0]
        e = lax.dot_general(wq_all, latc, _NT, preferred_element_type=F32)
        e3 = e[:n_w].reshape(N_HEADS, NOPE_DIM, n)
        r = lax.rsqrt(jnp.sum(e3 * e3, axis=1) * (1.0 / NOPE_DIM) + EPS)
        raw3 = e[n_w:].reshape(n_q // N_HEADS, N_HEADS, n) * r[None]
        return raw3.reshape(n_q, n) + pe

    pad = jnp.zeros((LANES - SUBLANES, KV_RANK), F32)
    lat_new = jnp.concatenate([latn_ref[0], pad], axis=0).astype(BF16)
    kpec = jnp.concatenate([kpen_ref[0], pad[:, :ROPE_DIM]], axis=0).astype(BF16)
    s_new = lax.dot_general(q_pe, kpec, _NT, preferred_element_type=F32)
    cross = lax.dot_general(qn_ref[0], knn_ref[0], _NT, preferred_element_type=F32)
    n_new = knn_ref.shape[1] // N_HEADS
    row = lax.broadcasted_iota(jnp.int32, cross.shape, 0)
    col = lax.broadcasted_iota(jnp.int32, cross.shape, 1)
    same_head = row % N_HEADS == col % N_HEADS
    t_q = lax.broadcasted_iota(jnp.int32, (n_q, LANES), 0) // N_HEADS
    j_k = lax.broadcasted_iota(jnp.int32, (n_q, LANES), 1)
    for j in range(n_new):
        pick = jnp.where(same_head & (col // N_HEADS == j), cross, 0.0)
        s_new = s_new + jnp.where(j_k == j, jnp.sum(pick, axis=-1, keepdims=True), 0.0)
    s_new = jnp.where(j_k <= t_q, s_new, NEG)

    m = jnp.max(s_new, axis=-1, keepdims=True)
    p = jnp.exp2(s_new - m)
    l = jnp.sum(p, axis=-1, keepdims=True)
    acc = _dot(p.astype(BF16), lat_new)

    pending = None
    for c in range(n_chunks):
        start_pages((b + 1) % n_seq, 1 - slot, c * pages_per_chunk, pages_per_chunk)
        keys = slice(c * KEY_CHUNK, (c + 1) * KEY_CHUNK)
        latc = latbuf[slot, keys, :].astype(BF16)
        kpec = kpebuf[slot, :, keys].astype(BF16)
        s = scores(latc, _dot(q_pe, kpec))
        if pending is not None:
            alpha, p_prev, lat_prev = pending
            acc = alpha * acc + _dot(p_prev, lat_prev)
        m_new = jnp.maximum(m, jnp.max(s, axis=-1, keepdims=True))
        alpha = jnp.exp2(m - m_new)
        p = jnp.exp2(s - m_new)
        l = alpha * l + jnp.sum(p, axis=-1, keepdims=True)
        m = m_new
        pending = (alpha, p.astype(BF16), latc)
    alpha, p_prev, lat_prev = pending
    acc = alpha * acc + _dot(p_prev, lat_prev)
    ctx_ref[0] = acc / l

    @pl.when(b == n_seq - 1)
    def _():
        wait_fetch(1 - slot)


def _absorb_kernel(q_ref, gkn_ref, wukt_ref, o_ref):
    gkn = gkn_ref[...]
    w = KV_RANK + LANES
    for h in range(N_HEADS):
        qn = (q_ref[:, 2 * h * LANES:(2 * h + 1) * LANES].astype(F32) * gkn).astype(BF16)
        o_ref[:, h * w:h * w + KV_RANK] = _dot(
            qn, wukt_ref[h * NOPE_DIM:(h + 1) * NOPE_DIM, :]).astype(BF16)
        o_ref[:, h * w + KV_RANK:(h + 1) * w] = q_ref[:, (2 * h + 1) * LANES:(2 * h + 2) * LANES]


def _ctx_out_kernel(c_ref, wuv_ref, o_ref):
    for h in range(N_HEADS):
        c = c_ref[:, h * KV_RANK:(h + 1) * KV_RANK].astype(BF16)
        o_ref[:, h * V_DIM:(h + 1) * V_DIM] = _dot(
            c, wuv_ref[:, h * V_DIM:(h + 1) * V_DIM]).astype(BF16)


def _params(n_axes=1):
    return pltpu.CompilerParams(dimension_semantics=("arbitrary",) * n_axes,
                                vmem_limit_bytes=VMEM_LIMIT)


def _row_spec(width):
    return pl.BlockSpec((TM, width), lambda i: (i, 0))


def _table_spec(n_rows):
    tiles = n_rows // TM
    return pl.BlockSpec((TM, LANES), lambda i: (i % tiles, 0))


def _ffn_specs(layer):
    pick = lambda *_: (layer, 0, 0)
    return [_const_spec((1, D_MODEL)),
            pl.BlockSpec((None, D_MODEL, 2 * D_FF), pick, pipeline_mode=pl.Buffered(1)),
            pl.BlockSpec((None, D_FF, D_MODEL), pick, pipeline_mode=pl.Buffered(1))]


def _ffn_call(x, g, wgu, wd, layer):
    t = x.shape[0]
    return pl.pallas_call(
        _ffn_kernel, grid=(t // TM,),
        in_specs=[_row_spec(D_MODEL)] + _ffn_specs(layer),
        out_specs=_row_spec(D_MODEL),
        out_shape=jax.ShapeDtypeStruct((t, D_MODEL), F32),
        compiler_params=_params(), name="ffn")(x, g, wgu, wd)


def _mixer_call(x, g, win, cw, wout, tiles_per_seq, prev=None):
    t = x.shape[0]
    n_tiles = t // TM
    is_sample = prev is not None
    in_specs = [_row_spec(D_MODEL), _const_spec((1, D_MODEL)),
                _const_spec((D_MODEL, 3 * D_MODEL)), _const_spec((CONV_W, D_MODEL)),
                _const_spec((D_MODEL, D_MODEL))]
    args = [x, g, win, cw, wout]
    if is_sample:
        in_specs += [_row_spec(D_MODEL), _row_spec(D_MODEL)]
        args += list(prev)
        u_spec = _row_spec(D_MODEL)
        u_shape = jax.ShapeDtypeStruct((t, D_MODEL), F32)
    else:
        u_spec = pl.BlockSpec((1, SUBLANES, D_MODEL), lambda i: (i, 0, 0))
        u_shape = jax.ShapeDtypeStruct((n_tiles, SUBLANES, D_MODEL), F32)
    return pl.pallas_call(
        functools.partial(_mixer_kernel, tiles_per_seq, is_sample), grid=(n_tiles,),
        in_specs=in_specs,
        out_specs=[_row_spec(D_MODEL), u_spec],
        out_shape=[jax.ShapeDtypeStruct((t, D_MODEL), F32), u_shape],
        scratch_shapes=[pltpu.VMEM((TM + SUBLANES, D_MODEL), F32)],
        compiler_params=_params(), name="mixer")(*args)


def _ffn_kv_call(x, g, wgu, wd, layer, gkv, wdkv, kvn, ga, gb, cos_t, sin_t, wuk, wuvt, gkn):
    t = x.shape[0]
    in_specs = ([_row_spec(D_MODEL)] + _ffn_specs(layer) + [
        _const_spec((1, D_MODEL)), _const_spec((D_MODEL, KV_RANK + 2 * LANES)),
        _const_spec((1, KV_RANK)), _const_spec((1, LANES)), _const_spec((1, LANES)),
        _table_spec(cos_t.shape[0]), _table_spec(sin_t.shape[0]),
        _const_spec((KV_RANK, D_MODEL)), _const_spec((D_MODEL, KV_RANK)),
        _const_spec((1, NOPE_DIM))])
    out_specs = [_row_spec(D_MODEL), _row_spec(KV_RANK), _row_spec(ROPE_DIM),
                 _row_spec(2 * N_HEADS * LANES),
                 pl.BlockSpec((N_HEADS * V_ROWS, TM), lambda i: (0, i))]
    out_shape = [jax.ShapeDtypeStruct((t, D_MODEL), F32),
                 jax.ShapeDtypeStruct((t, KV_RANK), F32),
                 jax.ShapeDtypeStruct((t, ROPE_DIM), F32),
                 jax.ShapeDtypeStruct((t, 2 * N_HEADS * LANES), BF16),
                 jax.ShapeDtypeStruct((N_HEADS * V_ROWS, t), BF16)]
    return pl.pallas_call(
        _ffn_kv_kernel, grid=(t // TM,), in_specs=in_specs, out_specs=out_specs,
        out_shape=out_shape, compiler_params=_params(), name="ffn_kv")(
            x, g, wgu, wd, gkv, wdkv, kvn, ga, gb, cos_t, sin_t, wuk, wuvt, gkn)


def _ffn_q_call(x, g, wgu, wd, layer, gmix, wdq, qnorm, wuq, gqn, ga, gb, cos_t, sin_t):
    t = x.shape[0]
    in_specs = ([_row_spec(D_MODEL)] + _ffn_specs(layer) + [
        _const_spec((1, D_MODEL)), _const_spec((D_MODEL, Q_RANK)),
        _const_spec((1, Q_RANK)), _const_spec((Q_RANK, 3 * N_HEADS * LANES)),
        _const_spec((1, NOPE_DIM)), _const_spec((1, LANES)), _const_spec((1, LANES)),
        _table_spec(cos_t.shape[0]), _table_spec(sin_t.shape[0])])
    return pl.pallas_call(
        _ffn_q_kernel, grid=(t // TM,), in_specs=in_specs,
        out_specs=[_row_spec(D_MODEL), _row_spec(2 * N_HEADS * LANES)],
        out_shape=[jax.ShapeDtypeStruct((t, D_MODEL), F32),
                   jax.ShapeDtypeStruct((t, 2 * N_HEADS * LANES), BF16)],
        compiler_params=_params(), name="ffn_q")(
            x, g, wgu, wd, gmix, wdq, qnorm, wuq, gqn, ga, gb, cos_t, sin_t)


def _out_ffn_call(x, a, wo, g, wgu, wd, layer):
    t = x.shape[0]
    return pl.pallas_call(
        _out_ffn_kernel, grid=(t // TM,),
        in_specs=[_row_spec(D_MODEL), _row_spec(D_MODEL),
                  _const_spec((D_MODEL, D_MODEL))] + _ffn_specs(layer),
        out_specs=_row_spec(D_MODEL),
        out_shape=jax.ShapeDtypeStruct((t, D_MODEL), F32),
        compiler_params=_params(), name="out_ffn")(x, a, wo, g, wgu, wd)


def _prompt_attn_call(q, k, vt, batch, seq):
    nq = seq // TQ
    hps = HEADS_PER_STEP
    resident = pl.Buffered(1)
    return pl.pallas_call(
        _prompt_attn_kernel, grid=(batch, N_HEADS // hps, nq),
        in_specs=[pl.BlockSpec((TQ, hps * 2 * LANES), lambda b, h, i: (b * nq + i, h)),
                  pl.BlockSpec((seq, hps * 2 * LANES), lambda b, h, i: (b, h),
                               pipeline_mode=resident),
                  pl.BlockSpec((hps * V_ROWS, seq), lambda b, h, i: (h, b),
                               pipeline_mode=resident)],
        out_specs=pl.BlockSpec((TQ, hps * V_DIM), lambda b, h, i: (b * nq + i, h)),
        out_shape=jax.ShapeDtypeStruct((batch * seq, N_HEADS * V_DIM), BF16),
        scratch_shapes=[pltpu.VMEM((hps, TQ, TQ), F32), pltpu.VMEM((hps, TQ, TQ), F32),
                        pltpu.VMEM((hps, 1, TQ), F32), pltpu.VMEM((hps, 1, TQ), F32),
                        pltpu.VMEM((hps, 1, TQ), F32), pltpu.VMEM((hps, V_ROWS, TQ), F32)],
        compiler_params=_params(3), name="prompt_attn")(q, k, vt)


def _sample_attn_call(page_table, qa, qn, kn_new, lat_new, kpe_new, wukt,
                      cache_latent, cache_kpe):
    n_seq, n_pages = page_table.shape
    n_q = qa.shape[1]
    n_keys = n_pages * PAGE_SIZE
    grid_spec = pltpu.PrefetchScalarGridSpec(
        num_scalar_prefetch=1, grid=(n_seq,),
        in_specs=[pl.BlockSpec((1, n_q, KV_RANK + LANES), lambda b, pt: (b, 0, 0)),
                  pl.BlockSpec((1, n_q, NOPE_DIM), lambda b, pt: (b, 0, 0)),
                  pl.BlockSpec((1, kn_new.shape[1], NOPE_DIM), lambda b, pt: (b, 0, 0)),
                  pl.BlockSpec((1, SUBLANES, KV_RANK), lambda b, pt: (b, 0, 0)),
                  pl.BlockSpec((1, SUBLANES, ROPE_DIM), lambda b, pt: (b, 0, 0)),
                  pl.BlockSpec((N_HEADS * NOPE_DIM, KV_RANK), lambda b, pt: (0, 0),
                               pipeline_mode=pl.Buffered(1)),
                  pl.BlockSpec(memory_space=pl.ANY),
                  pl.BlockSpec(memory_space=pl.ANY)],
        out_specs=pl.BlockSpec((1, n_q, KV_RANK), lambda b, pt: (b, 0, 0)),
        scratch_shapes=[pltpu.VMEM((2, n_keys, KV_RANK), F32),
                        pltpu.VMEM((2, ROPE_DIM, n_keys), F32),
                        pltpu.VMEM((N_HEADS * NOPE_DIM + n_q, KV_RANK), BF16),
                        pltpu.SemaphoreType.DMA((2, 2))])
    return pl.pallas_call(
        functools.partial(_sample_attn_kernel, n_pages), grid_spec=grid_spec,
        out_shape=jax.ShapeDtypeStruct((n_seq, n_q, KV_RANK), F32),
        compiler_params=_params(), name="sample_attn")(
            page_table.reshape(-1), qa, qn, kn_new, lat_new, kpe_new, wukt, cache_latent,
            jnp.swapaxes(cache_kpe, 1, 2))


def _absorb_call(q, gkn, wukt):
    t = q.shape[0]
    return pl.pallas_call(
        _absorb_kernel, grid=(t // TM,),
        in_specs=[_row_spec(2 * N_HEADS * LANES), _const_spec((1, NOPE_DIM)),
                  _const_spec((N_HEADS * NOPE_DIM, KV_RANK))],
        out_specs=_row_spec(N_HEADS * (KV_RANK + LANES)),
        out_shape=jax.ShapeDtypeStruct((t, N_HEADS * (KV_RANK + LANES)), BF16),
        compiler_params=_params(), name="absorb")(q, gkn, wukt)


def _ctx_out_call(ctx, wuv):
    t = ctx.shape[0]
    return pl.pallas_call(
        _ctx_out_kernel, grid=(t // TM,),
        in_specs=[_row_spec(N_HEADS * KV_RANK),
                  _const_spec((KV_RANK, N_HEADS * V_DIM))],
        out_specs=_row_spec(N_HEADS * V_DIM),
        out_shape=jax.ShapeDtypeStruct((t, N_HEADS * V_DIM), BF16),
        compiler_params=_params(), name="ctx_out")(ctx, wuv)


def _rope_tables(pos):
    half = ROPE_DIM // 2
    freqs = ROPE_THETA ** (-jnp.arange(half, dtype=F32) / half)
    ang = pos.astype(F32)[:, None] * freqs
    cos = jnp.cos(ang)
    sin = jnp.sin(ang)
    zero = jnp.zeros((pos.shape[0], LANES - ROPE_DIM), F32)
    return (jnp.concatenate([cos, cos, zero], axis=1),
            jnp.concatenate([-sin, sin, zero], axis=1))


def _swap_halves(a):
    half = a.shape[-1] // 2
    return jnp.concatenate([a[..., half:], a[..., :half]], axis=-1)


def _pad_lanes(a):
    pad = [(0, 0)] * (a.ndim - 1) + [(0, LANES - a.shape[-1])]
    return jnp.pad(a, pad)


def kernel(x_prompt, x_sample, cache_latent, cache_kpe, state_conv, page_table, norm_ffn1, w_ffn1_gu, w_ffn1_down, norm_mix, norm_ffn2, w_ffn2_gu, w_ffn2_down, w_conv_in, conv_w, w_conv_out, w_dq, q_norm, w_uq, q_nope_norm, q_pe_norm, w_o, norm_kv_in, w_dkv, kv_norm, k_pe_norm, w_uk, w_uv, k_nope_norm):
    batch, seq, _ = x_prompt.shape
    n_dec, dec_seq, _ = x_sample.shape
    n_pages = page_table.shape[1]
    assert seq % TM == 0 and (n_dec * dec_seq) % TM == 0 and dec_seq == 4

    row = lambda a: a.reshape(1, -1)
    ffn1 = (w_ffn1_gu.astype(BF16), w_ffn1_down.astype(BF16))
    ffn2 = (w_ffn2_gu.astype(BF16), w_ffn2_down.astype(BF16))
    win = w_conv_in[0].astype(BF16)
    wout = w_conv_out[0].astype(BF16)
    w_pe = w_dkv[:, KV_RANK:]
    wdkv = jnp.concatenate([w_dkv[:, :KV_RANK], _pad_lanes(w_pe),
                            _pad_lanes(_swap_halves(w_pe))], axis=1).astype(BF16)
    wuk = w_uk.astype(BF16)
    wuvt = w_uv.T.astype(BF16)
    wukt = w_uk.T.astype(BF16)
    wuv = w_uv.astype(BF16)
    wdq = w_dq[0].astype(BF16)
    wuq3 = w_uq[0].reshape(Q_RANK, N_HEADS, NOPE_DIM + ROPE_DIM)
    wq_pe = wuq3[:, :, NOPE_DIM:]
    wuq = jnp.concatenate([
        wuq3[:, :, :NOPE_DIM].reshape(Q_RANK, -1),
        _pad_lanes(wq_pe).reshape(Q_RANK, -1),
        _pad_lanes(_swap_halves(wq_pe)).reshape(Q_RANK, -1)], axis=1).astype(BF16)
    wo = w_o[0].astype(BF16)
    kga = row(_pad_lanes(k_pe_norm))
    kgb = row(_pad_lanes(_swap_halves(k_pe_norm)))
    qga = row(_pad_lanes(q_pe_norm[0]))
    qgb = row(_pad_lanes(_swap_halves(q_pe_norm[0])))

    cos_p, sin_p = _rope_tables(jnp.arange(seq))
    pos_s = n_pages * PAGE_SIZE + jnp.arange(n_dec * dec_seq) % dec_seq
    cos_s, sin_s = _rope_tables(pos_s)

    prev = state_conv[0]
    zrow = jnp.zeros((n_dec, 1, D_MODEL), F32)
    p1 = jnp.concatenate([prev[:, 1:2], zrow, zrow, zrow], axis=1).reshape(-1, D_MODEL)
    p2 = jnp.concatenate([prev[:, 0:1], prev[:, 1:2], zrow, zrow], axis=1).reshape(-1, D_MODEL)

    def dense(x, tiles_per_seq, cos_t, sin_t, prev_rows):
        x = _ffn_call(x, row(norm_ffn1[0]), *ffn1, 0)
        x, u = _mixer_call(x, row(norm_mix[0]), win, conv_w[0], wout, tiles_per_seq, prev_rows)
        x, lat, kpe, k, vt = _ffn_kv_call(
            x, row(norm_ffn2[0]), *ffn2, 0, row(norm_kv_in), wdkv, row(kv_norm),
            kga, kgb, cos_t, sin_t, wuk, wuvt, row(k_nope_norm))
        x, q = _ffn_q_call(
            x, row(norm_ffn1[1]), *ffn1, 1, row(norm_mix[1]), wdq, row(q_norm[0]), wuq,
            row(q_nope_norm[0]), qga, qgb, cos_t, sin_t)
        return x, u, lat, kpe, k, vt, q

    xp, up, lat_p, kpe_p, k_p, vt_p, q_p = dense(
        x_prompt.reshape(batch * seq, D_MODEL), seq // TM, cos_p, sin_p, None)
    o_p = _prompt_attn_call(q_p, k_p, vt_p, batch, seq)
    y_p = _out_ffn_call(xp, o_p, wo, row(norm_ffn2[1]), *ffn2, 1)
    conv_p = up.reshape(batch, seq // TM, SUBLANES, D_MODEL)[:, -1, SUBLANES - 2:][None]

    xs, us, lat_s, kpe_s, k_s, _, q_s = dense(
        x_sample.reshape(n_dec * dec_seq, D_MODEL), 1, cos_s, sin_s, (p1, p2))
    qa = _absorb_call(q_s, row(k_nope_norm), wukt)
    qa = qa.reshape(n_dec, dec_seq * N_HEADS, KV_RANK + LANES)
    pad_rows = ((0, 0), (0, SUBLANES - dec_seq), (0, 0))
    lat_new = jnp.pad(lat_s.reshape(n_dec, dec_seq, KV_RANK), pad_rows)
    kpe_new = jnp.pad(kpe_s.reshape(n_dec, dec_seq, ROPE_DIM), pad_rows)
    qn_s = q_s.reshape(n_dec, dec_seq * N_HEADS, 2 * LANES)[:, :, :NOPE_DIM]
    kn_new = k_s.reshape(n_dec, dec_seq * N_HEADS, 2 * LANES)[:, :, :NOPE_DIM]
    ctx = _sample_attn_call(page_table, qa, qn_s, kn_new, lat_new, kpe_new, wukt,
                            cache_latent, cache_kpe)
    o_s = _ctx_out_call(ctx.reshape(n_dec * dec_seq, N_HEADS * KV_RANK), wuv)
    y_s = _out_ffn_call(xs, o_s, wo, row(norm_ffn2[1]), *ffn2, 1)
    conv_s = us.reshape(n_dec, dec_seq, D_MODEL)[:, dec_seq - (CONV_W - 1):][None]

    return (y_p.reshape(batch, seq, D_MODEL),
            y_s.reshape(n_dec, dec_seq, D_MODEL),
            conv_p, conv_s,
            lat_p.reshape(batch, seq, KV_RANK),
            kpe_p.reshape(batch, seq, ROPE_DIM),
            lat_s.reshape(n_dec, dec_seq, KV_RANK),
            kpe_s.reshape(n_dec, dec_seq, ROPE_DIM))
```

```python
import functools

import jax
import jax.numpy as jnp
from jax import lax
from jax.experimental import pallas as pl
from jax.experimental.pallas import tpu as pltpu

F32 = jnp.float32
BF16 = jnp.bfloat16

D_MODEL = 1024
D_FF = 2816
N_HEADS = 8
NOPE_DIM = 128
ROPE_DIM = 64
V_DIM = 128
Q_RANK = 512
KV_RANK = 256
CONV_W = 3
PAGE_SIZE = 128
ROPE_THETA = 10000.0
EPS = 1e-6
SCALE = (NOPE_DIM + ROPE_DIM) ** -0.5
LOG2E = 1.4426950408889634
Q_SCALE = SCALE * LOG2E
V_ROWS = V_DIM + 16
NEG = -1e30

LANES = 128
SUBLANES = 8
TM = 512
FF_CHUNKS = 1
FC = D_FF // FF_CHUNKS
TQ = 512
HEADS_PER_STEP = 4
KEY_CHUNK = 1024
VMEM_LIMIT = 56 * 1024 * 1024

_NT = (((1,), (1,)), ((), ()))


def _const_spec(shape):
    zeros = (0,) * len(shape)
    return pl.BlockSpec(shape, lambda *_: zeros, pipeline_mode=pl.Buffered(1))


def _rms(x, g):
    ms = jnp.mean(x * x, axis=-1, keepdims=True)
    return x * lax.rsqrt(ms + EPS) * g


def _dot(a, b):
    return jnp.dot(a, b, preferred_element_type=F32)


def _ffn(x, g, wgu_ref, wd_ref):
    xn = _rms(x, g).astype(BF16)
    acc = None
    for c in range(FF_CHUNKS):
        gate = _dot(xn, wgu_ref[:, c * FC:(c + 1) * FC])
        up = _dot(xn, wgu_ref[:, D_FF + c * FC:D_FF + (c + 1) * FC])
        act = (gate * jax.nn.sigmoid(gate) * up).astype(BF16)
        part = _dot(act, wd_ref[c * FC:(c + 1) * FC, :])
        acc = part if acc is None else acc + part
    return x + 0.5 * acc


def _rope_norm(a, b, ga, gb, cos_t, sin_t):
    ms = jnp.sum(a * a, axis=-1, keepdims=True) * (1.0 / ROPE_DIM)
    return (a * ga * cos_t + b * gb * sin_t) * lax.rsqrt(ms + EPS)


def _ffn_kernel(x_ref, g_ref, wgu_ref, wd_ref, o_ref):
    o_ref[...] = _ffn(x_ref[...], g_ref[...], wgu_ref, wd_ref)


def _mixer_kernel(tiles_per_seq, is_sample, *refs):
    if is_sample:
        (x_ref, g_ref, win_ref, cw_ref, wout_ref, p1_ref, p2_ref,
         o_ref, u_ref, ubuf) = refs
    else:
        x_ref, g_ref, win_ref, cw_ref, wout_ref, o_ref, u_ref, ubuf = refs
    i = pl.program_id(0)
    x = x_ref[...]
    xn = _rms(x, g_ref[...]).astype(BF16)
    bch = _dot(xn, win_ref[...])
    b = bch[:, :D_MODEL]
    u = bch[:, D_MODEL:2 * D_MODEL] * bch[:, 2 * D_MODEL:]

    @pl.when(i % tiles_per_seq == 0)
    def _():
        ubuf[0:SUBLANES, :] = jnp.zeros((SUBLANES, D_MODEL), F32)

    ubuf[SUBLANES:SUBLANES + TM, :] = u
    s1 = ubuf[SUBLANES - 1:SUBLANES - 1 + TM, :]
    s2 = ubuf[SUBLANES - 2:SUBLANES - 2 + TM, :]
    if is_sample:
        t = lax.broadcasted_iota(jnp.int32, (TM, 1), 0) % 4
        s1 = jnp.where(t >= 1, s1, p1_ref[...])
        s2 = jnp.where(t >= 2, s2, p2_ref[...])
    cw = cw_ref[...]
    y = cw[0:1, :] * s2 + cw[1:2, :] * s1 + cw[2:3, :] * u
    o_ref[...] = x + _dot((b * y).astype(BF16), wout_ref[...])
    if is_sample:
        u_ref[...] = u
    else:
        ubuf[0:SUBLANES, :] = u[TM - SUBLANES:, :]
        u_ref[0] = u[TM - SUBLANES:, :]


def _ffn_kv_kernel(x_ref, g_ref, wgu_ref, wd_ref, gkv_ref, wdkv_ref, kvn_ref,
                   ga_ref, gb_ref, cos_ref, sin_ref, wuk_ref, wuvt_ref, gkn_ref,
                   o_ref, lat_ref, kpe_ref, k_ref, vt_ref):
    x = _ffn(x_ref[...], g_ref[...], wgu_ref, wd_ref)
    o_ref[...] = x
    hk = _rms(x, gkv_ref[...]).astype(BF16)
    ckv = _dot(hk, wdkv_ref[...])
    lat = _rms(ckv[:, :KV_RANK], kvn_ref[...])
    lat_ref[...] = lat
    kpe = _rope_norm(ckv[:, KV_RANK:KV_RANK + LANES], ckv[:, KV_RANK + LANES:],
                     ga_ref[...], gb_ref[...], cos_ref[...], sin_ref[...])
    kpe_ref[...] = kpe[:, :ROPE_DIM]
    kpeb = kpe.astype(BF16)
    latb = lat.astype(BF16)
    ek = _dot(latb, wuk_ref[...])
    gkn = gkn_ref[...]
    for h in range(N_HEADS):
        e = ek[:, h * NOPE_DIM:(h + 1) * NOPE_DIM]
        k_ref[:, 2 * h * LANES:(2 * h + 1) * LANES] = _rms(e, gkn).astype(BF16)
        k_ref[:, (2 * h + 1) * LANES:(2 * h + 2) * LANES] = kpeb
    vt = lax.dot_general(wuvt_ref[...], latb, _NT, preferred_element_type=F32).astype(BF16)
    ones = jnp.ones((V_ROWS - V_DIM, vt.shape[1]), BF16)
    for h in range(N_HEADS):
        vt_ref[h * V_ROWS:h * V_ROWS + V_DIM, :] = vt[h * V_DIM:(h + 1) * V_DIM, :]
        vt_ref[h * V_ROWS + V_DIM:(h + 1) * V_ROWS, :] = ones


def _ffn_q_kernel(x_ref, g_ref, wgu_ref, wd_ref, gmix_ref, wdq_ref, qn_ref,
                  wuq_ref, gqn_ref, ga_ref, gb_ref, cos_ref, sin_ref,
                  o_ref, q_ref):
    x = _ffn(x_ref[...], g_ref[...], wgu_ref, wd_ref)
    o_ref[...] = x
    xn = _rms(x, gmix_ref[...]).astype(BF16)
    cq = _rms(_dot(xn, wdq_ref[...]), qn_ref[...]).astype(BF16)
    q = _dot(cq, wuq_ref[...])
    gqn = gqn_ref[...]
    ga = ga_ref[...]
    gb = gb_ref[...]
    cos_t = cos_ref[...]
    sin_t = sin_ref[...]
    hw = N_HEADS * LANES
    for h in range(N_HEADS):
        qn = _rms(q[:, h * LANES:(h + 1) * LANES], gqn) * Q_SCALE
        qp = _rope_norm(q[:, hw + h * LANES:hw + (h + 1) * LANES],
                        q[:, 2 * hw + h * LANES:2 * hw + (h + 1) * LANES],
                        ga, gb, cos_t, sin_t) * Q_SCALE
        q_ref[:, 2 * h * LANES:(2 * h + 1) * LANES] = qn.astype(BF16)
        q_ref[:, (2 * h + 1) * LANES:(2 * h + 2) * LANES] = qp.astype(BF16)


def _out_ffn_kernel(x_ref, a_ref, wo_ref, g_ref, wgu_ref, wd_ref, o_ref):
    x = x_ref[...] + _dot(a_ref[...], wo_ref[...])
    o_ref[...] = _ffn(x, g_ref[...], wgu_ref, wd_ref)


def _prompt_attn_kernel(q_ref, k_ref, vt_ref, o_ref,
                        sa_ref, sb_ref, smaxa_ref, smaxb_ref, m_scr, acc_scr):
    qi = pl.program_id(2)

    def scores(slot, ki, j, masked=False):
        s_ref, smax_ref = slot
        off = pl.multiple_of(ki * TQ, TQ)
        s = lax.dot_general(k_ref[pl.ds(off, TQ), 2 * j * LANES:2 * (j + 1) * LANES],
                            q_ref[:, 2 * j * LANES:2 * (j + 1) * LANES], _NT,
                            preferred_element_type=F32)
        if masked:
            key = lax.broadcasted_iota(jnp.int32, (TQ, TQ), 0)
            qry = lax.broadcasted_iota(jnp.int32, (TQ, TQ), 1)
            s = jnp.where(qry >= key, s, NEG)
        s_ref[j] = s
        smax_ref[j] = jnp.max(s, axis=0, keepdims=True)

    def accumulate(slot, ki, j):
        s_ref, smax_ref = slot
        off = pl.multiple_of(ki * TQ, TQ)
        m = m_scr[j]
        vt = vt_ref[j * V_ROWS:(j + 1) * V_ROWS, pl.ds(off, TQ)]
        m_new = jnp.maximum(m, smax_ref[j])
        alpha = jnp.exp2(m - m_new)
        p = jnp.exp2((s_ref[j] - m_new).astype(BF16))
        acc_scr[j] = alpha * acc_scr[j] + _dot(vt, p)
        m_scr[j] = m_new

    def advance(slot_next, ki_next, slot_cur, ki_cur):
        for j in range(HEADS_PER_STEP):
            scores(slot_next, ki_next, j)
            accumulate(slot_cur, ki_cur, j)

    for j in range(HEADS_PER_STEP):
        m_scr[j] = jnp.full((1, TQ), NEG, F32)
        acc_scr[j] = jnp.zeros((V_ROWS, TQ), F32)

    slot_a = (sa_ref, smaxa_ref)
    slot_b = (sb_ref, smaxb_ref)
    n_pairs = qi // 2
    for j in range(HEADS_PER_STEP):
        scores(slot_a, qi, j, masked=True)

    def pair(t, _):
        advance(slot_b, 2 * t, slot_a, jnp.where(t == 0, qi, 2 * t - 1))
        advance(slot_a, 2 * t + 1, slot_b, 2 * t)
        return 0

    lax.fori_loop(0, n_pairs, pair, 0)
    in_a = jnp.where(n_pairs == 0, qi, 2 * n_pairs - 1)

    @pl.when(qi % 2 == 1)
    def _():
        advance(slot_b, qi - 1, slot_a, in_a)
        for j in range(HEADS_PER_STEP):
            accumulate(slot_b, qi - 1, j)

    @pl.when(qi % 2 == 0)
    def _():
        for j in range(HEADS_PER_STEP):
            accumulate(slot_a, in_a, j)

    for j in range(HEADS_PER_STEP):
        acc = acc_scr[j]
        o = acc[:V_DIM] / acc[V_DIM:V_DIM + 1]
        o_ref[:, j * V_DIM:(j + 1) * V_DIM] = o.T.astype(BF16)


def _sample_attn_kernel(n_pages, pt_ref, qa_ref, qn_ref, knn_ref, latn_ref, kpen_ref,
                        wukt_ref, lat_hbm, kpet_hbm, ctx_ref,
                        latbuf, kpebuf, wq, sems):
    n_seq = pl.num_programs(0)
    b = pl.program_id(0)
    slot = b % 2
    n_past = n_pages * PAGE_SIZE
    n_chunks = n_past // KEY_CHUNK
    pages_per_chunk = n_pages // n_chunks
    n_w = N_HEADS * NOPE_DIM

    def page_copies(seq, slot_, j):
        pg = pt_ref[seq * n_pages + j]
        row = pl.multiple_of(j * PAGE_SIZE, PAGE_SIZE)
        return (
            pltpu.make_async_copy(lat_hbm.at[pg], latbuf.at[slot_, pl.ds(row, PAGE_SIZE)],
                                  sems.at[0, slot_]),
            pltpu.make_async_copy(kpet_hbm.at[pg], kpebuf.at[slot_, :, pl.ds(row, PAGE_SIZE)],
                                  sems.at[1, slot_]),
        )

    def start_pages(seq, slot_, first, count):
        for j in range(count):
            for cp in page_copies(seq, slot_, first + j):
                cp.start()

    def wait_fetch(slot_):
        pltpu.make_async_copy(latbuf.at[1 - slot_], latbuf.at[slot_], sems.at[0, slot_]).wait()
        pltpu.make_async_copy(kpebuf.at[1 - slot_], kpebuf.at[slot_], sems.at[1, slot_]).wait()

    @pl.when(b == 0)
    def _():
        def first_fetch(c, _):
            start_pages(0, 0, c * pages_per_chunk, pages_per_chunk)
            return 0
        lax.fori_loop(0, n_chunks, first_fetch, 0)
        wq[0:n_w, :] = wukt_ref[...]

    wait_fetch(slot)

    qa = qa_ref[0]
    n_q = qa.shape[0]
    wq[n_w:n_w + n_q, :] = qa[:, :KV_RANK]
    q_pe = qa[:, KV_RANK:KV_RANK + ROPE_DIM]
    wq_all = wq[...]

    def scores(latc, pe):
        n = latc.shape[0]
        e = lax.dot_general(wq_all, latc, _NT, preferred_element_type=F32)
        e3 = e[:n_w].reshape(N_HEADS, NOPE_DIM, n)
        r = lax.rsqrt(jnp.sum(e3 * e3, axis=1) * (1.0 / NOPE_DIM) + EPS)
        raw3 = e[n_w:].reshape(n_q // N_HEADS, N_HEADS, n) * r[None]
        return raw3.reshape(n_q, n) + pe

    pad = jnp.zeros((LANES - SUBLANES, KV_RANK), F32)
    lat_new = jnp.concatenate([latn_ref[0], pad], axis=0).astype(BF16)
    kpec = jnp.concatenate([kpen_ref[0], pad[:, :ROPE_DIM]], axis=0).astype(BF16)
    s_new = lax.dot_general(q_pe, kpec, _NT, preferred_element_type=F32)
    cross = lax.dot_general(qn_ref[0], knn_ref[0], _NT, preferred_element_type=F32)
    n_new = knn_ref.shape[1] // N_HEADS
    row = lax.broadcasted_iota(jnp.int32, cross.shape, 0)
    col = lax.broadcasted_iota(jnp.int32, cross.shape, 1)
    same_head = row % N_HEADS == col % N_HEADS
    t_q = lax.broadcasted_iota(jnp.int32, (n_q, LANES), 0) // N_HEADS
    j_k = lax.broadcasted_iota(jnp.int32, (n_q, LANES), 1)
    for j in range(n_new):
        pick = jnp.where(same_head & (col // N_HEADS == j), cross, 0.0)
        s_new = s_new + jnp.where(j_k == j, jnp.sum(pick, axis=-1, keepdims=True), 0.0)
    s_new = jnp.where(j_k <= t_q, s_new, NEG)

    m = jnp.max(s_new, axis=-1, keepdims=True)
    p = jnp.exp2(s_new - m)
    l = jnp.sum(p, axis=-1, keepdims=True)
    acc = _dot(p.astype(BF16), lat_new)

    pending = None
    for c in range(n_chunks):
        start_pages((b + 1) % n_seq, 1 - slot, c * pages_per_chunk, pages_per_chunk)
        keys = slice(c * KEY_CHUNK, (c + 1) * KEY_CHUNK)
        latc = latbuf[slot, keys, :].astype(BF16)
        kpec = kpebuf[slot, :, keys].astype(BF16)
        s = scores(latc, _dot(q_pe, kpec))
        if pending is not None:
            alpha, p_prev, lat_prev = pending
            acc = alpha * acc + _dot(p_prev, lat_prev)
        m_new = jnp.maximum(m, jnp.max(s, axis=-1, keepdims=True))
        alpha = jnp.exp2(m - m_new)
        p = jnp.exp2(s - m_new)
        l = alpha * l + jnp.sum(p, axis=-1, keepdims=True)
        m = m_new
        pending = (alpha, p.astype(BF16), latc)
    alpha, p_prev, lat_prev = pending
    acc = alpha * acc + _dot(p_prev, lat_prev)
    ctx_ref[0] = acc / l

    @pl.when(b == n_seq - 1)
    def _():
        wait_fetch(1 - slot)


def _absorb_kernel(q_ref, gkn_ref, wukt_ref, o_ref):
    gkn = gkn_ref[...]
    w = KV_RANK + LANES
    for h in range(N_HEADS):
        qn = (q_ref[:, 2 * h * LANES:(2 * h + 1) * LANES].astype(F32) * gkn).astype(BF16)
        o_ref[:, h * w:h * w + KV_RANK] = _dot(
            qn, wukt_ref[h * NOPE_DIM:(h + 1) * NOPE_DIM, :]).astype(BF16)
        o_ref[:, h * w + KV_RANK:(h + 1) * w] = q_ref[:, (2 * h + 1) * LANES:(2 * h + 2) * LANES]


def _ctx_out_kernel(c_ref, wuv_ref, o_ref):
    for h in range(N_HEADS):
        c = c_ref[:, h * KV_RANK:(h + 1) * KV_RANK].astype(BF16)
        o_ref[:, h * V_DIM:(h + 1) * V_DIM] = _dot(
            c, wuv_ref[:, h * V_DIM:(h + 1) * V_DIM]).astype(BF16)


def _params(n_axes=1):
    return pltpu.CompilerParams(dimension_semantics=("arbitrary",) * n_axes,
                                vmem_limit_bytes=VMEM_LIMIT)


def _row_spec(width):
    return pl.BlockSpec((TM, width), lambda i: (i, 0))


def _table_spec(n_rows):
    tiles = n_rows // TM
    return pl.BlockSpec((TM, LANES), lambda i: (i % tiles, 0))


def _ffn_specs(layer):
    pick = lambda *_: (layer, 0, 0)
    return [_const_spec((1, D_MODEL)),
            pl.BlockSpec((None, D_MODEL, 2 * D_FF), pick, pipeline_mode=pl.Buffered(1)),
            pl.BlockSpec((None, D_FF, D_MODEL), pick, pipeline_mode=pl.Buffered(1))]


def _ffn_call(x, g, wgu, wd, layer):
    t = x.shape[0]
    return pl.pallas_call(
        _ffn_kernel, grid=(t // TM,),
        in_specs=[_row_spec(D_MODEL)] + _ffn_specs(layer),
        out_specs=_row_spec(D_MODEL),
        out_shape=jax.ShapeDtypeStruct((t, D_MODEL), F32),
        compiler_params=_params(), name="ffn")(x, g, wgu, wd)


def _mixer_call(x, g, win, cw, wout, tiles_per_seq, prev=None):
    t = x.shape[0]
    n_tiles = t // TM
    is_sample = prev is not None
    in_specs = [_row_spec(D_MODEL), _const_spec((1, D_MODEL)),
                _const_spec((D_MODEL, 3 * D_MODEL)), _const_spec((CONV_W, D_MODEL)),
                _const_spec((D_MODEL, D_MODEL))]
    args = [x, g, win, cw, wout]
    if is_sample:
        in_specs += [_row_spec(D_MODEL), _row_spec(D_MODEL)]
        args += list(prev)
        u_spec = _row_spec(D_MODEL)
        u_shape = jax.ShapeDtypeStruct((t, D_MODEL), F32)
    else:
        u_spec = pl.BlockSpec((1, SUBLANES, D_MODEL), lambda i: (i, 0, 0))
        u_shape = jax.ShapeDtypeStruct((n_tiles, SUBLANES, D_MODEL), F32)
    return pl.pallas_call(
        functools.partial(_mixer_kernel, tiles_per_seq, is_sample), grid=(n_tiles,),
        in_specs=in_specs,
        out_specs=[_row_spec(D_MODEL), u_spec],
        out_shape=[jax.ShapeDtypeStruct((t, D_MODEL), F32), u_shape],
        scratch_shapes=[pltpu.VMEM((TM + SUBLANES, D_MODEL), F32)],
        compiler_params=_params(), name="mixer")(*args)


def _ffn_kv_call(x, g, wgu, wd, layer, gkv, wdkv, kvn, ga, gb, cos_t, sin_t, wuk, wuvt, gkn):
    t = x.shape[0]
    in_specs = ([_row_spec(D_MODEL)] + _ffn_specs(layer) + [
        _const_spec((1, D_MODEL)), _const_spec((D_MODEL, KV_RANK + 2 * LANES)),
        _const_spec((1, KV_RANK)), _const_spec((1, LANES)), _const_spec((1, LANES)),
        _table_spec(cos_t.shape[0]), _table_spec(sin_t.shape[0]),
        _const_spec((KV_RANK, D_MODEL)), _const_spec((D_MODEL, KV_RANK)),
        _const_spec((1, NOPE_DIM))])
    out_specs = [_row_spec(D_MODEL), _row_spec(KV_RANK), _row_spec(ROPE_DIM),
                 _row_spec(2 * N_HEADS * LANES),
                 pl.BlockSpec((N_HEADS * V_ROWS, TM), lambda i: (0, i))]
    out_shape = [jax.ShapeDtypeStruct((t, D_MODEL), F32),
                 jax.ShapeDtypeStruct((t, KV_RANK), F32),
                 jax.ShapeDtypeStruct((t, ROPE_DIM), F32),
                 jax.ShapeDtypeStruct((t, 2 * N_HEADS * LANES), BF16),
                 jax.ShapeDtypeStruct((N_HEADS * V_ROWS, t), BF16)]
    return pl.pallas_call(
        _ffn_kv_kernel, grid=(t // TM,), in_specs=in_specs, out_specs=out_specs,
        out_shape=out_shape, compiler_params=_params(), name="ffn_kv")(
            x, g, wgu, wd, gkv, wdkv, kvn, ga, gb, cos_t, sin_t, wuk, wuvt, gkn)


def _ffn_q_call(x, g, wgu, wd, layer, gmix, wdq, qnorm, wuq, gqn, ga, gb, cos_t, sin_t):
    t = x.shape[0]
    in_specs = ([_row_spec(D_MODEL)] + _ffn_specs(layer) + [
        _const_spec((1, D_MODEL)), _const_spec((D_MODEL, Q_RANK)),
        _const_spec((1, Q_RANK)), _const_spec((Q_RANK, 3 * N_HEADS * LANES)),
        _const_spec((1, NOPE_DIM)), _const_spec((1, LANES)), _const_spec((1, LANES)),
        _table_spec(cos_t.shape[0]), _table_spec(sin_t.shape[0])])
    return pl.pallas_call(
        _ffn_q_kernel, grid=(t // TM,), in_specs=in_specs,
        out_specs=[_row_spec(D_MODEL), _row_spec(2 * N_HEADS * LANES)],
        out_shape=[jax.ShapeDtypeStruct((t, D_MODEL), F32),
                   jax.ShapeDtypeStruct((t, 2 * N_HEADS * LANES), BF16)],
        compiler_params=_params(), name="ffn_q")(
            x, g, wgu, wd, gmix, wdq, qnorm, wuq, gqn, ga, gb, cos_t, sin_t)


def _out_ffn_call(x, a, wo, g, wgu, wd, layer):
    t = x.shape[0]
    return pl.pallas_call(
        _out_ffn_kernel, grid=(t // TM,),
        in_specs=[_row_spec(D_MODEL), _row_spec(D_MODEL),
                  _const_spec((D_MODEL, D_MODEL))] + _ffn_specs(layer),
        out_specs=_row_spec(D_MODEL),
        out_shape=jax.ShapeDtypeStruct((t, D_MODEL), F32),
        compiler_params=_params(), name="out_ffn")(x, a, wo, g, wgu, wd)


def _prompt_attn_call(q, k, vt, batch, seq):
    nq = seq // TQ
    hps = HEADS_PER_STEP
    resident = pl.Buffered(1)
    return pl.pallas_call(
        _prompt_attn_kernel, grid=(batch, N_HEADS // hps, nq),
        in_specs=[pl.BlockSpec((TQ, hps * 2 * LANES), lambda b, h, i: (b * nq + i, h)),
                  pl.BlockSpec((seq, hps * 2 * LANES), lambda b, h, i: (b, h),
                               pipeline_mode=resident),
                  pl.BlockSpec((hps * V_ROWS, seq), lambda b, h, i: (h, b),
                               pipeline_mode=resident)],
        out_specs=pl.BlockSpec((TQ, hps * V_DIM), lambda b, h, i: (b * nq + i, h)),
        out_shape=jax.ShapeDtypeStruct((batch * seq, N_HEADS * V_DIM), BF16),
        scratch_shapes=[pltpu.VMEM((hps, TQ, TQ), F32), pltpu.VMEM((hps, TQ, TQ), F32),
                        pltpu.VMEM((hps, 1, TQ), F32), pltpu.VMEM((hps, 1, TQ), F32),
                        pltpu.VMEM((hps, 1, TQ), F32), pltpu.VMEM((hps, V_ROWS, TQ), F32)],
        compiler_params=_params(3), name="prompt_attn")(q, k, vt)


def _sample_attn_call(page_table, qa, qn, kn_new, lat_new, kpe_new, wukt,
                      cache_latent, cache_kpe):
    n_seq, n_pages = page_table.shape
    n_q = qa.shape[1]
    n_keys = n_pages * PAGE_SIZE
    grid_spec = pltpu.PrefetchScalarGridSpec(
        num_scalar_prefetch=1, grid=(n_seq,),
        in_specs=[pl.BlockSpec((1, n_q, KV_RANK + LANES), lambda b, pt: (b, 0, 0)),
                  pl.BlockSpec((1, n_q, NOPE_DIM), lambda b, pt: (b, 0, 0)),
                  pl.BlockSpec((1, kn_new.shape[1], NOPE_DIM), lambda b, pt: (b, 0, 0)),
                  pl.BlockSpec((1, SUBLANES, KV_RANK), lambda b, pt: (b, 0, 0)),
                  pl.BlockSpec((1, SUBLANES, ROPE_DIM), lambda b, pt: (b, 0, 0)),
                  pl.BlockSpec((N_HEADS * NOPE_DIM, KV_RANK), lambda b, pt: (0, 0),
                               pipeline_mode=pl.Buffered(1)),
                  pl.BlockSpec(memory_space=pl.ANY),
                  pl.BlockSpec(memory_space=pl.ANY)],
        out_specs=pl.BlockSpec((1, n_q, KV_RANK), lambda b, pt: (b, 0, 0)),
        scratch_shapes=[pltpu.VMEM((2, n_keys, KV_RANK), F32),
                        pltpu.VMEM((2, ROPE_DIM, n_keys), F32),
                        pltpu.VMEM((N_HEADS * NOPE_DIM + n_q, KV_RANK), BF16),
                        pltpu.SemaphoreType.DMA((2, 2))])
    return pl.pallas_call(
        functools.partial(_sample_attn_kernel, n_pages), grid_spec=grid_spec,
        out_shape=jax.ShapeDtypeStruct((n_seq, n_q, KV_RANK), F32),
        compiler_params=_params(), name="sample_attn")(
            page_table.reshape(-1), qa, qn, kn_new, lat_new, kpe_new, wukt, cache_latent,
            jnp.swapaxes(cache_kpe, 1, 2))


def _absorb_call(q, gkn, wukt):
    t = q.shape[0]
    return pl.pallas_call(
        _absorb_kernel, grid=(t // TM,),
        in_specs=[_row_spec(2 * N_HEADS * LANES), _const_spec((1, NOPE_DIM)),
                  _const_spec((N_HEADS * NOPE_DIM, KV_RANK))],
        out_specs=_row_spec(N_HEADS * (KV_RANK + LANES)),
        out_shape=jax.ShapeDtypeStruct((t, N_HEADS * (KV_RANK + LANES)), BF16),
        compiler_params=_params(), name="absorb")(q, gkn, wukt)


def _ctx_out_call(ctx, wuv):
    t = ctx.shape[0]
    return pl.pallas_call(
        _ctx_out_kernel, grid=(t // TM,),
        in_specs=[_row_spec(N_HEADS * KV_RANK),
                  _const_spec((KV_RANK, N_HEADS * V_DIM))],
        out_specs=_row_spec(N_HEADS * V_DIM),
        out_shape=jax.ShapeDtypeStruct((t, N_HEADS * V_DIM), BF16),
        compiler_params=_params(), name="ctx_out")(ctx, wuv)


def _rope_tables(pos):
    half = ROPE_DIM // 2
    freqs = ROPE_THETA ** (-jnp.arange(half, dtype=F32) / half)
    ang = pos.astype(F32)[:, None] * freqs
    cos = jnp.cos(ang)
    sin = jnp.sin(ang)
    zero = jnp.zeros((pos.shape[0], LANES - ROPE_DIM), F32)
    return (jnp.concatenate([cos, cos, zero], axis=1),
            jnp.concatenate([-sin, sin, zero], axis=1))


def _swap_halves(a):
    half = a.shape[-1] // 2
    return jnp.concatenate([a[..., half:], a[..., :half]], axis=-1)


def _pad_lanes(a):
    pad = [(0, 0)] * (a.ndim - 1) + [(0, LANES - a.shape[-1])]
    return jnp.pad(a, pad)


def kernel(x_prompt, x_sample, cache_latent, cache_kpe, state_conv, page_table, norm_ffn1, w_ffn1_gu, w_ffn1_down, norm_mix, norm_ffn2, w_ffn2_gu, w_ffn2_down, w_conv_in, conv_w, w_conv_out, w_dq, q_norm, w_uq, q_nope_norm, q_pe_norm, w_o, norm_kv_in, w_dkv, kv_norm, k_pe_norm, w_uk, w_uv, k_nope_norm):
    batch, seq, _ = x_prompt.shape
    n_dec, dec_seq, _ = x_sample.shape
    n_pages = page_table.shape[1]
    assert seq % TM == 0 and (n_dec * dec_seq) % TM == 0 and dec_seq == 4

    row = lambda a: a.reshape(1, -1)
    ffn1 = (w_ffn1_gu.astype(BF16), w_ffn1_down.astype(BF16))
    ffn2 = (w_ffn2_gu.astype(BF16), w_ffn2_down.astype(BF16))
    win = w_conv_in[0].astype(BF16)
    wout = w_conv_out[0].astype(BF16)
    w_pe = w_dkv[:, KV_RANK:]
    wdkv = jnp.concatenate([w_dkv[:, :KV_RANK], _pad_lanes(w_pe),
                            _pad_lanes(_swap_halves(w_pe))], axis=1).astype(BF16)
    wuk = w_uk.astype(BF16)
    wuvt = w_uv.T.astype(BF16)
    wukt = w_uk.T.astype(BF16)
    wuv = w_uv.astype(BF16)
    wdq = w_dq[0].astype(BF16)
    wuq3 = w_uq[0].reshape(Q_RANK, N_HEADS, NOPE_DIM + ROPE_DIM)
    wq_pe = wuq3[:, :, NOPE_DIM:]
    wuq = jnp.concatenate([
        wuq3[:, :, :NOPE_DIM].reshape(Q_RANK, -1),
        _pad_lanes(wq_pe).reshape(Q_RANK, -1),
        _pad_lanes(_swap_halves(wq_pe)).reshape(Q_RANK, -1)], axis=1).astype(BF16)
    wo = w_o[0].astype(BF16)
    kga = row(_pad_lanes(k_pe_norm))
    kgb = row(_pad_lanes(_swap_halves(k_pe_norm)))
    qga = row(_pad_lanes(q_pe_norm[0]))
    qgb = row(_pad_lanes(_swap_halves(q_pe_norm[0])))

    cos_p, sin_p = _rope_tables(jnp.arange(seq))
    pos_s = n_pages * PAGE_SIZE + jnp.arange(n_dec * dec_seq) % dec_seq
    cos_s, sin_s = _rope_tables(pos_s)

    prev = state_conv[0]
    zrow = jnp.zeros((n_dec, 1, D_MODEL), F32)
    p1 = jnp.concatenate([prev[:, 1:2], zrow, zrow, zrow], axis=1).reshape(-1, D_MODEL)
    p2 = jnp.concatenate([prev[:, 0:1], prev[:, 1:2], zrow, zrow], axis=1).reshape(-1, D_MODEL)

    def dense(x, tiles_per_seq, cos_t, sin_t, prev_rows):
        x = _ffn_call(x, row(norm_ffn1[0]), *ffn1, 0)
        x, u = _mixer_call(x, row(norm_mix[0]), win, conv_w[0], wout, tiles_per_seq, prev_rows)
        x, lat, kpe, k, vt = _ffn_kv_call(
            x, row(norm_ffn2[0]), *ffn2, 0, row(norm_kv_in), wdkv, row(kv_norm),
            kga, kgb, cos_t, sin_t, wuk, wuvt, row(k_nope_norm))
        x, q = _ffn_q_call(
            x, row(norm_ffn1[1]), *ffn1, 1, row(norm_mix[1]), wdq, row(q_norm[0]), wuq,
            row(q_nope_norm[0]), qga, qgb, cos_t, sin_t)
        return x, u, lat, kpe, k, vt, q

    xp, up, lat_p, kpe_p, k_p, vt_p, q_p = dense(
        x_prompt.reshape(batch * seq, D_MODEL), seq // TM, cos_p, sin_p, None)
    o_p = _prompt_attn_call(q_p, k_p, vt_p, batch, seq)
    y_p = _out_ffn_call(xp, o_p, wo, row(norm_ffn2[1]), *ffn2, 1)
    conv_p = up.reshape(batch, seq // TM, SUBLANES, D_MODEL)[:, -1, SUBLANES - 2:][None]

    xs, us, lat_s, kpe_s, k_s, _, q_s = dense(
        x_sample.reshape(n_dec * dec_seq, D_MODEL), 1, cos_s, sin_s, (p1, p2))
    qa = _absorb_call(q_s, row(k_nope_norm), wukt)
    qa = qa.reshape(n_dec, dec_seq * N_HEADS, KV_RANK + LANES)
    pad_rows = ((0, 0), (0, SUBLANES - dec_seq), (0, 0))
    lat_new = jnp.pad(lat_s.reshape(n_dec, dec_seq, KV_RANK), pad_rows)
    kpe_new = jnp.pad(kpe_s.reshape(n_dec, dec_seq, ROPE_DIM), pad_rows)
    qn_s = q_s.reshape(n_dec, dec_seq * N_HEADS, 2 * LANES)[:, :, :NOPE_DIM]
    kn_new = k_s.reshape(n_dec, dec_seq * N_HEADS, 2 * LANES)[:, :, :NOPE_DIM]
    ctx = _sample_attn_call(page_table, qa, qn_s, kn_new, lat_new, kpe_new, wukt,
                            cache_latent, cache_kpe)
    o_s = _ctx_out_call(ctx.reshape(n_dec * dec_seq, N_HEADS * KV_RANK), wuv)
    y_s = _out_ffn_call(xs, o_s, wo, row(norm_ffn2[1]), *ffn2, 1)
    conv_s = us.reshape(n_dec, dec_seq, D_MODEL)[:, dec_seq - (CONV_W - 1):][None]

    return (y_p.reshape(batch, seq, D_MODEL),
            y_s.reshape(n_dec, dec_seq, D_MODEL),
            conv_p, conv_s,
            lat_p.reshape(batch, seq, KV_RANK),
            kpe_p.reshape(batch, seq, ROPE_DIM),
            lat_s.reshape(n_dec, dec_seq, KV_RANK),
            kpe_s.reshape(n_dec, dec_seq, ROPE_DIM))
```

```python
import functools

import jax
import jax.numpy as jnp
from jax import lax
from jax.experimental import pallas as pl
from jax.experimental.pallas import tpu as pltpu

F32 = jnp.float32
BF16 = jnp.bfloat16

D_MODEL = 1024
D_FF = 2816
N_HEADS = 8
NOPE_DIM = 128
ROPE_DIM = 64
V_DIM = 128
Q_RANK = 512
KV_RANK = 256
CONV_W = 3
PAGE_SIZE = 128
ROPE_THETA = 10000.0
EPS = 1e-6
SCALE = (NOPE_DIM + ROPE_DIM) ** -0.5
LOG2E = 1.4426950408889634
Q_SCALE = SCALE * LOG2E
V_ROWS = V_DIM + 16
NEG = -1e30

LANES = 128
SUBLANES = 8
TM = 512
FF_CHUNKS = 1
FC = D_FF // FF_CHUNKS
TQ = 512
HEADS_PER_STEP = 4
KEY_CHUNK = 1024
VMEM_LIMIT = 56 * 1024 * 1024

_NT = (((1,), (1,)), ((), ()))


def _const_spec(shape):
    zeros = (0,) * len(shape)
    return pl.BlockSpec(shape, lambda *_: zeros, pipeline_mode=pl.Buffered(1))


def _rms(x, g):
    ms = jnp.mean(x * x, axis=-1, keepdims=True)
    return x * lax.rsqrt(ms + EPS) * g


def _dot(a, b):
    return jnp.dot(a, b, preferred_element_type=F32)


def _ffn(x, g, wgu_ref, wd_ref):
    xn = _rms(x, g).astype(BF16)
    acc = None
    for c in range(FF_CHUNKS):
        gate = _dot(xn, wgu_ref[:, c * FC:(c + 1) * FC])
        up = _dot(xn, wgu_ref[:, D_FF + c * FC:D_FF + (c + 1) * FC])
        act = (gate * jax.nn.sigmoid(gate) * up).astype(BF16)
        part = _dot(act, wd_ref[c * FC:(c + 1) * FC, :])
        acc = part if acc is None else acc + part
    return x + 0.5 * acc


def _rope_norm(a, b, ga, gb, cos_t, sin_t):
    ms = jnp.sum(a * a, axis=-1, keepdims=True) * (1.0 / ROPE_DIM)
    return (a * ga * cos_t + b * gb * sin_t) * lax.rsqrt(ms + EPS)


def _ffn_kernel(x_ref, g_ref, wgu_ref, wd_ref, o_ref):
    o_ref[...] = _ffn(x_ref[...], g_ref[...], wgu_ref, wd_ref)


def _mixer_kernel(tiles_per_seq, is_sample, *refs):
    if is_sample:
        (x_ref, g_ref, win_ref, cw_ref, wout_ref, p1_ref, p2_ref,
         o_ref, u_ref, ubuf) = refs
    else:
        x_ref, g_ref, win_ref, cw_ref, wout_ref, o_ref, u_ref, ubuf = refs
    i = pl.program_id(0)
    x = x_ref[...]
    xn = _rms(x, g_ref[...]).astype(BF16)
    bch = _dot(xn, win_ref[...])
    b = bch[:, :D_MODEL]
    u = bch[:, D_MODEL:2 * D_MODEL] * bch[:, 2 * D_MODEL:]

    @pl.when(i % tiles_per_seq == 0)
    def _():
        ubuf[0:SUBLANES, :] = jnp.zeros((SUBLANES, D_MODEL), F32)

    ubuf[SUBLANES:SUBLANES + TM, :] = u
    s1 = ubuf[SUBLANES - 1:SUBLANES - 1 + TM, :]
    s2 = ubuf[SUBLANES - 2:SUBLANES - 2 + TM, :]
    if is_sample:
        t = lax.broadcasted_iota(jnp.int32, (TM, 1), 0) % 4
        s1 = jnp.where(t >= 1, s1, p1_ref[...])
        s2 = jnp.where(t >= 2, s2, p2_ref[...])
    cw = cw_ref[...]
    y = cw[0:1, :] * s2 + cw[1:2, :] * s1 + cw[2:3, :] * u
    o_ref[...] = x + _dot((b * y).astype(BF16), wout_ref[...])
    if is_sample:
        u_ref[...] = u
    else:
        ubuf[0:SUBLANES, :] = u[TM - SUBLANES:, :]
        u_ref[0] = u[TM - SUBLANES:, :]


def _ffn_kv_kernel(x_ref, g_ref, wgu_ref, wd_ref, gkv_ref, wdkv_ref, kvn_ref,
                   ga_ref, gb_ref, cos_ref, sin_ref, wuk_ref, wuvt_ref, gkn_ref,
                   o_ref, lat_ref, kpe_ref, k_ref, vt_ref):
    x = _ffn(x_ref[...], g_ref[...], wgu_ref, wd_ref)
    o_ref[...] = x
    hk = _rms(x, gkv_ref[...]).astype(BF16)
    ckv = _dot(hk, wdkv_ref[...])
    lat = _rms(ckv[:, :KV_RANK], kvn_ref[...])
    lat_ref[...] = lat
    kpe = _rope_norm(ckv[:, KV_RANK:KV_RANK + LANES], ckv[:, KV_RANK + LANES:],
                     ga_ref[...], gb_ref[...], cos_ref[...], sin_ref[...])
    kpe_ref[...] = kpe[:, :ROPE_DIM]
    kpeb = kpe.astype(BF16)
    latb = lat.astype(BF16)
    ek = _dot(latb, wuk_ref[...])
    gkn = gkn_ref[...]
    for h in range(N_HEADS):
        e = ek[:, h * NOPE_DIM:(h + 1) * NOPE_DIM]
        k_ref[:, 2 * h * LANES:(2 * h + 1) * LANES] = _rms(e, gkn).astype(BF16)
        k_ref[:, (2 * h + 1) * LANES:(2 * h + 2) * LANES] = kpeb
    vt = lax.dot_general(wuvt_ref[...], latb, _NT, preferred_element_type=F32).astype(BF16)
    ones = jnp.ones((V_ROWS - V_DIM, vt.shape[1]), BF16)
    for h in range(N_HEADS):
        vt_ref[h * V_ROWS:h * V_ROWS + V_DIM, :] = vt[h * V_DIM:(h + 1) * V_DIM, :]
        vt_ref[h * V_ROWS + V_DIM:(h + 1) * V_ROWS, :] = ones


def _ffn_q_kernel(x_ref, g_ref, wgu_ref, wd_ref, gmix_ref, wdq_ref, qn_ref,
                  wuq_ref, gqn_ref, ga_ref, gb_ref, cos_ref, sin_ref,
                  o_ref, q_ref):
    x = _ffn(x_ref[...], g_ref[...], wgu_ref, wd_ref)
    o_ref[...] = x
    xn = _rms(x, gmix_ref[...]).astype(BF16)
    cq = _rms(_dot(xn, wdq_ref[...]), qn_ref[...]).astype(BF16)
    q = _dot(cq, wuq_ref[...])
    gqn = gqn_ref[...]
    ga = ga_ref[...]
    gb = gb_ref[...]
    cos_t = cos_ref[...]
    sin_t = sin_ref[...]
    hw = N_HEADS * LANES
    for h in range(N_HEADS):
        qn = _rms(q[:, h * LANES:(h + 1) * LANES], gqn) * Q_SCALE
        qp = _rope_norm(q[:, hw + h * LANES:hw + (h + 1) * LANES],
                        q[:, 2 * hw + h * LANES:2 * hw + (h + 1) * LANES],
                        ga, gb, cos_t, sin_t) * Q_SCALE
        q_ref[:, 2 * h * LANES:(2 * h + 1) * LANES] = qn.astype(BF16)
        q_ref[:, (2 * h + 1) * LANES:(2 * h + 2) * LANES] = qp.astype(BF16)


def _out_ffn_kernel(x_ref, a_ref, wo_ref, g_ref, wgu_ref, wd_ref, o_ref):
    x = x_ref[...] + _dot(a_ref[...], wo_ref[...])
    o_ref[...] = _ffn(x, g_ref[...], wgu_ref, wd_ref)


def _prompt_attn_kernel(q_ref, k_ref, vt_ref, o_ref,
                        sa_ref, sb_ref, smaxa_ref, smaxb_ref, m_scr, acc_scr):
    qi = pl.program_id(2)

    def scores(slot, ki, j, masked=False):
        s_ref, smax_ref = slot
        off = pl.multiple_of(ki * TQ, TQ)
        s = lax.dot_general(k_ref[pl.ds(off, TQ), 2 * j * LANES:2 * (j + 1) * LANES],
                            q_ref[:, 2 * j * LANES:2 * (j + 1) * LANES], _NT,
                            preferred_element_type=F32)
        if masked:
            key = lax.broadcasted_iota(jnp.int32, (TQ, TQ), 0)
            qry = lax.broadcasted_iota(jnp.int32, (TQ, TQ), 1)
            s = jnp.where(qry >= key, s, NEG)
        s_ref[j] = s
        smax_ref[j] = jnp.max(s, axis=0, keepdims=True)

    def accumulate(slot, ki, j):
        s_ref, smax_ref = slot
        off = pl.multiple_of(ki * TQ, TQ)
        m = m_scr[j]
        vt = vt_ref[j * V_ROWS:(j + 1) * V_ROWS, pl.ds(off, TQ)]
        m_new = jnp.maximum(m, smax_ref[j])
        alpha = jnp.exp2(m - m_new)
        p = jnp.exp2(s_ref[j] - m_new).astype(BF16)
        acc_scr[j] = alpha * acc_scr[j] + _dot(vt, p)
        m_scr[j] = m_new

    def advance(slot_next, ki_next, slot_cur, ki_cur):
        for j in range(HEADS_PER_STEP):
            scores(slot_next, ki_next, j)
            accumulate(slot_cur, ki_cur, j)

    for j in range(HEADS_PER_STEP):
        m_scr[j] = jnp.full((1, TQ), NEG, F32)
        acc_scr[j] = jnp.zeros((V_ROWS, TQ), F32)

    slot_a = (sa_ref, smaxa_ref)
    slot_b = (sb_ref, smaxb_ref)
    n_pairs = qi // 2
    for j in range(HEADS_PER_STEP):
        scores(slot_a, qi, j, masked=True)

    def pair(t, _):
        advance(slot_b, 2 * t, slot_a, jnp.where(t == 0, qi, 2 * t - 1))
        advance(slot_a, 2 * t + 1, slot_b, 2 * t)
        return 0

    lax.fori_loop(0, n_pairs, pair, 0)
    in_a = jnp.where(n_pairs == 0, qi, 2 * n_pairs - 1)

    @pl.when(qi % 2 == 1)
    def _():
        advance(slot_b, qi - 1, slot_a, in_a)
        for j in range(HEADS_PER_STEP):
            accumulate(slot_b, qi - 1, j)

    @pl.when(qi % 2 == 0)
    def _():
        for j in range(HEADS_PER_STEP):
            accumulate(slot_a, in_a, j)

    for j in range(HEADS_PER_STEP):
        acc = acc_scr[j]
        o = acc[:V_DIM] / acc[V_DIM:V_DIM + 1]
        o_ref[:, j * V_DIM:(j + 1) * V_DIM] = o.T.astype(BF16)


def _sample_attn_kernel(n_pages, pt_ref, qa_ref, qn_ref, knn_ref, latn_ref, kpen_ref,
                        wukt_ref, lat_hbm, kpet_hbm, ctx_ref,
                        latbuf, kpebuf, wq, sems):
    n_seq = pl.num_programs(0)
    b = pl.program_id(0)
    slot = b % 2
    n_past = n_pages * PAGE_SIZE
    n_chunks = n_past // KEY_CHUNK
    pages_per_chunk = n_pages // n_chunks
    n_w = N_HEADS * NOPE_DIM

    def page_copies(seq, slot_, j):
        pg = pt_ref[seq * n_pages + j]
        row = pl.multiple_of(j * PAGE_SIZE, PAGE_SIZE)
        return (
            pltpu.make_async_copy(lat_hbm.at[pg], latbuf.at[slot_, pl.ds(row, PAGE_SIZE)],
                                  sems.at[0, slot_]),
            pltpu.make_async_copy(kpet_hbm.at[pg], kpebuf.at[slot_, :, pl.ds(row, PAGE_SIZE)],
                                  sems.at[1, slot_]),
        )

    def start_pages(seq, slot_, first, count):
        for j in range(count):
            for cp in page_copies(seq, slot_, first + j):
                cp.start()

    def wait_fetch(slot_):
        pltpu.make_async_copy(latbuf.at[1 - slot_], latbuf.at[slot_], sems.at[0, slot_]).wait()
        pltpu.make_async_copy(kpebuf.at[1 - slot_], kpebuf.at[slot_], sems.at[1, slot_]).wait()

    @pl.when(b == 0)
    def _():
        def first_fetch(c, _):
            start_pages(0, 0, c * pages_per_chunk, pages_per_chunk)
            return 0
        lax.fori_loop(0, n_chunks, first_fetch, 0)
        wq[0:n_w, :] = wukt_ref[...]

    wait_fetch(slot)

    qa = qa_ref[0]
    n_q = qa.shape[0]
    wq[n_w:n_w + n_q, :] = qa[:, :KV_RANK]
    q_pe = qa[:, KV_RANK:KV_RANK + ROPE_DIM]
    wq_all = wq[...]

    def scores(latc, pe):
        n = latc.shape[0]
        e = lax.dot_general(wq_all, latc, _NT, preferred_element_type=F32)
        e3 = e[:n_w].reshape(N_HEADS, NOPE_DIM, n)
        r = lax.rsqrt(jnp.sum(e3 * e3, axis=1) * (1.0 / NOPE_DIM) + EPS)
        raw3 = e[n_w:].reshape(n_q // N_HEADS, N_HEADS, n) * r[None]
        return raw3.reshape(n_q, n) + pe

    pad = jnp.zeros((LANES - SUBLANES, KV_RANK), F32)
    lat_new = jnp.concatenate([latn_ref[0], pad], axis=0).astype(BF16)
    kpec = jnp.concatenate([kpen_ref[0], pad[:, :ROPE_DIM]], axis=0).astype(BF16)
    s_new = lax.dot_general(q_pe, kpec, _NT, preferred_element_type=F32)
    cross = lax.dot_general(qn_ref[0], knn_ref[0], _NT, preferred_element_type=F32)
    n_new = knn_ref.shape[1] // N_HEADS
    row = lax.broadcasted_iota(jnp.int32, cross.shape, 0)
    col = lax.broadcasted_iota(jnp.int32, cross.shape, 1)
    same_head = row % N_HEADS == col % N_HEADS
    t_q = lax.broadcasted_iota(jnp.int32, (n_q, LANES), 0) // N_HEADS
    j_k = lax.broadcasted_iota(jnp.int32, (n_q, LANES), 1)
    for j in range(n_new):
        pick = jnp.where(same_head & (col // N_HEADS == j), cross, 0.0)
        s_new = s_new + jnp.where(j_k == j, jnp.sum(pick, axis=-1, keepdims=True), 0.0)
    s_new = jnp.where(j_k <= t_q, s_new, NEG)

    m = jnp.max(s_new, axis=-1, keepdims=True)
    p = jnp.exp2(s_new - m)
    l = jnp.sum(p, axis=-1, keepdims=True)
    acc = _dot(p.astype(BF16), lat_new)

    pending = None
    for c in range(n_chunks):
        start_pages((b + 1) % n_seq, 1 - slot, c * pages_per_chunk, pages_per_chunk)
        keys = slice(c * KEY_CHUNK, (c + 1) * KEY_CHUNK)
        latc = latbuf[slot, keys, :].astype(BF16)
        kpec = kpebuf[slot, :, keys].astype(BF16)
        s = scores(latc, _dot(q_pe, kpec))
        if pending is not None:
            alpha, p_prev, lat_prev = pending
            acc = alpha * acc + _dot(p_prev, lat_prev)
        m_new = jnp.maximum(m, jnp.max(s, axis=-1, keepdims=True))
        alpha = jnp.exp2(m - m_new)
        p = jnp.exp2(s - m_new)
        l = alpha * l + jnp.sum(p, axis=-1, keepdims=True)
        m = m_new
        pending = (alpha, p.astype(BF16), latc)
    alpha, p_prev, lat_prev = pending
    acc = alpha * acc + _dot(p_prev, lat_prev)
    ctx_ref[0] = acc / l

    @pl.when(b == n_seq - 1)
    def _():
        wait_fetch(1 - slot)


def _absorb_kernel(q_ref, gkn_ref, wukt_ref, o_ref):
    gkn = gkn_ref[...]
    w = KV_RANK + LANES
    for h in range(N_HEADS):
        qn = (q_ref[:, 2 * h * LANES:(2 * h + 1) * LANES].astype(F32) * gkn).astype(BF16)
        o_ref[:, h * w:h * w + KV_RANK] = _dot(
            qn, wukt_ref[h * NOPE_DIM:(h + 1) * NOPE_DIM, :]).astype(BF16)
        o_ref[:, h * w + KV_RANK:(h + 1) * w] = q_ref[:, (2 * h + 1) * LANES:(2 * h + 2) * LANES]


def _ctx_out_kernel(c_ref, wuv_ref, o_ref):
    for h in range(N_HEADS):
        c = c_ref[:, h * KV_RANK:(h + 1) * KV_RANK].astype(BF16)
        o_ref[:, h * V_DIM:(h + 1) * V_DIM] = _dot(
            c, wuv_ref[:, h * V_DIM:(h + 1) * V_DIM]).astype(BF16)


def _params(n_axes=1):
    return pltpu.CompilerParams(dimension_semantics=("arbitrary",) * n_axes,
                                vmem_limit_bytes=VMEM_LIMIT)


def _row_spec(width):
    return pl.BlockSpec((TM, width), lambda i: (i, 0))


def _table_spec(n_rows):
    tiles = n_rows // TM
    return pl.BlockSpec((TM, LANES), lambda i: (i % tiles, 0))


def _ffn_specs(layer):
    pick = lambda *_: (layer, 0, 0)
    return [_const_spec((1, D_MODEL)),
            pl.BlockSpec((None, D_MODEL, 2 * D_FF), pick, pipeline_mode=pl.Buffered(1)),
            pl.BlockSpec((None, D_FF, D_MODEL), pick, pipeline_mode=pl.Buffered(1))]


def _ffn_call(x, g, wgu, wd, layer):
    t = x.shape[0]
    return pl.pallas_call(
        _ffn_kernel, grid=(t // TM,),
        in_specs=[_row_spec(D_MODEL)] + _ffn_specs(layer),
        out_specs=_row_spec(D_MODEL),
        out_shape=jax.ShapeDtypeStruct((t, D_MODEL), F32),
        compiler_params=_params(), name="ffn")(x, g, wgu, wd)


def _mixer_call(x, g, win, cw, wout, tiles_per_seq, prev=None):
    t = x.shape[0]
    n_tiles = t // TM
    is_sample = prev is not None
    in_specs = [_row_spec(D_MODEL), _const_spec((1, D_MODEL)),
                _const_spec((D_MODEL, 3 * D_MODEL)), _const_spec((CONV_W, D_MODEL)),
                _const_spec((D_MODEL, D_MODEL))]
    args = [x, g, win, cw, wout]
    if is_sample:
        in_specs += [_row_spec(D_MODEL), _row_spec(D_MODEL)]
        args += list(prev)
        u_spec = _row_spec(D_MODEL)
        u_shape = jax.ShapeDtypeStruct((t, D_MODEL), F32)
    else:
        u_spec = pl.BlockSpec((1, SUBLANES, D_MODEL), lambda i: (i, 0, 0))
        u_shape = jax.ShapeDtypeStruct((n_tiles, SUBLANES, D_MODEL), F32)
    return pl.pallas_call(
        functools.partial(_mixer_kernel, tiles_per_seq, is_sample), grid=(n_tiles,),
        in_specs=in_specs,
        out_specs=[_row_spec(D_MODEL), u_spec],
        out_shape=[jax.ShapeDtypeStruct((t, D_MODEL), F32), u_shape],
        scratch_shapes=[pltpu.VMEM((TM + SUBLANES, D_MODEL), F32)],
        compiler_params=_params(), name="mixer")(*args)


def _ffn_kv_call(x, g, wgu, wd, layer, gkv, wdkv, kvn, ga, gb, cos_t, sin_t, wuk, wuvt, gkn):
    t = x.shape[0]
    in_specs = ([_row_spec(D_MODEL)] + _ffn_specs(layer) + [
        _const_spec((1, D_MODEL)), _const_spec((D_MODEL, KV_RANK + 2 * LANES)),
        _const_spec((1, KV_RANK)), _const_spec((1, LANES)), _const_spec((1, LANES)),
        _table_spec(cos_t.shape[0]), _table_spec(sin_t.shape[0]),
        _const_spec((KV_RANK, D_MODEL)), _const_spec((D_MODEL, KV_RANK)),
        _const_spec((1, NOPE_DIM))])
    out_specs = [_row_spec(D_MODEL), _row_spec(KV_RANK), _row_spec(ROPE_DIM),
                 _row_spec(2 * N_HEADS * LANES),
                 pl.BlockSpec((N_HEADS * V_ROWS, TM), lambda i: (0, i))]
    out_shape = [jax.ShapeDtypeStruct((t, D_MODEL), F32),
                 jax.ShapeDtypeStruct((t, KV_RANK), F32),
                 jax.ShapeDtypeStruct((t, ROPE_DIM), F32),
                 jax.ShapeDtypeStruct((t, 2 * N_HEADS * LANES), BF16),
                 jax.ShapeDtypeStruct((N_HEADS * V_ROWS, t), BF16)]
    return pl.pallas_call(
        _ffn_kv_kernel, grid=(t // TM,), in_specs=in_specs, out_specs=out_specs,
        out_shape=out_shape, compiler_params=_params(), name="ffn_kv")(
            x, g, wgu, wd, gkv, wdkv, kvn, ga, gb, cos_t, sin_t, wuk, wuvt, gkn)


def _ffn_q_call(x, g, wgu, wd, layer, gmix, wdq, qnorm, wuq, gqn, ga, gb, cos_t, sin_t):
    t = x.shape[0]
    in_specs = ([_row_spec(D_MODEL)] + _ffn_specs(layer) + [
        _const_spec((1, D_MODEL)), _const_spec((D_MODEL, Q_RANK)),
        _const_spec((1, Q_RANK)), _const_spec((Q_RANK, 3 * N_HEADS * LANES)),
        _const_spec((1, NOPE_DIM)), _const_spec((1, LANES)), _const_spec((1, LANES)),
        _table_spec(cos_t.shape[0]), _table_spec(sin_t.shape[0])])
    return pl.pallas_call(
        _ffn_q_kernel, grid=(t // TM,), in_specs=in_specs,
        out_specs=[_row_spec(D_MODEL), _row_spec(2 * N_HEADS * LANES)],
        out_shape=[jax.ShapeDtypeStruct((t, D_MODEL), F32),
                   jax.ShapeDtypeStruct((t, 2 * N_HEADS * LANES), BF16)],
        compiler_params=_params(), name="ffn_q")(
            x, g, wgu, wd, gmix, wdq, qnorm, wuq, gqn, ga, gb, cos_t, sin_t)


def _out_ffn_call(x, a, wo, g, wgu, wd, layer):
    t = x.shape[0]
    return pl.pallas_call(
        _out_ffn_kernel, grid=(t // TM,),
        in_specs=[_row_spec(D_MODEL), _row_spec(D_MODEL),
                  _const_spec((D_MODEL, D_MODEL))] + _ffn_specs(layer),
        out_specs=_row_spec(D_MODEL),
        out_shape=jax.ShapeDtypeStruct((t, D_MODEL), F32),
        compiler_params=_params(), name="out_ffn")(x, a, wo, g, wgu, wd)


def _prompt_attn_call(q, k, vt, batch, seq):
    nq = seq // TQ
    hps = HEADS_PER_STEP
    resident = pl.Buffered(1)
    return pl.pallas_call(
        _prompt_attn_kernel, grid=(batch, N_HEADS // hps, nq),
        in_specs=[pl.BlockSpec((TQ, hps * 2 * LANES), lambda b, h, i: (b * nq + i, h)),
                  pl.BlockSpec((seq, hps * 2 * LANES), lambda b, h, i: (b, h),
                               pipeline_mode=resident),
                  pl.BlockSpec((hps * V_ROWS, seq), lambda b, h, i: (h, b),
                               pipeline_mode=resident)],
        out_specs=pl.BlockSpec((TQ, hps * V_DIM), lambda b, h, i: (b * nq + i, h)),
        out_shape=jax.ShapeDtypeStruct((batch * seq, N_HEADS * V_DIM), BF16),
        scratch_shapes=[pltpu.VMEM((hps, TQ, TQ), F32), pltpu.VMEM((hps, TQ, TQ), F32),
                        pltpu.VMEM((hps, 1, TQ), F32), pltpu.VMEM((hps, 1, TQ), F32),
                        pltpu.VMEM((hps, 1, TQ), F32), pltpu.VMEM((hps, V_ROWS, TQ), F32)],
        compiler_params=_params(3), name="prompt_attn")(q, k, vt)


def _sample_attn_call(page_table, qa, qn, kn_new, lat_new, kpe_new, wukt,
                      cache_latent, cache_kpe):
    n_seq, n_pages = page_table.shape
    n_q = qa.shape[1]
    n_keys = n_pages * PAGE_SIZE
    grid_spec = pltpu.PrefetchScalarGridSpec(
        num_scalar_prefetch=1, grid=(n_seq,),
        in_specs=[pl.BlockSpec((1, n_q, KV_RANK + LANES), lambda b, pt: (b, 0, 0)),
                  pl.BlockSpec((1, n_q, NOPE_DIM), lambda b, pt: (b, 0, 0)),
                  pl.BlockSpec((1, kn_new.shape[1], NOPE_DIM), lambda b, pt: (b, 0, 0)),
                  pl.BlockSpec((1, SUBLANES, KV_RANK), lambda b, pt: (b, 0, 0)),
                  pl.BlockSpec((1, SUBLANES, ROPE_DIM), lambda b, pt: (b, 0, 0)),
                  pl.BlockSpec((N_HEADS * NOPE_DIM, KV_RANK), lambda b, pt: (0, 0),
                               pipeline_mode=pl.Buffered(1)),
                  pl.BlockSpec(memory_space=pl.ANY),
                  pl.BlockSpec(memory_space=pl.ANY)],
        out_specs=pl.BlockSpec((1, n_q, KV_RANK), lambda b, pt: (b, 0, 0)),
        scratch_shapes=[pltpu.VMEM((2, n_keys, KV_RANK), F32),
                        pltpu.VMEM((2, ROPE_DIM, n_keys), F32),
                        pltpu.VMEM((N_HEADS * NOPE_DIM + n_q, KV_RANK), BF16),
                        pltpu.SemaphoreType.DMA((2, 2))])
    return pl.pallas_call(
        functools.partial(_sample_attn_kernel, n_pages), grid_spec=grid_spec,
        out_shape=jax.ShapeDtypeStruct((n_seq, n_q, KV_RANK), F32),
        compiler_params=_params(), name="sample_attn")(
            page_table.reshape(-1), qa, qn, kn_new, lat_new, kpe_new, wukt, cache_latent,
            jnp.swapaxes(cache_kpe, 1, 2))


def _absorb_call(q, gkn, wukt):
    t = q.shape[0]
    return pl.pallas_call(
        _absorb_kernel, grid=(t // TM,),
        in_specs=[_row_spec(2 * N_HEADS * LANES), _const_spec((1, NOPE_DIM)),
                  _const_spec((N_HEADS * NOPE_DIM, KV_RANK))],
        out_specs=_row_spec(N_HEADS * (KV_RANK + LANES)),
        out_shape=jax.ShapeDtypeStruct((t, N_HEADS * (KV_RANK + LANES)), BF16),
        compiler_params=_params(), name="absorb")(q, gkn, wukt)


def _ctx_out_call(ctx, wuv):
    t = ctx.shape[0]
    return pl.pallas_call(
        _ctx_out_kernel, grid=(t // TM,),
        in_specs=[_row_spec(N_HEADS * KV_RANK),
                  _const_spec((KV_RANK, N_HEADS * V_DIM))],
        out_specs=_row_spec(N_HEADS * V_DIM),
        out_shape=jax.ShapeDtypeStruct((t, N_HEADS * V_DIM), BF16),
        compiler_params=_params(), name="ctx_out")(ctx, wuv)


def _rope_tables(pos):
    half = ROPE_DIM // 2
    freqs = ROPE_THETA ** (-jnp.arange(half, dtype=F32) / half)
    ang = pos.astype(F32)[:, None] * freqs
    cos = jnp.cos(ang)
    sin = jnp.sin(ang)
    zero = jnp.zeros((pos.shape[0], LANES - ROPE_DIM), F32)
    return (jnp.concatenate([cos, cos, zero], axis=1),
            jnp.concatenate([-sin, sin, zero], axis=1))


def _swap_halves(a):
    half = a.shape[-1] // 2
    return jnp.concatenate([a[..., half:], a[..., :half]], axis=-1)


def _pad_lanes(a):
    pad = [(0, 0)] * (a.ndim - 1) + [(0, LANES - a.shape[-1])]
    return jnp.pad(a, pad)


def kernel(x_prompt, x_sample, cache_latent, cache_kpe, state_conv, page_table, norm_ffn1, w_ffn1_gu, w_ffn1_down, norm_mix, norm_ffn2, w_ffn2_gu, w_ffn2_down, w_conv_in, conv_w, w_conv_out, w_dq, q_norm, w_uq, q_nope_norm, q_pe_norm, w_o, norm_kv_in, w_dkv, kv_norm, k_pe_norm, w_uk, w_uv, k_nope_norm):
    batch, seq, _ = x_prompt.shape
    n_dec, dec_seq, _ = x_sample.shape
    n_pages = page_table.shape[1]
    assert seq % TM == 0 and (n_dec * dec_seq) % TM == 0 and dec_seq == 4

    row = lambda a: a.reshape(1, -1)
    ffn1 = (w_ffn1_gu.astype(BF16), w_ffn1_down.astype(BF16))
    ffn2 = (w_ffn2_gu.astype(BF16), w_ffn2_down.astype(BF16))
    win = w_conv_in[0].astype(BF16)
    wout = w_conv_out[0].astype(BF16)
    w_pe = w_dkv[:, KV_RANK:]
    wdkv = jnp.concatenate([w_dkv[:, :KV_RANK], _pad_lanes(w_pe),
                            _pad_lanes(_swap_halves(w_pe))], axis=1).astype(BF16)
    wuk = w_uk.astype(BF16)
    wuvt = w_uv.T.astype(BF16)
    wukt = w_uk.T.astype(BF16)
    wuv = w_uv.astype(BF16)
    wdq = w_dq[0].astype(BF16)
    wuq3 = w_uq[0].reshape(Q_RANK, N_HEADS, NOPE_DIM + ROPE_DIM)
    wq_pe = wuq3[:, :, NOPE_DIM:]
    wuq = jnp.concatenate([
        wuq3[:, :, :NOPE_DIM].reshape(Q_RANK, -1),
        _pad_lanes(wq_pe).reshape(Q_RANK, -1),
        _pad_lanes(_swap_halves(wq_pe)).reshape(Q_RANK, -1)], axis=1).astype(BF16)
    wo = w_o[0].astype(BF16)
    kga = row(_pad_lanes(k_pe_norm))
    kgb = row(_pad_lanes(_swap_halves(k_pe_norm)))
    qga = row(_pad_lanes(q_pe_norm[0]))
    qgb = row(_pad_lanes(_swap_halves(q_pe_norm[0])))

    cos_p, sin_p = _rope_tables(jnp.arange(seq))
    pos_s = n_pages * PAGE_SIZE + jnp.arange(n_dec * dec_seq) % dec_seq
    cos_s, sin_s = _rope_tables(pos_s)

    prev = state_conv[0]
    zrow = jnp.zeros((n_dec, 1, D_MODEL), F32)
    p1 = jnp.concatenate([prev[:, 1:2], zrow, zrow, zrow], axis=1).reshape(-1, D_MODEL)
    p2 = jnp.concatenate([prev[:, 0:1], prev[:, 1:2], zrow, zrow], axis=1).reshape(-1, D_MODEL)

    def dense(x, tiles_per_seq, cos_t, sin_t, prev_rows):
        x = _ffn_call(x, row(norm_ffn1[0]), *ffn1, 0)
        x, u = _mixer_call(x, row(norm_mix[0]), win, conv_w[0], wout, tiles_per_seq, prev_rows)
        x, lat, kpe, k, vt = _ffn_kv_call(
            x, row(norm_ffn2[0]), *ffn2, 0, row(norm_kv_in), wdkv, row(kv_norm),
            kga, kgb, cos_t, sin_t, wuk, wuvt, row(k_nope_norm))
        x, q = _ffn_q_call(
            x, row(norm_ffn1[1]), *ffn1, 1, row(norm_mix[1]), wdq, row(q_norm[0]), wuq,
            row(q_nope_norm[0]), qga, qgb, cos_t, sin_t)
        return x, u, lat, kpe, k, vt, q

    xp, up, lat_p, kpe_p, k_p, vt_p, q_p = dense(
        x_prompt.reshape(batch * seq, D_MODEL), seq // TM, cos_p, sin_p, None)
    o_p = _prompt_attn_call(q_p, k_p, vt_p, batch, seq)
    y_p = _out_ffn_call(xp, o_p, wo, row(norm_ffn2[1]), *ffn2, 1)
    conv_p = up.reshape(batch, seq // TM, SUBLANES, D_MODEL)[:, -1, SUBLANES - 2:][None]

    xs, us, lat_s, kpe_s, k_s, _, q_s = dense(
        x_sample.reshape(n_dec * dec_seq, D_MODEL), 1, cos_s, sin_s, (p1, p2))
    qa = _absorb_call(q_s, row(k_nope_norm), wukt)
    qa = qa.reshape(n_dec, dec_seq * N_HEADS, KV_RANK + LANES)
    pad_rows = ((0, 0), (0, SUBLANES - dec_seq), (0, 0))
    lat_new = jnp.pad(lat_s.reshape(n_dec, dec_seq, KV_RANK), pad_rows)
    kpe_new = jnp.pad(kpe_s.reshape(n_dec, dec_seq, ROPE_DIM), pad_rows)
    qn_s = q_s.reshape(n_dec, dec_seq * N_HEADS, 2 * LANES)[:, :, :NOPE_DIM]
    kn_new = k_s.reshape(n_dec, dec_seq * N_HEADS, 2 * LANES)[:, :, :NOPE_DIM]
    ctx = _sample_attn_call(page_table, qa, qn_s, kn_new, lat_new, kpe_new, wukt,
                            cache_latent, cache_kpe)
    o_s = _ctx_out_call(ctx.reshape(n_dec * dec_seq, N_HEADS * KV_RANK), wuv)
    y_s = _out_ffn_call(xs, o_s, wo, row(norm_ffn2[1]), *ffn2, 1)
    conv_s = us.reshape(n_dec, dec_seq, D_MODEL)[:, dec_seq - (CONV_W - 1):][None]

    return (y_p.reshape(batch, seq, D_MODEL),
            y_s.reshape(n_dec, dec_seq, D_MODEL),
            conv_p, conv_s,
            lat_p.reshape(batch, seq, KV_RANK),
            kpe_p.reshape(batch, seq, ROPE_DIM),
            lat_s.reshape(n_dec, dec_seq, KV_RANK),
            kpe_s.reshape(n_dec, dec_seq, ROPE_DIM))
```

```python
import functools

import jax
import jax.numpy as jnp
from jax import lax
from jax.experimental import pallas as pl
from jax.experimental.pallas import tpu as pltpu

F32 = jnp.float32
BF16 = jnp.bfloat16

D_MODEL = 1024
D_FF = 2816
N_HEADS = 8
NOPE_DIM = 128
ROPE_DIM = 64
V_DIM = 128
Q_RANK = 512
KV_RANK = 256
CONV_W = 3
PAGE_SIZE = 128
ROPE_THETA = 10000.0
EPS = 1e-6
SCALE = (NOPE_DIM + ROPE_DIM) ** -0.5
LOG2E = 1.4426950408889634
Q_SCALE = SCALE * LOG2E
V_ROWS = V_DIM + 16
NEG = -1e30

LANES = 128
SUBLANES = 8
TM = 512
FF_CHUNKS = 1
FC = D_FF // FF_CHUNKS
TQ = 512
HEADS_PER_STEP = 4
KEY_CHUNK = 1024
VMEM_LIMIT = 56 * 1024 * 1024

_NT = (((1,), (1,)), ((), ()))


def _const_spec(shape):
    zeros = (0,) * len(shape)
    return pl.BlockSpec(shape, lambda *_: zeros, pipeline_mode=pl.Buffered(1))


def _rms(x, g):
    ms = jnp.mean(x * x, axis=-1, keepdims=True)
    return x * lax.rsqrt(ms + EPS) * g


def _dot(a, b):
    return jnp.dot(a, b, preferred_element_type=F32)


def _ffn(x, g, wgu_ref, wd_ref):
    xn = _rms(x, g).astype(BF16)
    acc = None
    for c in range(FF_CHUNKS):
        gate = _dot(xn, wgu_ref[:, c * FC:(c + 1) * FC])
        up = _dot(xn, wgu_ref[:, D_FF + c * FC:D_FF + (c + 1) * FC])
        act = (gate * jax.nn.sigmoid(gate) * up).astype(BF16)
        part = _dot(act, wd_ref[c * FC:(c + 1) * FC, :])
        acc = part if acc is None else acc + part
    return x + 0.5 * acc


def _rope_norm(a, b, ga, gb, cos_t, sin_t):
    ms = jnp.sum(a * a, axis=-1, keepdims=True) * (1.0 / ROPE_DIM)
    return (a * ga * cos_t + b * gb * sin_t) * lax.rsqrt(ms + EPS)


def _ffn_kernel(x_ref, g_ref, wgu_ref, wd_ref, o_ref):
    o_ref[...] = _ffn(x_ref[...], g_ref[...], wgu_ref, wd_ref)


def _mixer_kernel(tiles_per_seq, is_sample, *refs):
    if is_sample:
        (x_ref, g_ref, win_ref, cw_ref, wout_ref, p1_ref, p2_ref,
         o_ref, u_ref, ubuf) = refs
    else:
        x_ref, g_ref, win_ref, cw_ref, wout_ref, o_ref, u_ref, ubuf = refs
    i = pl.program_id(0)
    x = x_ref[...]
    xn = _rms(x, g_ref[...]).astype(BF16)
    bch = _dot(xn, win_ref[...])
    b = bch[:, :D_MODEL]
    u = bch[:, D_MODEL:2 * D_MODEL] * bch[:, 2 * D_MODEL:]

    @pl.when(i % tiles_per_seq == 0)
    def _():
        ubuf[0:SUBLANES, :] = jnp.zeros((SUBLANES, D_MODEL), F32)

    ubuf[SUBLANES:SUBLANES + TM, :] = u
    s1 = ubuf[SUBLANES - 1:SUBLANES - 1 + TM, :]
    s2 = ubuf[SUBLANES - 2:SUBLANES - 2 + TM, :]
    if is_sample:
        t = lax.broadcasted_iota(jnp.int32, (TM, 1), 0) % 4
        s1 = jnp.where(t >= 1, s1, p1_ref[...])
        s2 = jnp.where(t >= 2, s2, p2_ref[...])
    cw = cw_ref[...]
    y = cw[0:1, :] * s2 + cw[1:2, :] * s1 + cw[2:3, :] * u
    o_ref[...] = x + _dot((b * y).astype(BF16), wout_ref[...])
    if is_sample:
        u_ref[...] = u
    else:
        ubuf[0:SUBLANES, :] = u[TM - SUBLANES:, :]
        u_ref[0] = u[TM - SUBLANES:, :]


def _ffn_kv_kernel(x_ref, g_ref, wgu_ref, wd_ref, gkv_ref, wdkv_ref, kvn_ref,
                   ga_ref, gb_ref, cos_ref, sin_ref, wuk_ref, wuvt_ref, gkn_ref,
                   o_ref, lat_ref, kpe_ref, k_ref, vt_ref):
    x = _ffn(x_ref[...], g_ref[...], wgu_ref, wd_ref)
    o_ref[...] = x
    hk = _rms(x, gkv_ref[...]).astype(BF16)
    ckv = _dot(hk, wdkv_ref[...])
    lat = _rms(ckv[:, :KV_RANK], kvn_ref[...])
    lat_ref[...] = lat
    kpe = _rope_norm(ckv[:, KV_RANK:KV_RANK + LANES], ckv[:, KV_RANK + LANES:],
                     ga_ref[...], gb_ref[...], cos_ref[...], sin_ref[...])
    kpe_ref[...] = kpe[:, :ROPE_DIM]
    kpeb = kpe.astype(BF16)
    latb = lat.astype(BF16)
    ek = _dot(latb, wuk_ref[...])
    gkn = gkn_ref[...]
    for h in range(N_HEADS):
        e = ek[:, h * NOPE_DIM:(h + 1) * NOPE_DIM]
        k_ref[:, 2 * h * LANES:(2 * h + 1) * LANES] = _rms(e, gkn).astype(BF16)
        k_ref[:, (2 * h + 1) * LANES:(2 * h + 2) * LANES] = kpeb
    vt = lax.dot_general(wuvt_ref[...], latb, _NT, preferred_element_type=F32).astype(BF16)
    ones = jnp.ones((V_ROWS - V_DIM, vt.shape[1]), BF16)
    for h in range(N_HEADS):
        vt_ref[h * V_ROWS:h * V_ROWS + V_DIM, :] = vt[h * V_DIM:(h + 1) * V_DIM, :]
        vt_ref[h * V_ROWS + V_DIM:(h + 1) * V_ROWS, :] = ones


def _ffn_q_kernel(x_ref, g_ref, wgu_ref, wd_ref, gmix_ref, wdq_ref, qn_ref,
                  wuq_ref, gqn_ref, ga_ref, gb_ref, cos_ref, sin_ref,
                  o_ref, q_ref):
    x = _ffn(x_ref[...], g_ref[...], wgu_ref, wd_ref)
    o_ref[...] = x
    xn = _rms(x, gmix_ref[...]).astype(BF16)
    cq = _rms(_dot(xn, wdq_ref[...]), qn_ref[...]).astype(BF16)
    q = _dot(cq, wuq_ref[...])
    gqn = gqn_ref[...]
    ga = ga_ref[...]
    gb = gb_ref[...]
    cos_t = cos_ref[...]
    sin_t = sin_ref[...]
    hw = N_HEADS * LANES
    for h in range(N_HEADS):
        qn = _rms(q[:, h * LANES:(h + 1) * LANES], gqn) * Q_SCALE
        qp = _rope_norm(q[:, hw + h * LANES:hw + (h + 1) * LANES],
                        q[:, 2 * hw + h * LANES:2 * hw + (h + 1) * LANES],
                        ga, gb, cos_t, sin_t) * Q_SCALE
        q_ref[:, 2 * h * LANES:(2 * h + 1) * LANES] = qn.astype(BF16)
        q_ref[:, (2 * h + 1) * LANES:(2 * h + 2) * LANES] = qp.astype(BF16)


def _out_ffn_kernel(x_ref, a_ref, wo_ref, g_ref, wgu_ref, wd_ref, o_ref):
    x = x_ref[...] + _dot(a_ref[...], wo_ref[...])
    o_ref[...] = _ffn(x, g_ref[...], wgu_ref, wd_ref)


def _prompt_attn_kernel(q_ref, k_ref, vt_ref, o_ref,
                        sa_ref, sb_ref, smaxa_ref, smaxb_ref, m_scr, acc_scr):
    qi = pl.program_id(2)

    def scores(slot, ki, j, masked=False):
        s_ref, smax_ref = slot
        off = pl.multiple_of(ki * TQ, TQ)
        s = lax.dot_general(k_ref[pl.ds(off, TQ), 2 * j * LANES:2 * (j + 1) * LANES],
                            q_ref[:, 2 * j * LANES:2 * (j + 1) * LANES], _NT,
                            preferred_element_type=F32)
        if masked:
            key = lax.broadcasted_iota(jnp.int32, (TQ, TQ), 0)
            qry = lax.broadcasted_iota(jnp.int32, (TQ, TQ), 1)
            s = jnp.where(qry >= key, s, NEG)
        s_ref[j] = s
        smax_ref[j] = jnp.max(s, axis=0, keepdims=True)

    def accumulate(slot, ki, j):
        s_ref, smax_ref = slot
        off = pl.multiple_of(ki * TQ, TQ)
        m = m_scr[j]
        vt = vt_ref[j * V_ROWS:(j + 1) * V_ROWS, pl.ds(off, TQ)]
        m_new = jnp.maximum(m, smax_ref[j])
        alpha = jnp.exp2(m - m_new)
        p = jnp.exp2(s_ref[j] - m_new).astype(BF16)
        acc_scr[j] = alpha * acc_scr[j] + _dot(vt, p)
        m_scr[j] = m_new

    def advance(slot_next, ki_next, slot_cur, ki_cur):
        for j in range(HEADS_PER_STEP):
            scores(slot_next, ki_next, j)
            accumulate(slot_cur, ki_cur, j)

    for j in range(HEADS_PER_STEP):
        m_scr[j] = jnp.full((1, TQ), NEG, F32)
        acc_scr[j] = jnp.zeros((V_ROWS, TQ), F32)

    slot_a = (sa_ref, smaxa_ref)
    slot_b = (sb_ref, smaxb_ref)
    n_pairs = qi // 2
    for j in range(HEADS_PER_STEP):
        scores(slot_a, qi, j, masked=True)

    def pair(t, _):
        advance(slot_b, 2 * t, slot_a, jnp.where(t == 0, qi, 2 * t - 1))
        advance(slot_a, 2 * t + 1, slot_b, 2 * t)
        return 0

    lax.fori_loop(0, n_pairs, pair, 0)
    in_a = jnp.where(n_pairs == 0, qi, 2 * n_pairs - 1)

    @pl.when(qi % 2 == 1)
    def _():
        advance(slot_b, qi - 1, slot_a, in_a)
        for j in range(HEADS_PER_STEP):
            accumulate(slot_b, qi - 1, j)

    @pl.when(qi % 2 == 0)
    def _():
        for j in range(HEADS_PER_STEP):
            accumulate(slot_a, in_a, j)

    for j in range(HEADS_PER_STEP):
        acc = acc_scr[j]
        o = acc[:V_DIM] / acc[V_DIM:V_DIM + 1]
        o_ref[:, j * V_DIM:(j + 1) * V_DIM] = o.T.astype(BF16)


def _sample_attn_kernel(n_pages, pt_ref, qa_ref, qn_ref, knn_ref, latn_ref, kpen_ref,
                        wukt_ref, lat_hbm, kpet_hbm, ctx_ref,
                        latbuf, kpebuf, wq, sems):
    n_seq = pl.num_programs(0)
    b = pl.program_id(0)
    slot = b % 2
    n_past = n_pages * PAGE_SIZE
    n_chunks = n_past // KEY_CHUNK
    pages_per_chunk = n_pages // n_chunks
    n_w = N_HEADS * NOPE_DIM

    def page_copies(seq, slot_, j):
        pg = pt_ref[seq * n_pages + j]
        row = pl.multiple_of(j * PAGE_SIZE, PAGE_SIZE)
        return (
            pltpu.make_async_copy(lat_hbm.at[pg], latbuf.at[slot_, pl.ds(row, PAGE_SIZE)],
                                  sems.at[0, slot_]),
            pltpu.make_async_copy(kpet_hbm.at[pg], kpebuf.at[slot_, :, pl.ds(row, PAGE_SIZE)],
                                  sems.at[1, slot_]),
        )

    def start_pages(seq, slot_, first, count):
        for j in range(count):
            for cp in page_copies(seq, slot_, first + j):
                cp.start()

    def wait_fetch(slot_):
        pltpu.make_async_copy(latbuf.at[1 - slot_], latbuf.at[slot_], sems.at[0, slot_]).wait()
        pltpu.make_async_copy(kpebuf.at[1 - slot_], kpebuf.at[slot_], sems.at[1, slot_]).wait()

    @pl.when(b == 0)
    def _():
        def first_fetch(c, _):
            start_pages(0, 0, c * pages_per_chunk, pages_per_chunk)
            return 0
        lax.fori_loop(0, n_chunks, first_fetch, 0)
        wq[0:n_w, :] = wukt_ref[...]

    wait_fetch(slot)

    qa = qa_ref[0]
    n_q = qa.shape[0]
    wq[n_w:n_w + n_q, :] = qa[:, :KV_RANK]
    q_pe = qa[:, KV_RANK:KV_RANK + ROPE_DIM]
    wq_all = wq[...]

    def scores(latc, pe):
        n = latc.shape[0]
        e = lax.dot_general(wq_all, latc, _NT, preferred_element_type=F32)
        e3 = e[:n_w].reshape(N_HEADS, NOPE_DIM, n)
        r = lax.rsqrt(jnp.sum(e3 * e3, axis=1) * (1.0 / NOPE_DIM) + EPS)
        raw3 = e[n_w:].reshape(n_q // N_HEADS, N_HEADS, n) * r[None]
        return raw3.reshape(n_q, n) + pe

    def new_key_scores():
        kpec = jnp.concatenate(
            [kpen_ref[0], jnp.zeros((LANES - SUBLANES, ROPE_DIM), F32)], axis=0).astype(BF16)
        s_new = lax.dot_general(q_pe, kpec, _NT, preferred_element_type=F32)
        cross = lax.dot_general(qn_ref[0], knn_ref[0], _NT, preferred_element_type=F32)
        n_new = knn_ref.shape[1] // N_HEADS
        row = lax.broadcasted_iota(jnp.int32, cross.shape, 0)
        col = lax.broadcasted_iota(jnp.int32, cross.shape, 1)
        same_head = row % N_HEADS == col % N_HEADS
        t_q = lax.broadcasted_iota(jnp.int32, (n_q, LANES), 0) // N_HEADS
        j_k = lax.broadcasted_iota(jnp.int32, (n_q, LANES), 1)
        for j in range(n_new):
            pick = jnp.where(same_head & (col // N_HEADS == j), cross, 0.0)
            s_new = s_new + jnp.where(j_k == j, jnp.sum(pick, axis=-1, keepdims=True), 0.0)
        return jnp.where(j_k <= t_q, s_new, NEG)

    m = jnp.full((n_q, 1), NEG, F32)
    l = jnp.zeros((n_q, 1), F32)
    acc = jnp.zeros((n_q, KV_RANK), F32)
    pending = None
    for c in range(n_chunks):
        keys = slice(c * KEY_CHUNK, (c + 1) * KEY_CHUNK)
        latc = latbuf[slot, keys, :].astype(BF16)
        kpec = kpebuf[slot, :, keys].astype(BF16)
        s = scores(latc, _dot(q_pe, kpec))
        start_pages((b + 1) % n_seq, 1 - slot, c * pages_per_chunk, pages_per_chunk)
        if c == 0:
            s_new = new_key_scores()
        if pending is not None:
            alpha, p_prev, lat_prev = pending
            acc = alpha * acc + _dot(p_prev, lat_prev)
        m_new = jnp.maximum(m, jnp.max(s, axis=-1, keepdims=True))
        alpha = jnp.exp2(m - m_new)
        p = jnp.exp2(s - m_new)
        l = alpha * l + jnp.sum(p, axis=-1, keepdims=True)
        m = m_new
        pending = (alpha, p.astype(BF16), latc)
    alpha, p_prev, lat_prev = pending
    acc = alpha * acc + _dot(p_prev, lat_prev)

    lat_new = jnp.concatenate(
        [latn_ref[0], jnp.zeros((LANES - SUBLANES, KV_RANK), F32)], axis=0).astype(BF16)
    m_new = jnp.maximum(m, jnp.max(s_new, axis=-1, keepdims=True))
    alpha = jnp.exp2(m - m_new)
    p = jnp.exp2(s_new - m_new)
    l = alpha * l + jnp.sum(p, axis=-1, keepdims=True)
    acc = alpha * acc + _dot(p.astype(BF16), lat_new)
    ctx_ref[0] = acc / l

    @pl.when(b == n_seq - 1)
    def _():
        wait_fetch(1 - slot)


def _absorb_kernel(q_ref, gkn_ref, wukt_ref, o_ref):
    gkn = gkn_ref[...]
    w = KV_RANK + LANES
    for h in range(N_HEADS):
        qn = (q_ref[:, 2 * h * LANES:(2 * h + 1) * LANES].astype(F32) * gkn).astype(BF16)
        o_ref[:, h * w:h * w + KV_RANK] = _dot(
            qn, wukt_ref[h * NOPE_DIM:(h + 1) * NOPE_DIM, :]).astype(BF16)
        o_ref[:, h * w + KV_RANK:(h + 1) * w] = q_ref[:, (2 * h + 1) * LANES:(2 * h + 2) * LANES]


def _ctx_out_kernel(c_ref, wuv_ref, o_ref):
    for h in range(N_HEADS):
        c = c_ref[:, h * KV_RANK:(h + 1) * KV_RANK].astype(BF16)
        o_ref[:, h * V_DIM:(h + 1) * V_DIM] = _dot(
            c, wuv_ref[:, h * V_DIM:(h + 1) * V_DIM]).astype(BF16)


def _params(n_axes=1):
    return pltpu.CompilerParams(dimension_semantics=("arbitrary",) * n_axes,
                                vmem_limit_bytes=VMEM_LIMIT)


def _row_spec(width):
    return pl.BlockSpec((TM, width), lambda i: (i, 0))


def _table_spec(n_rows):
    tiles = n_rows // TM
    return pl.BlockSpec((TM, LANES), lambda i: (i % tiles, 0))


def _ffn_specs(layer):
    pick = lambda *_: (layer, 0, 0)
    return [_const_spec((1, D_MODEL)),
            pl.BlockSpec((None, D_MODEL, 2 * D_FF), pick, pipeline_mode=pl.Buffered(1)),
            pl.BlockSpec((None, D_FF, D_MODEL), pick, pipeline_mode=pl.Buffered(1))]


def _ffn_call(x, g, wgu, wd, layer):
    t = x.shape[0]
    return pl.pallas_call(
        _ffn_kernel, grid=(t // TM,),
        in_specs=[_row_spec(D_MODEL)] + _ffn_specs(layer),
        out_specs=_row_spec(D_MODEL),
        out_shape=jax.ShapeDtypeStruct((t, D_MODEL), F32),
        compiler_params=_params(), name="ffn")(x, g, wgu, wd)


def _mixer_call(x, g, win, cw, wout, tiles_per_seq, prev=None):
    t = x.shape[0]
    n_tiles = t // TM
    is_sample = prev is not None
    in_specs = [_row_spec(D_MODEL), _const_spec((1, D_MODEL)),
                _const_spec((D_MODEL, 3 * D_MODEL)), _const_spec((CONV_W, D_MODEL)),
                _const_spec((D_MODEL, D_MODEL))]
    args = [x, g, win, cw, wout]
    if is_sample:
        in_specs += [_row_spec(D_MODEL), _row_spec(D_MODEL)]
        args += list(prev)
        u_spec = _row_spec(D_MODEL)
        u_shape = jax.ShapeDtypeStruct((t, D_MODEL), F32)
    else:
        u_spec = pl.BlockSpec((1, SUBLANES, D_MODEL), lambda i: (i, 0, 0))
        u_shape = jax.ShapeDtypeStruct((n_tiles, SUBLANES, D_MODEL), F32)
    return pl.pallas_call(
        functools.partial(_mixer_kernel, tiles_per_seq, is_sample), grid=(n_tiles,),
        in_specs=in_specs,
        out_specs=[_row_spec(D_MODEL), u_spec],
        out_shape=[jax.ShapeDtypeStruct((t, D_MODEL), F32), u_shape],
        scratch_shapes=[pltpu.VMEM((TM + SUBLANES, D_MODEL), F32)],
        compiler_params=_params(), name="mixer")(*args)


def _ffn_kv_call(x, g, wgu, wd, layer, gkv, wdkv, kvn, ga, gb, cos_t, sin_t, wuk, wuvt, gkn):
    t = x.shape[0]
    in_specs = ([_row_spec(D_MODEL)] + _ffn_specs(layer) + [
        _const_spec((1, D_MODEL)), _const_spec((D_MODEL, KV_RANK + 2 * LANES)),
        _const_spec((1, KV_RANK)), _const_spec((1, LANES)), _const_spec((1, LANES)),
        _table_spec(cos_t.shape[0]), _table_spec(sin_t.shape[0]),
        _const_spec((KV_RANK, D_MODEL)), _const_spec((D_MODEL, KV_RANK)),
        _const_spec((1, NOPE_DIM))])
    out_specs = [_row_spec(D_MODEL), _row_spec(KV_RANK), _row_spec(ROPE_DIM),
                 _row_spec(2 * N_HEADS * LANES),
                 pl.BlockSpec((N_HEADS * V_ROWS, TM), lambda i: (0, i))]
    out_shape = [jax.ShapeDtypeStruct((t, D_MODEL), F32),
                 jax.ShapeDtypeStruct((t, KV_RANK), F32),
                 jax.ShapeDtypeStruct((t, ROPE_DIM), F32),
                 jax.ShapeDtypeStruct((t, 2 * N_HEADS * LANES), BF16),
                 jax.ShapeDtypeStruct((N_HEADS * V_ROWS, t), BF16)]
    return pl.pallas_call(
        _ffn_kv_kernel, grid=(t // TM,), in_specs=in_specs, out_specs=out_specs,
        out_shape=out_shape, compiler_params=_params(), name="ffn_kv")(
            x, g, wgu, wd, gkv, wdkv, kvn, ga, gb, cos_t, sin_t, wuk, wuvt, gkn)


def _ffn_q_call(x, g, wgu, wd, layer, gmix, wdq, qnorm, wuq, gqn, ga, gb, cos_t, sin_t):
    t = x.shape[0]
    in_specs = ([_row_spec(D_MODEL)] + _ffn_specs(layer) + [
        _const_spec((1, D_MODEL)), _const_spec((D_MODEL, Q_RANK)),
        _const_spec((1, Q_RANK)), _const_spec((Q_RANK, 3 * N_HEADS * LANES)),
        _const_spec((1, NOPE_DIM)), _const_spec((1, LANES)), _const_spec((1, LANES)),
        _table_spec(cos_t.shape[0]), _table_spec(sin_t.shape[0])])
    return pl.pallas_call(
        _ffn_q_kernel, grid=(t // TM,), in_specs=in_specs,
        out_specs=[_row_spec(D_MODEL), _row_spec(2 * N_HEADS * LANES)],
        out_shape=[jax.ShapeDtypeStruct((t, D_MODEL), F32),
                   jax.ShapeDtypeStruct((t, 2 * N_HEADS * LANES), BF16)],
        compiler_params=_params(), name="ffn_q")(
            x, g, wgu, wd, gmix, wdq, qnorm, wuq, gqn, ga, gb, cos_t, sin_t)


def _out_ffn_call(x, a, wo, g, wgu, wd, layer):
    t = x.shape[0]
    return pl.pallas_call(
        _out_ffn_kernel, grid=(t // TM,),
        in_specs=[_row_spec(D_MODEL), _row_spec(D_MODEL),
                  _const_spec((D_MODEL, D_MODEL))] + _ffn_specs(layer),
        out_specs=_row_spec(D_MODEL),
        out_shape=jax.ShapeDtypeStruct((t, D_MODEL), F32),
        compiler_params=_params(), name="out_ffn")(x, a, wo, g, wgu, wd)


def _prompt_attn_call(q, k, vt, batch, seq):
    nq = seq // TQ
    hps = HEADS_PER_STEP
    resident = pl.Buffered(1)
    return pl.pallas_call(
        _prompt_attn_kernel, grid=(batch, N_HEADS // hps, nq),
        in_specs=[pl.BlockSpec((TQ, hps * 2 * LANES), lambda b, h, i: (b * nq + i, h)),
                  pl.BlockSpec((seq, hps * 2 * LANES), lambda b, h, i: (b, h),
                               pipeline_mode=resident),
                  pl.BlockSpec((hps * V_ROWS, seq), lambda b, h, i: (h, b),
                               pipeline_mode=resident)],
        out_specs=pl.BlockSpec((TQ, hps * V_DIM), lambda b, h, i: (b * nq + i, h)),
        out_shape=jax.ShapeDtypeStruct((batch * seq, N_HEADS * V_DIM), BF16),
        scratch_shapes=[pltpu.VMEM((hps, TQ, TQ), F32), pltpu.VMEM((hps, TQ, TQ), F32),
                        pltpu.VMEM((hps, 1, TQ), F32), pltpu.VMEM((hps, 1, TQ), F32),
                        pltpu.VMEM((hps, 1, TQ), F32), pltpu.VMEM((hps, V_ROWS, TQ), F32)],
        compiler_params=_params(3), name="prompt_attn")(q, k, vt)


def _sample_attn_call(page_table, qa, qn, kn_new, lat_new, kpe_new, wukt,
                      cache_latent, cache_kpe):
    n_seq, n_pages = page_table.shape
    n_q = qa.shape[1]
    n_keys = n_pages * PAGE_SIZE
    grid_spec = pltpu.PrefetchScalarGridSpec(
        num_scalar_prefetch=1, grid=(n_seq,),
        in_specs=[pl.BlockSpec((1, n_q, KV_RANK + LANES), lambda b, pt: (b, 0, 0)),
                  pl.BlockSpec((1, n_q, NOPE_DIM), lambda b, pt: (b, 0, 0)),
                  pl.BlockSpec((1, kn_new.shape[1], NOPE_DIM), lambda b, pt: (b, 0, 0)),
                  pl.BlockSpec((1, SUBLANES, KV_RANK), lambda b, pt: (b, 0, 0)),
                  pl.BlockSpec((1, SUBLANES, ROPE_DIM), lambda b, pt: (b, 0, 0)),
                  pl.BlockSpec((N_HEADS * NOPE_DIM, KV_RANK), lambda b, pt: (0, 0),
                               pipeline_mode=pl.Buffered(1)),
                  pl.BlockSpec(memory_space=pl.ANY),
                  pl.BlockSpec(memory_space=pl.ANY)],
        out_specs=pl.BlockSpec((1, n_q, KV_RANK), lambda b, pt: (b, 0, 0)),
        scratch_shapes=[pltpu.VMEM((2, n_keys, KV_RANK), F32),
                        pltpu.VMEM((2, ROPE_DIM, n_keys), F32),
                        pltpu.VMEM((N_HEADS * NOPE_DIM + n_q, KV_RANK), BF16),
                        pltpu.SemaphoreType.DMA((2, 2))])
    return pl.pallas_call(
        functools.partial(_sample_attn_kernel, n_pages), grid_spec=grid_spec,
        out_shape=jax.ShapeDtypeStruct((n_seq, n_q, KV_RANK), F32),
        compiler_params=_params(), name="sample_attn")(
            page_table.reshape(-1), qa, qn, kn_new, lat_new, kpe_new, wukt, cache_latent,
            jnp.swapaxes(cache_kpe, 1, 2))


def _absorb_call(q, gkn, wukt):
    t = q.shape[0]
    return pl.pallas_call(
        _absorb_kernel, grid=(t // TM,),
        in_specs=[_row_spec(2 * N_HEADS * LANES), _const_spec((1, NOPE_DIM)),
                  _const_spec((N_HEADS * NOPE_DIM, KV_RANK))],
        out_specs=_row_spec(N_HEADS * (KV_RANK + LANES)),
        out_shape=jax.ShapeDtypeStruct((t, N_HEADS * (KV_RANK + LANES)), BF16),
        compiler_params=_params(), name="absorb")(q, gkn, wukt)


def _ctx_out_call(ctx, wuv):
    t = ctx.shape[0]
    return pl.pallas_call(
        _ctx_out_kernel, grid=(t // TM,),
        in_specs=[_row_spec(N_HEADS * KV_RANK),
                  _const_spec((KV_RANK, N_HEADS * V_DIM))],
        out_specs=_row_spec(N_HEADS * V_DIM),
        out_shape=jax.ShapeDtypeStruct((t, N_HEADS * V_DIM), BF16),
        compiler_params=_params(), name="ctx_out")(ctx, wuv)


def _rope_tables(pos):
    half = ROPE_DIM // 2
    freqs = ROPE_THETA ** (-jnp.arange(half, dtype=F32) / half)
    ang = pos.astype(F32)[:, None] * freqs
    cos = jnp.cos(ang)
    sin = jnp.sin(ang)
    zero = jnp.zeros((pos.shape[0], LANES - ROPE_DIM), F32)
    return (jnp.concatenate([cos, cos, zero], axis=1),
            jnp.concatenate([-sin, sin, zero], axis=1))


def _swap_halves(a):
    half = a.shape[-1] // 2
    return jnp.concatenate([a[..., half:], a[..., :half]], axis=-1)


def _pad_lanes(a):
    pad = [(0, 0)] * (a.ndim - 1) + [(0, LANES - a.shape[-1])]
    return jnp.pad(a, pad)


def kernel(x_prompt, x_sample, cache_latent, cache_kpe, state_conv, page_table, norm_ffn1, w_ffn1_gu, w_ffn1_down, norm_mix, norm_ffn2, w_ffn2_gu, w_ffn2_down, w_conv_in, conv_w, w_conv_out, w_dq, q_norm, w_uq, q_nope_norm, q_pe_norm, w_o, norm_kv_in, w_dkv, kv_norm, k_pe_norm, w_uk, w_uv, k_nope_norm):
    batch, seq, _ = x_prompt.shape
    n_dec, dec_seq, _ = x_sample.shape
    n_pages = page_table.shape[1]
    assert seq % TM == 0 and (n_dec * dec_seq) % TM == 0 and dec_seq == 4

    row = lambda a: a.reshape(1, -1)
    ffn1 = (w_ffn1_gu.astype(BF16), w_ffn1_down.astype(BF16))
    ffn2 = (w_ffn2_gu.astype(BF16), w_ffn2_down.astype(BF16))
    win = w_conv_in[0].astype(BF16)
    wout = w_conv_out[0].astype(BF16)
    w_pe = w_dkv[:, KV_RANK:]
    wdkv = jnp.concatenate([w_dkv[:, :KV_RANK], _pad_lanes(w_pe),
                            _pad_lanes(_swap_halves(w_pe))], axis=1).astype(BF16)
    wuk = w_uk.astype(BF16)
    wuvt = w_uv.T.astype(BF16)
    wukt = w_uk.T.astype(BF16)
    wuv = w_uv.astype(BF16)
    wdq = w_dq[0].astype(BF16)
    wuq3 = w_uq[0].reshape(Q_RANK, N_HEADS, NOPE_DIM + ROPE_DIM)
    wq_pe = wuq3[:, :, NOPE_DIM:]
    wuq = jnp.concatenate([
        wuq3[:, :, :NOPE_DIM].reshape(Q_RANK, -1),
        _pad_lanes(wq_pe).reshape(Q_RANK, -1),
        _pad_lanes(_swap_halves(wq_pe)).reshape(Q_RANK, -1)], axis=1).astype(BF16)
    wo = w_o[0].astype(BF16)
    kga = row(_pad_lanes(k_pe_norm))
    kgb = row(_pad_lanes(_swap_halves(k_pe_norm)))
    qga = row(_pad_lanes(q_pe_norm[0]))
    qgb = row(_pad_lanes(_swap_halves(q_pe_norm[0])))

    cos_p, sin_p = _rope_tables(jnp.arange(seq))
    pos_s = n_pages * PAGE_SIZE + jnp.arange(n_dec * dec_seq) % dec_seq
    cos_s, sin_s = _rope_tables(pos_s)

    prev = state_conv[0]
    zrow = jnp.zeros((n_dec, 1, D_MODEL), F32)
    p1 = jnp.concatenate([prev[:, 1:2], zrow, zrow, zrow], axis=1).reshape(-1, D_MODEL)
    p2 = jnp.concatenate([prev[:, 0:1], prev[:, 1:2], zrow, zrow], axis=1).reshape(-1, D_MODEL)

    def dense(x, tiles_per_seq, cos_t, sin_t, prev_rows):
        x = _ffn_call(x, row(norm_ffn1[0]), *ffn1, 0)
        x, u = _mixer_call(x, row(norm_mix[0]), win, conv_w[0], wout, tiles_per_seq, prev_rows)
        x, lat, kpe, k, vt = _ffn_kv_call(
            x, row(norm_ffn2[0]), *ffn2, 0, row(norm_kv_in), wdkv, row(kv_norm),
            kga, kgb, cos_t, sin_t, wuk, wuvt, row(k_nope_norm))
        x, q = _ffn_q_call(
            x, row(norm_ffn1[1]), *ffn1, 1, row(norm_mix[1]), wdq, row(q_norm[0]), wuq,
            row(q_nope_norm[0]), qga, qgb, cos_t, sin_t)
        return x, u, lat, kpe, k, vt, q

    xp, up, lat_p, kpe_p, k_p, vt_p, q_p = dense(
        x_prompt.reshape(batch * seq, D_MODEL), seq // TM, cos_p, sin_p, None)
    o_p = _prompt_attn_call(q_p, k_p, vt_p, batch, seq)
    y_p = _out_ffn_call(xp, o_p, wo, row(norm_ffn2[1]), *ffn2, 1)
    conv_p = up.reshape(batch, seq // TM, SUBLANES, D_MODEL)[:, -1, SUBLANES - 2:][None]

    xs, us, lat_s, kpe_s, k_s, _, q_s = dense(
        x_sample.reshape(n_dec * dec_seq, D_MODEL), 1, cos_s, sin_s, (p1, p2))
    qa = _absorb_call(q_s, row(k_nope_norm), wukt)
    qa = qa.reshape(n_dec, dec_seq * N_HEADS, KV_RANK + LANES)
    pad_rows = ((0, 0), (0, SUBLANES - dec_seq), (0, 0))
    lat_new = jnp.pad(lat_s.reshape(n_dec, dec_seq, KV_RANK), pad_rows)
    kpe_new = jnp.pad(kpe_s.reshape(n_dec, dec_seq, ROPE_DIM), pad_rows)
    qn_s = q_s.reshape(n_dec, dec_seq * N_HEADS, 2 * LANES)[:, :, :NOPE_DIM]
    kn_new = k_s.reshape(n_dec, dec_seq * N_HEADS, 2 * LANES)[:, :, :NOPE_DIM]
    ctx = _sample_attn_call(page_table, qa, qn_s, kn_new, lat_new, kpe_new, wukt,
                            cache_latent, cache_kpe)
    o_s = _ctx_out_call(ctx.reshape(n_dec * dec_seq, N_HEADS * KV_RANK), wuv)
    y_s = _out_ffn_call(xs, o_s, wo, row(norm_ffn2[1]), *ffn2, 1)
    conv_s = us.reshape(n_dec, dec_seq, D_MODEL)[:, dec_seq - (CONV_W - 1):][None]

    return (y_p.reshape(batch, seq, D_MODEL),
            y_s.reshape(n_dec, dec_seq, D_MODEL),
            conv_p, conv_s,
            lat_p.reshape(batch, seq, KV_RANK),
            kpe_p.reshape(batch, seq, ROPE_DIM),
            lat_s.reshape(n_dec, dec_seq, KV_RANK),
            kpe_s.reshape(n_dec, dec_seq, ROPE_DIM))
```

```python
import functools

import jax
import jax.numpy as jnp
from jax import lax
from jax.experimental import pallas as pl
from jax.experimental.pallas import tpu as pltpu

F32 = jnp.float32
BF16 = jnp.bfloat16

D_MODEL = 1024
D_FF = 2816
N_HEADS = 8
NOPE_DIM = 128
ROPE_DIM = 64
V_DIM = 128
Q_RANK = 512
KV_RANK = 256
CONV_W = 3
PAGE_SIZE = 128
ROPE_THETA = 10000.0
EPS = 1e-6
SCALE = (NOPE_DIM + ROPE_DIM) ** -0.5
LOG2E = 1.4426950408889634
Q_SCALE = SCALE * LOG2E
V_ROWS = V_DIM + 16
NEG = -1e30

LANES = 128
SUBLANES = 8
TM = 512
FF_CHUNKS = 1
FC = D_FF // FF_CHUNKS
TQ = 512
HEADS_PER_STEP = 4
KEY_CHUNK = 1024
VMEM_LIMIT = 56 * 1024 * 1024

_NT = (((1,), (1,)), ((), ()))


def _const_spec(shape):
    zeros = (0,) * len(shape)
    return pl.BlockSpec(shape, lambda *_: zeros, pipeline_mode=pl.Buffered(1))


def _rms(x, g):
    ms = jnp.mean(x * x, axis=-1, keepdims=True)
    return x * lax.rsqrt(ms + EPS) * g


def _dot(a, b):
    return jnp.dot(a, b, preferred_element_type=F32)


def _ffn(x, g, wgu_ref, wd_ref):
    xn = _rms(x, g).astype(BF16)
    acc = None
    for c in range(FF_CHUNKS):
        gate = _dot(xn, wgu_ref[:, c * FC:(c + 1) * FC])
        up = _dot(xn, wgu_ref[:, D_FF + c * FC:D_FF + (c + 1) * FC])
        act = (gate * jax.nn.sigmoid(gate) * up).astype(BF16)
        part = _dot(act, wd_ref[c * FC:(c + 1) * FC, :])
        acc = part if acc is None else acc + part
    return x + 0.5 * acc


def _rope_norm(a, b, ga, gb, cos_t, sin_t):
    ms = jnp.sum(a * a, axis=-1, keepdims=True) * (1.0 / ROPE_DIM)
    return (a * ga * cos_t + b * gb * sin_t) * lax.rsqrt(ms + EPS)


def _ffn_kernel(x_ref, g_ref, wgu_ref, wd_ref, o_ref):
    o_ref[...] = _ffn(x_ref[...], g_ref[...], wgu_ref, wd_ref)


def _mixer_kernel(tiles_per_seq, is_sample, *refs):
    if is_sample:
        (x_ref, g_ref, win_ref, cw_ref, wout_ref, p1_ref, p2_ref,
         o_ref, u_ref, ubuf) = refs
    else:
        x_ref, g_ref, win_ref, cw_ref, wout_ref, o_ref, u_ref, ubuf = refs
    i = pl.program_id(0)
    x = x_ref[...]
    xn = _rms(x, g_ref[...]).astype(BF16)
    bch = _dot(xn, win_ref[...])
    b = bch[:, :D_MODEL]
    u = bch[:, D_MODEL:2 * D_MODEL] * bch[:, 2 * D_MODEL:]

    @pl.when(i % tiles_per_seq == 0)
    def _():
        ubuf[0:SUBLANES, :] = jnp.zeros((SUBLANES, D_MODEL), F32)

    ubuf[SUBLANES:SUBLANES + TM, :] = u
    s1 = ubuf[SUBLANES - 1:SUBLANES - 1 + TM, :]
    s2 = ubuf[SUBLANES - 2:SUBLANES - 2 + TM, :]
    if is_sample:
        t = lax.broadcasted_iota(jnp.int32, (TM, 1), 0) % 4
        s1 = jnp.where(t >= 1, s1, p1_ref[...])
        s2 = jnp.where(t >= 2, s2, p2_ref[...])
    cw = cw_ref[...]
    y = cw[0:1, :] * s2 + cw[1:2, :] * s1 + cw[2:3, :] * u
    o_ref[...] = x + _dot((b * y).astype(BF16), wout_ref[...])
    if is_sample:
        u_ref[...] = u
    else:
        ubuf[0:SUBLANES, :] = u[TM - SUBLANES:, :]
        u_ref[0] = u[TM - SUBLANES:, :]


def _ffn_kv_kernel(x_ref, g_ref, wgu_ref, wd_ref, gkv_ref, wdkv_ref, kvn_ref,
                   ga_ref, gb_ref, cos_ref, sin_ref, wuk_ref, wuvt_ref, gkn_ref,
                   o_ref, lat_ref, kpe_ref, k_ref, vt_ref):
    x = _ffn(x_ref[...], g_ref[...], wgu_ref, wd_ref)
    o_ref[...] = x
    hk = _rms(x, gkv_ref[...]).astype(BF16)
    ckv = _dot(hk, wdkv_ref[...])
    lat = _rms(ckv[:, :KV_RANK], kvn_ref[...])
    lat_ref[...] = lat
    kpe = _rope_norm(ckv[:, KV_RANK:KV_RANK + LANES], ckv[:, KV_RANK + LANES:],
                     ga_ref[...], gb_ref[...], cos_ref[...], sin_ref[...])
    kpe_ref[...] = kpe[:, :ROPE_DIM]
    kpeb = kpe.astype(BF16)
    latb = lat.astype(BF16)
    ek = _dot(latb, wuk_ref[...])
    gkn = gkn_ref[...]
    for h in range(N_HEADS):
        e = ek[:, h * NOPE_DIM:(h + 1) * NOPE_DIM]
        k_ref[:, 2 * h * LANES:(2 * h + 1) * LANES] = _rms(e, gkn).astype(BF16)
        k_ref[:, (2 * h + 1) * LANES:(2 * h + 2) * LANES] = kpeb
    vt = lax.dot_general(wuvt_ref[...], latb, _NT, preferred_element_type=F32).astype(BF16)
    ones = jnp.ones((V_ROWS - V_DIM, vt.shape[1]), BF16)
    for h in range(N_HEADS):
        vt_ref[h * V_ROWS:h * V_ROWS + V_DIM, :] = vt[h * V_DIM:(h + 1) * V_DIM, :]
        vt_ref[h * V_ROWS + V_DIM:(h + 1) * V_ROWS, :] = ones


def _ffn_q_kernel(x_ref, g_ref, wgu_ref, wd_ref, gmix_ref, wdq_ref, qn_ref,
                  wuq_ref, gqn_ref, ga_ref, gb_ref, cos_ref, sin_ref,
                  o_ref, q_ref):
    x = _ffn(x_ref[...], g_ref[...], wgu_ref, wd_ref)
    o_ref[...] = x
    xn = _rms(x, gmix_ref[...]).astype(BF16)
    cq = _rms(_dot(xn, wdq_ref[...]), qn_ref[...]).astype(BF16)
    q = _dot(cq, wuq_ref[...])
    gqn = gqn_ref[...]
    ga = ga_ref[...]
    gb = gb_ref[...]
    cos_t = cos_ref[...]
    sin_t = sin_ref[...]
    hw = N_HEADS * LANES
    for h in range(N_HEADS):
        qn = _rms(q[:, h * LANES:(h + 1) * LANES], gqn) * Q_SCALE
        qp = _rope_norm(q[:, hw + h * LANES:hw + (h + 1) * LANES],
                        q[:, 2 * hw + h * LANES:2 * hw + (h + 1) * LANES],
                        ga, gb, cos_t, sin_t) * Q_SCALE
        q_ref[:, 2 * h * LANES:(2 * h + 1) * LANES] = qn.astype(BF16)
        q_ref[:, (2 * h + 1) * LANES:(2 * h + 2) * LANES] = qp.astype(BF16)


def _out_ffn_kernel(x_ref, a_ref, wo_ref, g_ref, wgu_ref, wd_ref, o_ref):
    x = x_ref[...] + _dot(a_ref[...], wo_ref[...])
    o_ref[...] = _ffn(x, g_ref[...], wgu_ref, wd_ref)


def _prompt_attn_kernel(q_ref, k_ref, vt_ref, o_ref,
                        sa_ref, sb_ref, smaxa_ref, smaxb_ref, m_scr, acc_scr):
    qi = pl.program_id(2)

    def scores(slot, ki, j, masked=False):
        s_ref, smax_ref = slot
        off = pl.multiple_of(ki * TQ, TQ)
        s = lax.dot_general(k_ref[pl.ds(off, TQ), 2 * j * LANES:2 * (j + 1) * LANES],
                            q_ref[:, 2 * j * LANES:2 * (j + 1) * LANES], _NT,
                            preferred_element_type=F32)
        if masked:
            key = lax.broadcasted_iota(jnp.int32, (TQ, TQ), 0)
            qry = lax.broadcasted_iota(jnp.int32, (TQ, TQ), 1)
            s = jnp.where(qry >= key, s, NEG)
        s_ref[j] = s
        smax_ref[j] = jnp.max(s, axis=0, keepdims=True)

    def accumulate(slot, ki, j):
        s_ref, smax_ref = slot
        off = pl.multiple_of(ki * TQ, TQ)
        m = m_scr[j]
        vt = vt_ref[j * V_ROWS:(j + 1) * V_ROWS, pl.ds(off, TQ)]
        m_new = jnp.maximum(m, smax_ref[j])
        alpha = jnp.exp2(m - m_new)
        p = jnp.exp2(s_ref[j] - m_new).astype(BF16)
        acc_scr[j] = alpha * acc_scr[j] + _dot(vt, p)
        m_scr[j] = m_new

    def advance(slot_next, ki_next, slot_cur, ki_cur):
        for j in range(HEADS_PER_STEP):
            scores(slot_next, ki_next, j)
            accumulate(slot_cur, ki_cur, j)

    for j in range(HEADS_PER_STEP):
        m_scr[j] = jnp.full((1, TQ), NEG, F32)
        acc_scr[j] = jnp.zeros((V_ROWS, TQ), F32)

    slot_a = (sa_ref, smaxa_ref)
    slot_b = (sb_ref, smaxb_ref)
    n_pairs = qi // 2
    for j in range(HEADS_PER_STEP):
        scores(slot_a, qi, j, masked=True)

    def pair(t, _):
        advance(slot_b, 2 * t, slot_a, jnp.where(t == 0, qi, 2 * t - 1))
        advance(slot_a, 2 * t + 1, slot_b, 2 * t)
        return 0

    lax.fori_loop(0, n_pairs, pair, 0)
    in_a = jnp.where(n_pairs == 0, qi, 2 * n_pairs - 1)

    @pl.when(qi % 2 == 1)
    def _():
        advance(slot_b, qi - 1, slot_a, in_a)
        for j in range(HEADS_PER_STEP):
            accumulate(slot_b, qi - 1, j)

    @pl.when(qi % 2 == 0)
    def _():
        for j in range(HEADS_PER_STEP):
            accumulate(slot_a, in_a, j)

    for j in range(HEADS_PER_STEP):
        acc = acc_scr[j]
        o = acc[:V_DIM] / acc[V_DIM:V_DIM + 1]
        o_ref[:, j * V_DIM:(j + 1) * V_DIM] = o.T.astype(BF16)


def _sample_attn_kernel(n_pages, pt_ref, qa_ref, qn_ref, knn_ref, latn_ref, kpen_ref,
                        wukt_ref, lat_hbm, kpet_hbm, ctx_ref,
                        latbuf, kpebuf, wq, sems):
    n_seq = pl.num_programs(0)
    b = pl.program_id(0)
    slot = b % 2
    n_past = n_pages * PAGE_SIZE
    n_chunks = n_past // KEY_CHUNK
    pages_per_chunk = n_pages // n_chunks
    n_w = N_HEADS * NOPE_DIM

    def page_copies(seq, slot_, j):
        pg = pt_ref[seq * n_pages + j]
        row = pl.multiple_of(j * PAGE_SIZE, PAGE_SIZE)
        return (
            pltpu.make_async_copy(lat_hbm.at[pg], latbuf.at[slot_, pl.ds(row, PAGE_SIZE)],
                                  sems.at[0, slot_]),
            pltpu.make_async_copy(kpet_hbm.at[pg], kpebuf.at[slot_, :, pl.ds(row, PAGE_SIZE)],
                                  sems.at[1, slot_]),
        )

    def start_pages(seq, slot_, first, count):
        for j in range(count):
            for cp in page_copies(seq, slot_, first + j):
                cp.start()

    def wait_fetch(slot_):
        pltpu.make_async_copy(latbuf.at[1 - slot_], latbuf.at[slot_], sems.at[0, slot_]).wait()
        pltpu.make_async_copy(kpebuf.at[1 - slot_], kpebuf.at[slot_], sems.at[1, slot_]).wait()

    @pl.when(b == 0)
    def _():
        def first_fetch(c, _):
            start_pages(0, 0, c * pages_per_chunk, pages_per_chunk)
            return 0
        lax.fori_loop(0, n_chunks, first_fetch, 0)
        wq[0:n_w, :] = wukt_ref[...]

    wait_fetch(slot)

    qa = qa_ref[0]
    n_q = qa.shape[0]
    wq[n_w:n_w + n_q, :] = qa[:, :KV_RANK]
    q_pe = qa[:, KV_RANK:KV_RANK + ROPE_DIM]
    wq_all = wq[...]

    def scores(latc, pe):
        n = latc.shape[0]
        e = lax.dot_general(wq_all, latc, _NT, preferred_element_type=F32)
        e3 = e[:n_w].reshape(N_HEADS, NOPE_DIM, n)
        r = lax.rsqrt(jnp.sum(e3 * e3, axis=1) * (1.0 / NOPE_DIM) + EPS)
        raw3 = e[n_w:].reshape(n_q // N_HEADS, N_HEADS, n) * r[None]
        return raw3.reshape(n_q, n) + pe

    def new_key_scores():
        kpec = jnp.concatenate(
            [kpen_ref[0], jnp.zeros((LANES - SUBLANES, ROPE_DIM), F32)], axis=0).astype(BF16)
        s_new = lax.dot_general(q_pe, kpec, _NT, preferred_element_type=F32)
        cross = lax.dot_general(qn_ref[0], knn_ref[0], _NT, preferred_element_type=F32)
        n_new = knn_ref.shape[1] // N_HEADS
        row = lax.broadcasted_iota(jnp.int32, cross.shape, 0)
        col = lax.broadcasted_iota(jnp.int32, cross.shape, 1)
        same_head = row % N_HEADS == col % N_HEADS
        t_q = lax.broadcasted_iota(jnp.int32, (n_q, LANES), 0) // N_HEADS
        j_k = lax.broadcasted_iota(jnp.int32, (n_q, LANES), 1)
        for j in range(n_new):
            pick = jnp.where(same_head & (col // N_HEADS == j), cross, 0.0)
            s_new = s_new + jnp.where(j_k == j, jnp.sum(pick, axis=-1, keepdims=True), 0.0)
        return jnp.where(j_k <= t_q, s_new, NEG)

    m = jnp.full((n_q, 1), NEG, F32)
    l = jnp.zeros((n_q, 1), F32)
    acc = jnp.zeros((n_q, KV_RANK), F32)
    pending = None
    for c in range(n_chunks):
        keys = slice(c * KEY_CHUNK, (c + 1) * KEY_CHUNK)
        latc = latbuf[slot, keys, :].astype(BF16)
        kpec = kpebuf[slot, :, keys].astype(BF16)
        s = scores(latc, _dot(q_pe, kpec))
        first = 0 if c == 0 else (c + 1) * pages_per_chunk
        count = 2 * pages_per_chunk if c == 0 else (pages_per_chunk if c + 1 < n_chunks else 0)
        start_pages((b + 1) % n_seq, 1 - slot, first, count)
        if c == 0:
            s_new = new_key_scores()
        if pending is not None:
            alpha, p_prev, lat_prev = pending
            acc = alpha * acc + _dot(p_prev, lat_prev)
        m_new = jnp.maximum(m, jnp.max(s, axis=-1, keepdims=True))
        alpha = jnp.exp2(m - m_new)
        p = jnp.exp2(s - m_new)
        l = alpha * l + jnp.sum(p, axis=-1, keepdims=True)
        m = m_new
        pending = (alpha, p.astype(BF16), latc)
    alpha, p_prev, lat_prev = pending
    acc = alpha * acc + _dot(p_prev, lat_prev)

    lat_new = jnp.concatenate(
        [latn_ref[0], jnp.zeros((LANES - SUBLANES, KV_RANK), F32)], axis=0).astype(BF16)
    m_new = jnp.maximum(m, jnp.max(s_new, axis=-1, keepdims=True))
    alpha = jnp.exp2(m - m_new)
    p = jnp.exp2(s_new - m_new)
    l = alpha * l + jnp.sum(p, axis=-1, keepdims=True)
    acc = alpha * acc + _dot(p.astype(BF16), lat_new)
    ctx_ref[0] = acc / l

    @pl.when(b == n_seq - 1)
    def _():
        wait_fetch(1 - slot)


def _absorb_kernel(q_ref, gkn_ref, wukt_ref, o_ref):
    gkn = gkn_ref[...]
    w = KV_RANK + LANES
    for h in range(N_HEADS):
        qn = (q_ref[:, 2 * h * LANES:(2 * h + 1) * LANES].astype(F32) * gkn).astype(BF16)
        o_ref[:, h * w:h * w + KV_RANK] = _dot(
            qn, wukt_ref[h * NOPE_DIM:(h + 1) * NOPE_DIM, :]).astype(BF16)
        o_ref[:, h * w + KV_RANK:(h + 1) * w] = q_ref[:, (2 * h + 1) * LANES:(2 * h + 2) * LANES]


def _ctx_out_kernel(c_ref, wuv_ref, o_ref):
    for h in range(N_HEADS):
        c = c_ref[:, h * KV_RANK:(h + 1) * KV_RANK].astype(BF16)
        o_ref[:, h * V_DIM:(h + 1) * V_DIM] = _dot(
            c, wuv_ref[:, h * V_DIM:(h + 1) * V_DIM]).astype(BF16)


def _params(n_axes=1):
    return pltpu.CompilerParams(dimension_semantics=("arbitrary",) * n_axes,
                                vmem_limit_bytes=VMEM_LIMIT)


def _row_spec(width):
    return pl.BlockSpec((TM, width), lambda i: (i, 0))


def _table_spec(n_rows):
    tiles = n_rows // TM
    return pl.BlockSpec((TM, LANES), lambda i: (i % tiles, 0))


def _ffn_specs(layer):
    pick = lambda *_: (layer, 0, 0)
    return [_const_spec((1, D_MODEL)),
            pl.BlockSpec((None, D_MODEL, 2 * D_FF), pick, pipeline_mode=pl.Buffered(1)),
            pl.BlockSpec((None, D_FF, D_MODEL), pick, pipeline_mode=pl.Buffered(1))]


def _ffn_call(x, g, wgu, wd, layer):
    t = x.shape[0]
    return pl.pallas_call(
        _ffn_kernel, grid=(t // TM,),
        in_specs=[_row_spec(D_MODEL)] + _ffn_specs(layer),
        out_specs=_row_spec(D_MODEL),
        out_shape=jax.ShapeDtypeStruct((t, D_MODEL), F32),
        compiler_params=_params(), name="ffn")(x, g, wgu, wd)


def _mixer_call(x, g, win, cw, wout, tiles_per_seq, prev=None):
    t = x.shape[0]
    n_tiles = t // TM
    is_sample = prev is not None
    in_specs = [_row_spec(D_MODEL), _const_spec((1, D_MODEL)),
                _const_spec((D_MODEL, 3 * D_MODEL)), _const_spec((CONV_W, D_MODEL)),
                _const_spec((D_MODEL, D_MODEL))]
    args = [x, g, win, cw, wout]
    if is_sample:
        in_specs += [_row_spec(D_MODEL), _row_spec(D_MODEL)]
        args += list(prev)
        u_spec = _row_spec(D_MODEL)
        u_shape = jax.ShapeDtypeStruct((t, D_MODEL), F32)
    else:
        u_spec = pl.BlockSpec((1, SUBLANES, D_MODEL), lambda i: (i, 0, 0))
        u_shape = jax.ShapeDtypeStruct((n_tiles, SUBLANES, D_MODEL), F32)
    return pl.pallas_call(
        functools.partial(_mixer_kernel, tiles_per_seq, is_sample), grid=(n_tiles,),
        in_specs=in_specs,
        out_specs=[_row_spec(D_MODEL), u_spec],
        out_shape=[jax.ShapeDtypeStruct((t, D_MODEL), F32), u_shape],
        scratch_shapes=[pltpu.VMEM((TM + SUBLANES, D_MODEL), F32)],
        compiler_params=_params(), name="mixer")(*args)


def _ffn_kv_call(x, g, wgu, wd, layer, gkv, wdkv, kvn, ga, gb, cos_t, sin_t, wuk, wuvt, gkn):
    t = x.shape[0]
    in_specs = ([_row_spec(D_MODEL)] + _ffn_specs(layer) + [
        _const_spec((1, D_MODEL)), _const_spec((D_MODEL, KV_RANK + 2 * LANES)),
        _const_spec((1, KV_RANK)), _const_spec((1, LANES)), _const_spec((1, LANES)),
        _table_spec(cos_t.shape[0]), _table_spec(sin_t.shape[0]),
        _const_spec((KV_RANK, D_MODEL)), _const_spec((D_MODEL, KV_RANK)),
        _const_spec((1, NOPE_DIM))])
    out_specs = [_row_spec(D_MODEL), _row_spec(KV_RANK), _row_spec(ROPE_DIM),
                 _row_spec(2 * N_HEADS * LANES),
                 pl.BlockSpec((N_HEADS * V_ROWS, TM), lambda i: (0, i))]
    out_shape = [jax.ShapeDtypeStruct((t, D_MODEL), F32),
                 jax.ShapeDtypeStruct((t, KV_RANK), F32),
                 jax.ShapeDtypeStruct((t, ROPE_DIM), F32),
                 jax.ShapeDtypeStruct((t, 2 * N_HEADS * LANES), BF16),
                 jax.ShapeDtypeStruct((N_HEADS * V_ROWS, t), BF16)]
    return pl.pallas_call(
        _ffn_kv_kernel, grid=(t // TM,), in_specs=in_specs, out_specs=out_specs,
        out_shape=out_shape, compiler_params=_params(), name="ffn_kv")(
            x, g, wgu, wd, gkv, wdkv, kvn, ga, gb, cos_t, sin_t, wuk, wuvt, gkn)


def _ffn_q_call(x, g, wgu, wd, layer, gmix, wdq, qnorm, wuq, gqn, ga, gb, cos_t, sin_t):
    t = x.shape[0]
    in_specs = ([_row_spec(D_MODEL)] + _ffn_specs(layer) + [
        _const_spec((1, D_MODEL)), _const_spec((D_MODEL, Q_RANK)),
        _const_spec((1, Q_RANK)), _const_spec((Q_RANK, 3 * N_HEADS * LANES)),
        _const_spec((1, NOPE_DIM)), _const_spec((1, LANES)), _const_spec((1, LANES)),
        _table_spec(cos_t.shape[0]), _table_spec(sin_t.shape[0])])
    return pl.pallas_call(
        _ffn_q_kernel, grid=(t // TM,), in_specs=in_specs,
        out_specs=[_row_spec(D_MODEL), _row_spec(2 * N_HEADS * LANES)],
        out_shape=[jax.ShapeDtypeStruct((t, D_MODEL), F32),
                   jax.ShapeDtypeStruct((t, 2 * N_HEADS * LANES), BF16)],
        compiler_params=_params(), name="ffn_q")(
            x, g, wgu, wd, gmix, wdq, qnorm, wuq, gqn, ga, gb, cos_t, sin_t)


def _out_ffn_call(x, a, wo, g, wgu, wd, layer):
    t = x.shape[0]
    return pl.pallas_call(
        _out_ffn_kernel, grid=(t // TM,),
        in_specs=[_row_spec(D_MODEL), _row_spec(D_MODEL),
                  _const_spec((D_MODEL, D_MODEL))] + _ffn_specs(layer),
        out_specs=_row_spec(D_MODEL),
        out_shape=jax.ShapeDtypeStruct((t, D_MODEL), F32),
        compiler_params=_params(), name="out_ffn")(x, a, wo, g, wgu, wd)


def _prompt_attn_call(q, k, vt, batch, seq):
    nq = seq // TQ
    hps = HEADS_PER_STEP
    resident = pl.Buffered(1)
    return pl.pallas_call(
        _prompt_attn_kernel, grid=(batch, N_HEADS // hps, nq),
        in_specs=[pl.BlockSpec((TQ, hps * 2 * LANES), lambda b, h, i: (b * nq + i, h)),
                  pl.BlockSpec((seq, hps * 2 * LANES), lambda b, h, i: (b, h),
                               pipeline_mode=resident),
                  pl.BlockSpec((hps * V_ROWS, seq), lambda b, h, i: (h, b),
                               pipeline_mode=resident)],
        out_specs=pl.BlockSpec((TQ, hps * V_DIM), lambda b, h, i: (b * nq + i, h)),
        out_shape=jax.ShapeDtypeStruct((batch * seq, N_HEADS * V_DIM), BF16),
        scratch_shapes=[pltpu.VMEM((hps, TQ, TQ), F32), pltpu.VMEM((hps, TQ, TQ), F32),
                        pltpu.VMEM((hps, 1, TQ), F32), pltpu.VMEM((hps, 1, TQ), F32),
                        pltpu.VMEM((hps, 1, TQ), F32), pltpu.VMEM((hps, V_ROWS, TQ), F32)],
        compiler_params=_params(3), name="prompt_attn")(q, k, vt)


def _sample_attn_call(page_table, qa, qn, kn_new, lat_new, kpe_new, wukt,
                      cache_latent, cache_kpe):
    n_seq, n_pages = page_table.shape
    n_q = qa.shape[1]
    n_keys = n_pages * PAGE_SIZE
    grid_spec = pltpu.PrefetchScalarGridSpec(
        num_scalar_prefetch=1, grid=(n_seq,),
        in_specs=[pl.BlockSpec((1, n_q, KV_RANK + LANES), lambda b, pt: (b, 0, 0)),
                  pl.BlockSpec((1, n_q, NOPE_DIM), lambda b, pt: (b, 0, 0)),
                  pl.BlockSpec((1, kn_new.shape[1], NOPE_DIM), lambda b, pt: (b, 0, 0)),
                  pl.BlockSpec((1, SUBLANES, KV_RANK), lambda b, pt: (b, 0, 0)),
                  pl.BlockSpec((1, SUBLANES, ROPE_DIM), lambda b, pt: (b, 0, 0)),
                  pl.BlockSpec((N_HEADS * NOPE_DIM, KV_RANK), lambda b, pt: (0, 0),
                               pipeline_mode=pl.Buffered(1)),
                  pl.BlockSpec(memory_space=pl.ANY),
                  pl.BlockSpec(memory_space=pl.ANY)],
        out_specs=pl.BlockSpec((1, n_q, KV_RANK), lambda b, pt: (b, 0, 0)),
        scratch_shapes=[pltpu.VMEM((2, n_keys, KV_RANK), F32),
                        pltpu.VMEM((2, ROPE_DIM, n_keys), F32),
                        pltpu.VMEM((N_HEADS * NOPE_DIM + n_q, KV_RANK), BF16),
                        pltpu.SemaphoreType.DMA((2, 2))])
    return pl.pallas_call(
        functools.partial(_sample_attn_kernel, n_pages), grid_spec=grid_spec,
        out_shape=jax.ShapeDtypeStruct((n_seq, n_q, KV_RANK), F32),
        compiler_params=_params(), name="sample_attn")(
            page_table.reshape(-1), qa, qn, kn_new, lat_new, kpe_new, wukt, cache_latent,
            jnp.swapaxes(cache_kpe, 1, 2))


def _absorb_call(q, gkn, wukt):
    t = q.shape[0]
    return pl.pallas_call(
        _absorb_kernel, grid=(t // TM,),
        in_specs=[_row_spec(2 * N_HEADS * LANES), _const_spec((1, NOPE_DIM)),
                  _const_spec((N_HEADS * NOPE_DIM, KV_RANK))],
        out_specs=_row_spec(N_HEADS * (KV_RANK + LANES)),
        out_shape=jax.ShapeDtypeStruct((t, N_HEADS * (KV_RANK + LANES)), BF16),
        compiler_params=_params(), name="absorb")(q, gkn, wukt)


def _ctx_out_call(ctx, wuv):
    t = ctx.shape[0]
    return pl.pallas_call(
        _ctx_out_kernel, grid=(t // TM,),
        in_specs=[_row_spec(N_HEADS * KV_RANK),
                  _const_spec((KV_RANK, N_HEADS * V_DIM))],
        out_specs=_row_spec(N_HEADS * V_DIM),
        out_shape=jax.ShapeDtypeStruct((t, N_HEADS * V_DIM), BF16),
        compiler_params=_params(), name="ctx_out")(ctx, wuv)


def _rope_tables(pos):
    half = ROPE_DIM // 2
    freqs = ROPE_THETA ** (-jnp.arange(half, dtype=F32) / half)
    ang = pos.astype(F32)[:, None] * freqs
    cos = jnp.cos(ang)
    sin = jnp.sin(ang)
    zero = jnp.zeros((pos.shape[0], LANES - ROPE_DIM), F32)
    return (jnp.concatenate([cos, cos, zero], axis=1),
            jnp.concatenate([-sin, sin, zero], axis=1))


def _swap_halves(a):
    half = a.shape[-1] // 2
    return jnp.concatenate([a[..., half:], a[..., :half]], axis=-1)


def _pad_lanes(a):
    pad = [(0, 0)] * (a.ndim - 1) + [(0, LANES - a.shape[-1])]
    return jnp.pad(a, pad)


def kernel(x_prompt, x_sample, cache_latent, cache_kpe, state_conv, page_table, norm_ffn1, w_ffn1_gu, w_ffn1_down, norm_mix, norm_ffn2, w_ffn2_gu, w_ffn2_down, w_conv_in, conv_w, w_conv_out, w_dq, q_norm, w_uq, q_nope_norm, q_pe_norm, w_o, norm_kv_in, w_dkv, kv_norm, k_pe_norm, w_uk, w_uv, k_nope_norm):
    batch, seq, _ = x_prompt.shape
    n_dec, dec_seq, _ = x_sample.shape
    n_pages = page_table.shape[1]
    assert seq % TM == 0 and (n_dec * dec_seq) % TM == 0 and dec_seq == 4

    row = lambda a: a.reshape(1, -1)
    ffn1 = (w_ffn1_gu.astype(BF16), w_ffn1_down.astype(BF16))
    ffn2 = (w_ffn2_gu.astype(BF16), w_ffn2_down.astype(BF16))
    win = w_conv_in[0].astype(BF16)
    wout = w_conv_out[0].astype(BF16)
    w_pe = w_dkv[:, KV_RANK:]
    wdkv = jnp.concatenate([w_dkv[:, :KV_RANK], _pad_lanes(w_pe),
                            _pad_lanes(_swap_halves(w_pe))], axis=1).astype(BF16)
    wuk = w_uk.astype(BF16)
    wuvt = w_uv.T.astype(BF16)
    wukt = w_uk.T.astype(BF16)
    wuv = w_uv.astype(BF16)
    wdq = w_dq[0].astype(BF16)
    wuq3 = w_uq[0].reshape(Q_RANK, N_HEADS, NOPE_DIM + ROPE_DIM)
    wq_pe = wuq3[:, :, NOPE_DIM:]
    wuq = jnp.concatenate([
        wuq3[:, :, :NOPE_DIM].reshape(Q_RANK, -1),
        _pad_lanes(wq_pe).reshape(Q_RANK, -1),
        _pad_lanes(_swap_halves(wq_pe)).reshape(Q_RANK, -1)], axis=1).astype(BF16)
    wo = w_o[0].astype(BF16)
    kga = row(_pad_lanes(k_pe_norm))
    kgb = row(_pad_lanes(_swap_halves(k_pe_norm)))
    qga = row(_pad_lanes(q_pe_norm[0]))
    qgb = row(_pad_lanes(_swap_halves(q_pe_norm[0])))

    cos_p, sin_p = _rope_tables(jnp.arange(seq))
    pos_s = n_pages * PAGE_SIZE + jnp.arange(n_dec * dec_seq) % dec_seq
    cos_s, sin_s = _rope_tables(pos_s)

    prev = state_conv[0]
    zrow = jnp.zeros((n_dec, 1, D_MODEL), F32)
    p1 = jnp.concatenate([prev[:, 1:2], zrow, zrow, zrow], axis=1).reshape(-1, D_MODEL)
    p2 = jnp.concatenate([prev[:, 0:1], prev[:, 1:2], zrow, zrow], axis=1).reshape(-1, D_MODEL)

    def dense(x, tiles_per_seq, cos_t, sin_t, prev_rows):
        x = _ffn_call(x, row(norm_ffn1[0]), *ffn1, 0)
        x, u = _mixer_call(x, row(norm_mix[0]), win, conv_w[0], wout, tiles_per_seq, prev_rows)
        x, lat, kpe, k, vt = _ffn_kv_call(
            x, row(norm_ffn2[0]), *ffn2, 0, row(norm_kv_in), wdkv, row(kv_norm),
            kga, kgb, cos_t, sin_t, wuk, wuvt, row(k_nope_norm))
        x, q = _ffn_q_call(
            x, row(norm_ffn1[1]), *ffn1, 1, row(norm_mix[1]), wdq, row(q_norm[0]), wuq,
            row(q_nope_norm[0]), qga, qgb, cos_t, sin_t)
        return x, u, lat, kpe, k, vt, q

    xp, up, lat_p, kpe_p, k_p, vt_p, q_p = dense(
        x_prompt.reshape(batch * seq, D_MODEL), seq // TM, cos_p, sin_p, None)
    o_p = _prompt_attn_call(q_p, k_p, vt_p, batch, seq)
    y_p = _out_ffn_call(xp, o_p, wo, row(norm_ffn2[1]), *ffn2, 1)
    conv_p = up.reshape(batch, seq // TM, SUBLANES, D_MODEL)[:, -1, SUBLANES - 2:][None]

    xs, us, lat_s, kpe_s, k_s, _, q_s = dense(
        x_sample.reshape(n_dec * dec_seq, D_MODEL), 1, cos_s, sin_s, (p1, p2))
    qa = _absorb_call(q_s, row(k_nope_norm), wukt)
    qa = qa.reshape(n_dec, dec_seq * N_HEADS, KV_RANK + LANES)
    pad_rows = ((0, 0), (0, SUBLANES - dec_seq), (0, 0))
    lat_new = jnp.pad(lat_s.reshape(n_dec, dec_seq, KV_RANK), pad_rows)
    kpe_new = jnp.pad(kpe_s.reshape(n_dec, dec_seq, ROPE_DIM), pad_rows)
    qn_s = q_s.reshape(n_dec, dec_seq * N_HEADS, 2 * LANES)[:, :, :NOPE_DIM]
    kn_new = k_s.reshape(n_dec, dec_seq * N_HEADS, 2 * LANES)[:, :, :NOPE_DIM]
    ctx = _sample_attn_call(page_table, qa, qn_s, kn_new, lat_new, kpe_new, wukt,
                            cache_latent, cache_kpe)
    o_s = _ctx_out_call(ctx.reshape(n_dec * dec_seq, N_HEADS * KV_RANK), wuv)
    y_s = _out_ffn_call(xs, o_s, wo, row(norm_ffn2[1]), *ffn2, 1)
    conv_s = us.reshape(n_dec, dec_seq, D_MODEL)[:, dec_seq - (CONV_W - 1):][None]

    return (y_p.reshape(batch, seq, D_MODEL),
            y_s.reshape(n_dec, dec_seq, D_MODEL),
            conv_p, conv_s,
            lat_p.reshape(batch, seq, KV_RANK),
            kpe_p.reshape(batch, seq, ROPE_DIM),
            lat_s.reshape(n_dec, dec_seq, KV_RANK),
            kpe_s.reshape(n_dec, dec_seq, ROPE_DIM))
```

```python
import functools

import jax
import jax.numpy as jnp
from jax import lax
from jax.experimental import pallas as pl
from jax.experimental.pallas import tpu as pltpu

F32 = jnp.float32
BF16 = jnp.bfloat16

D_MODEL = 1024
D_FF = 2816
N_HEADS = 8
NOPE_DIM = 128
ROPE_DIM = 64
V_DIM = 128
Q_RANK = 512
KV_RANK = 256
CONV_W = 3
PAGE_SIZE = 128
ROPE_THETA = 10000.0
EPS = 1e-6
SCALE = (NOPE_DIM + ROPE_DIM) ** -0.5
LOG2E = 1.4426950408889634
Q_SCALE = SCALE * LOG2E
V_ROWS = V_DIM + 16
NEG = -1e30

LANES = 128
SUBLANES = 8
TM = 512
FF_CHUNKS = 1
FC = D_FF // FF_CHUNKS
TQ = 1024
HEADS_PER_STEP = 2
KEY_CHUNK = 1024
VMEM_LIMIT = 56 * 1024 * 1024

_NT = (((1,), (1,)), ((), ()))


def _const_spec(shape):
    zeros = (0,) * len(shape)
    return pl.BlockSpec(shape, lambda *_: zeros, pipeline_mode=pl.Buffered(1))


def _rms(x, g):
    ms = jnp.mean(x * x, axis=-1, keepdims=True)
    return x * lax.rsqrt(ms + EPS) * g


def _dot(a, b):
    return jnp.dot(a, b, preferred_element_type=F32)


def _ffn(x, g, wgu_ref, wd_ref):
    xn = _rms(x, g).astype(BF16)
    acc = None
    for c in range(FF_CHUNKS):
        gate = _dot(xn, wgu_ref[:, c * FC:(c + 1) * FC])
        up = _dot(xn, wgu_ref[:, D_FF + c * FC:D_FF + (c + 1) * FC])
        act = (gate * jax.nn.sigmoid(gate) * up).astype(BF16)
        part = _dot(act, wd_ref[c * FC:(c + 1) * FC, :])
        acc = part if acc is None else acc + part
    return x + 0.5 * acc


def _rope_norm(a, b, ga, gb, cos_t, sin_t):
    ms = jnp.sum(a * a, axis=-1, keepdims=True) * (1.0 / ROPE_DIM)
    return (a * ga * cos_t + b * gb * sin_t) * lax.rsqrt(ms + EPS)


def _ffn_kernel(x_ref, g_ref, wgu_ref, wd_ref, o_ref):
    o_ref[...] = _ffn(x_ref[...], g_ref[...], wgu_ref, wd_ref)


def _mixer_kernel(tiles_per_seq, is_sample, *refs):
    if is_sample:
        (x_ref, g_ref, win_ref, cw_ref, wout_ref, p1_ref, p2_ref,
         o_ref, u_ref, ubuf) = refs
    else:
        x_ref, g_ref, win_ref, cw_ref, wout_ref, o_ref, u_ref, ubuf = refs
    i = pl.program_id(0)
    x = x_ref[...]
    xn = _rms(x, g_ref[...]).astype(BF16)
    bch = _dot(xn, win_ref[...])
    b = bch[:, :D_MODEL]
    u = bch[:, D_MODEL:2 * D_MODEL] * bch[:, 2 * D_MODEL:]

    @pl.when(i % tiles_per_seq == 0)
    def _():
        ubuf[0:SUBLANES, :] = jnp.zeros((SUBLANES, D_MODEL), F32)

    ubuf[SUBLANES:SUBLANES + TM, :] = u
    s1 = ubuf[SUBLANES - 1:SUBLANES - 1 + TM, :]
    s2 = ubuf[SUBLANES - 2:SUBLANES - 2 + TM, :]
    if is_sample:
        t = lax.broadcasted_iota(jnp.int32, (TM, 1), 0) % 4
        s1 = jnp.where(t >= 1, s1, p1_ref[...])
        s2 = jnp.where(t >= 2, s2, p2_ref[...])
    cw = cw_ref[...]
    y = cw[0:1, :] * s2 + cw[1:2, :] * s1 + cw[2:3, :] * u
    o_ref[...] = x + _dot((b * y).astype(BF16), wout_ref[...])
    if is_sample:
        u_ref[...] = u
    else:
        ubuf[0:SUBLANES, :] = u[TM - SUBLANES:, :]
        u_ref[0] = u[TM - SUBLANES:, :]


def _ffn_kv_kernel(x_ref, g_ref, wgu_ref, wd_ref, gkv_ref, wdkv_ref, kvn_ref,
                   ga_ref, gb_ref, cos_ref, sin_ref, wuk_ref, wuvt_ref, gkn_ref,
                   o_ref, lat_ref, kpe_ref, k_ref, vt_ref):
    x = _ffn(x_ref[...], g_ref[...], wgu_ref, wd_ref)
    o_ref[...] = x
    hk = _rms(x, gkv_ref[...]).astype(BF16)
    ckv = _dot(hk, wdkv_ref[...])
    lat = _rms(ckv[:, :KV_RANK], kvn_ref[...])
    lat_ref[...] = lat
    kpe = _rope_norm(ckv[:, KV_RANK:KV_RANK + LANES], ckv[:, KV_RANK + LANES:],
                     ga_ref[...], gb_ref[...], cos_ref[...], sin_ref[...])
    kpe_ref[...] = kpe[:, :ROPE_DIM]
    kpeb = kpe.astype(BF16)
    latb = lat.astype(BF16)
    ek = _dot(latb, wuk_ref[...])
    gkn = gkn_ref[...]
    for h in range(N_HEADS):
        e = ek[:, h * NOPE_DIM:(h + 1) * NOPE_DIM]
        k_ref[:, 2 * h * LANES:(2 * h + 1) * LANES] = _rms(e, gkn).astype(BF16)
        k_ref[:, (2 * h + 1) * LANES:(2 * h + 2) * LANES] = kpeb
    vt = lax.dot_general(wuvt_ref[...], latb, _NT, preferred_element_type=F32).astype(BF16)
    ones = jnp.ones((V_ROWS - V_DIM, vt.shape[1]), BF16)
    for h in range(N_HEADS):
        vt_ref[h * V_ROWS:h * V_ROWS + V_DIM, :] = vt[h * V_DIM:(h + 1) * V_DIM, :]
        vt_ref[h * V_ROWS + V_DIM:(h + 1) * V_ROWS, :] = ones


def _ffn_q_kernel(x_ref, g_ref, wgu_ref, wd_ref, gmix_ref, wdq_ref, qn_ref,
                  wuq_ref, gqn_ref, ga_ref, gb_ref, cos_ref, sin_ref,
                  o_ref, q_ref):
    x = _ffn(x_ref[...], g_ref[...], wgu_ref, wd_ref)
    o_ref[...] = x
    xn = _rms(x, gmix_ref[...]).astype(BF16)
    cq = _rms(_dot(xn, wdq_ref[...]), qn_ref[...]).astype(BF16)
    q = _dot(cq, wuq_ref[...])
    gqn = gqn_ref[...]
    ga = ga_ref[...]
    gb = gb_ref[...]
    cos_t = cos_ref[...]
    sin_t = sin_ref[...]
    hw = N_HEADS * LANES
    for h in range(N_HEADS):
        qn = _rms(q[:, h * LANES:(h + 1) * LANES], gqn) * Q_SCALE
        qp = _rope_norm(q[:, hw + h * LANES:hw + (h + 1) * LANES],
                        q[:, 2 * hw + h * LANES:2 * hw + (h + 1) * LANES],
                        ga, gb, cos_t, sin_t) * Q_SCALE
        q_ref[:, 2 * h * LANES:(2 * h + 1) * LANES] = qn.astype(BF16)
        q_ref[:, (2 * h + 1) * LANES:(2 * h + 2) * LANES] = qp.astype(BF16)


def _out_ffn_kernel(x_ref, a_ref, wo_ref, g_ref, wgu_ref, wd_ref, o_ref):
    x = x_ref[...] + _dot(a_ref[...], wo_ref[...])
    o_ref[...] = _ffn(x, g_ref[...], wgu_ref, wd_ref)


def _prompt_attn_kernel(q_ref, k_ref, vt_ref, o_ref,
                        sa_ref, sb_ref, smaxa_ref, smaxb_ref, m_scr, acc_scr):
    qi = pl.program_id(2)

    def scores(slot, ki, j, masked=False):
        s_ref, smax_ref = slot
        off = pl.multiple_of(ki * TQ, TQ)
        s = lax.dot_general(k_ref[pl.ds(off, TQ), 2 * j * LANES:2 * (j + 1) * LANES],
                            q_ref[:, 2 * j * LANES:2 * (j + 1) * LANES], _NT,
                            preferred_element_type=F32)
        if masked:
            key = lax.broadcasted_iota(jnp.int32, (TQ, TQ), 0)
            qry = lax.broadcasted_iota(jnp.int32, (TQ, TQ), 1)
            s = jnp.where(qry >= key, s, NEG)
        s_ref[j] = s
        smax_ref[j] = jnp.max(s, axis=0, keepdims=True)

    def accumulate(slot, ki, j):
        s_ref, smax_ref = slot
        off = pl.multiple_of(ki * TQ, TQ)
        m = m_scr[j]
        vt = vt_ref[j * V_ROWS:(j + 1) * V_ROWS, pl.ds(off, TQ)]
        m_new = jnp.maximum(m, smax_ref[j])
        alpha = jnp.exp2(m - m_new)
        p = jnp.exp2(s_ref[j] - m_new).astype(BF16)
        acc_scr[j] = alpha * acc_scr[j] + _dot(vt, p)
        m_scr[j] = m_new

    def advance(slot_next, ki_next, slot_cur, ki_cur):
        for j in range(HEADS_PER_STEP):
            scores(slot_next, ki_next, j)
            accumulate(slot_cur, ki_cur, j)

    for j in range(HEADS_PER_STEP):
        m_scr[j] = jnp.full((1, TQ), NEG, F32)
        acc_scr[j] = jnp.zeros((V_ROWS, TQ), F32)

    slot_a = (sa_ref, smaxa_ref)
    slot_b = (sb_ref, smaxb_ref)
    n_pairs = qi // 2
    for j in range(HEADS_PER_STEP):
        scores(slot_a, qi, j, masked=True)

    def pair(t, _):
        advance(slot_b, 2 * t, slot_a, jnp.where(t == 0, qi, 2 * t - 1))
        advance(slot_a, 2 * t + 1, slot_b, 2 * t)
        return 0

    lax.fori_loop(0, n_pairs, pair, 0)
    in_a = jnp.where(n_pairs == 0, qi, 2 * n_pairs - 1)

    @pl.when(qi % 2 == 1)
    def _():
        advance(slot_b, qi - 1, slot_a, in_a)
        for j in range(HEADS_PER_STEP):
            accumulate(slot_b, qi - 1, j)

    @pl.when(qi % 2 == 0)
    def _():
        for j in range(HEADS_PER_STEP):
            accumulate(slot_a, in_a, j)

    for j in range(HEADS_PER_STEP):
        acc = acc_scr[j]
        o = acc[:V_DIM] / acc[V_DIM:V_DIM + 1]
        o_ref[:, j * V_DIM:(j + 1) * V_DIM] = o.T.astype(BF16)


def _sample_attn_kernel(n_pages, pt_ref, qa_ref, qn_ref, knn_ref, latn_ref, kpen_ref,
                        wukt_ref, lat_hbm, kpet_hbm, ctx_ref,
                        latbuf, kpebuf, wq, sems):
    n_seq = pl.num_programs(0)
    b = pl.program_id(0)
    slot = b % 2
    n_past = n_pages * PAGE_SIZE
    n_chunks = n_past // KEY_CHUNK
    pages_per_chunk = n_pages // n_chunks
    n_w = N_HEADS * NOPE_DIM

    def page_copies(seq, slot_, j):
        pg = pt_ref[seq * n_pages + j]
        row = pl.multiple_of(j * PAGE_SIZE, PAGE_SIZE)
        return (
            pltpu.make_async_copy(lat_hbm.at[pg], latbuf.at[slot_, pl.ds(row, PAGE_SIZE)],
                                  sems.at[0, slot_]),
            pltpu.make_async_copy(kpet_hbm.at[pg], kpebuf.at[slot_, :, pl.ds(row, PAGE_SIZE)],
                                  sems.at[1, slot_]),
        )

    def start_pages(seq, slot_, first, count):
        for j in range(count):
            for cp in page_copies(seq, slot_, first + j):
                cp.start()

    def wait_fetch(slot_):
        pltpu.make_async_copy(latbuf.at[1 - slot_], latbuf.at[slot_], sems.at[0, slot_]).wait()
        pltpu.make_async_copy(kpebuf.at[1 - slot_], kpebuf.at[slot_], sems.at[1, slot_]).wait()

    @pl.when(b == 0)
    def _():
        def first_fetch(c, _):
            start_pages(0, 0, c * pages_per_chunk, pages_per_chunk)
            return 0
        lax.fori_loop(0, n_chunks, first_fetch, 0)
        wq[0:n_w, :] = wukt_ref[...]

    wait_fetch(slot)

    qa = qa_ref[0]
    n_q = qa.shape[0]
    wq[n_w:n_w + n_q, :] = qa[:, :KV_RANK]
    q_pe = qa[:, KV_RANK:KV_RANK + ROPE_DIM]
    wq_all = wq[...]

    def scores(latc, pe):
        n = latc.shape[0]
        e = lax.dot_general(wq_all, latc, _NT, preferred_element_type=F32)
        e3 = e[:n_w].reshape(N_HEADS, NOPE_DIM, n)
        r = lax.rsqrt(jnp.sum(e3 * e3, axis=1) * (1.0 / NOPE_DIM) + EPS)
        raw3 = e[n_w:].reshape(n_q // N_HEADS, N_HEADS, n) * r[None]
        return raw3.reshape(n_q, n) + pe

    def new_key_scores():
        kpec = jnp.concatenate(
            [kpen_ref[0], jnp.zeros((LANES - SUBLANES, ROPE_DIM), F32)], axis=0).astype(BF16)
        s_new = lax.dot_general(q_pe, kpec, _NT, preferred_element_type=F32)
        cross = lax.dot_general(qn_ref[0], knn_ref[0], _NT, preferred_element_type=F32)
        n_new = knn_ref.shape[1] // N_HEADS
        row = lax.broadcasted_iota(jnp.int32, cross.shape, 0)
        col = lax.broadcasted_iota(jnp.int32, cross.shape, 1)
        same_head = row % N_HEADS == col % N_HEADS
        t_q = lax.broadcasted_iota(jnp.int32, (n_q, LANES), 0) // N_HEADS
        j_k = lax.broadcasted_iota(jnp.int32, (n_q, LANES), 1)
        for j in range(n_new):
            pick = jnp.where(same_head & (col // N_HEADS == j), cross, 0.0)
            s_new = s_new + jnp.where(j_k == j, jnp.sum(pick, axis=-1, keepdims=True), 0.0)
        return jnp.where(j_k <= t_q, s_new, NEG)

    m = jnp.full((n_q, 1), NEG, F32)
    l = jnp.zeros((n_q, 1), F32)
    acc = jnp.zeros((n_q, KV_RANK), F32)
    pending = None
    for c in range(n_chunks):
        keys = slice(c * KEY_CHUNK, (c + 1) * KEY_CHUNK)
        latc = latbuf[slot, keys, :].astype(BF16)
        kpec = kpebuf[slot, :, keys].astype(BF16)
        s = scores(latc, _dot(q_pe, kpec))
        first = 0 if c == 0 else (c + 1) * pages_per_chunk
        count = 2 * pages_per_chunk if c == 0 else (pages_per_chunk if c + 1 < n_chunks else 0)
        start_pages((b + 1) % n_seq, 1 - slot, first, count)
        if c == 0:
            s_new = new_key_scores()
        if pending is not None:
            alpha, p_prev, lat_prev = pending
            acc = alpha * acc + _dot(p_prev, lat_prev)
        m_new = jnp.maximum(m, jnp.max(s, axis=-1, keepdims=True))
        alpha = jnp.exp2(m - m_new)
        p = jnp.exp2(s - m_new)
        l = alpha * l + jnp.sum(p, axis=-1, keepdims=True)
        m = m_new
        pending = (alpha, p.astype(BF16), latc)
    alpha, p_prev, lat_prev = pending
    acc = alpha * acc + _dot(p_prev, lat_prev)

    lat_new = jnp.concatenate(
        [latn_ref[0], jnp.zeros((LANES - SUBLANES, KV_RANK), F32)], axis=0).astype(BF16)
    m_new = jnp.maximum(m, jnp.max(s_new, axis=-1, keepdims=True))
    alpha = jnp.exp2(m - m_new)
    p = jnp.exp2(s_new - m_new)
    l = alpha * l + jnp.sum(p, axis=-1, keepdims=True)
    acc = alpha * acc + _dot(p.astype(BF16), lat_new)
    ctx_ref[0] = acc / l

    @pl.when(b == n_seq - 1)
    def _():
        wait_fetch(1 - slot)


def _absorb_kernel(q_ref, gkn_ref, wukt_ref, o_ref):
    gkn = gkn_ref[...]
    w = KV_RANK + LANES
    for h in range(N_HEADS):
        qn = (q_ref[:, 2 * h * LANES:(2 * h + 1) * LANES].astype(F32) * gkn).astype(BF16)
        o_ref[:, h * w:h * w + KV_RANK] = _dot(
            qn, wukt_ref[h * NOPE_DIM:(h + 1) * NOPE_DIM, :]).astype(BF16)
        o_ref[:, h * w + KV_RANK:(h + 1) * w] = q_ref[:, (2 * h + 1) * LANES:(2 * h + 2) * LANES]


def _ctx_out_kernel(c_ref, wuv_ref, o_ref):
    for h in range(N_HEADS):
        c = c_ref[:, h * KV_RANK:(h + 1) * KV_RANK].astype(BF16)
        o_ref[:, h * V_DIM:(h + 1) * V_DIM] = _dot(
            c, wuv_ref[:, h * V_DIM:(h + 1) * V_DIM]).astype(BF16)


def _params(n_axes=1):
    return pltpu.CompilerParams(dimension_semantics=("arbitrary",) * n_axes,
                                vmem_limit_bytes=VMEM_LIMIT)


def _row_spec(width):
    return pl.BlockSpec((TM, width), lambda i: (i, 0))


def _table_spec(n_rows):
    tiles = n_rows // TM
    return pl.BlockSpec((TM, LANES), lambda i: (i % tiles, 0))


def _ffn_specs(layer):
    pick = lambda *_: (layer, 0, 0)
    return [_const_spec((1, D_MODEL)),
            pl.BlockSpec((None, D_MODEL, 2 * D_FF), pick, pipeline_mode=pl.Buffered(1)),
            pl.BlockSpec((None, D_FF, D_MODEL), pick, pipeline_mode=pl.Buffered(1))]


def _ffn_call(x, g, wgu, wd, layer):
    t = x.shape[0]
    return pl.pallas_call(
        _ffn_kernel, grid=(t // TM,),
        in_specs=[_row_spec(D_MODEL)] + _ffn_specs(layer),
        out_specs=_row_spec(D_MODEL),
        out_shape=jax.ShapeDtypeStruct((t, D_MODEL), F32),
        compiler_params=_params(), name="ffn")(x, g, wgu, wd)


def _mixer_call(x, g, win, cw, wout, tiles_per_seq, prev=None):
    t = x.shape[0]
    n_tiles = t // TM
    is_sample = prev is not None
    in_specs = [_row_spec(D_MODEL), _const_spec((1, D_MODEL)),
                _const_spec((D_MODEL, 3 * D_MODEL)), _const_spec((CONV_W, D_MODEL)),
                _const_spec((D_MODEL, D_MODEL))]
    args = [x, g, win, cw, wout]
    if is_sample:
        in_specs += [_row_spec(D_MODEL), _row_spec(D_MODEL)]
        args += list(prev)
        u_spec = _row_spec(D_MODEL)
        u_shape = jax.ShapeDtypeStruct((t, D_MODEL), F32)
    else:
        u_spec = pl.BlockSpec((1, SUBLANES, D_MODEL), lambda i: (i, 0, 0))
        u_shape = jax.ShapeDtypeStruct((n_tiles, SUBLANES, D_MODEL), F32)
    return pl.pallas_call(
        functools.partial(_mixer_kernel, tiles_per_seq, is_sample), grid=(n_tiles,),
        in_specs=in_specs,
        out_specs=[_row_spec(D_MODEL), u_spec],
        out_shape=[jax.ShapeDtypeStruct((t, D_MODEL), F32), u_shape],
        scratch_shapes=[pltpu.VMEM((TM + SUBLANES, D_MODEL), F32)],
        compiler_params=_params(), name="mixer")(*args)


def _ffn_kv_call(x, g, wgu, wd, layer, gkv, wdkv, kvn, ga, gb, cos_t, sin_t, wuk, wuvt, gkn):
    t = x.shape[0]
    in_specs = ([_row_spec(D_MODEL)] + _ffn_specs(layer) + [
        _const_spec((1, D_MODEL)), _const_spec((D_MODEL, KV_RANK + 2 * LANES)),
        _const_spec((1, KV_RANK)), _const_spec((1, LANES)), _const_spec((1, LANES)),
        _table_spec(cos_t.shape[0]), _table_spec(sin_t.shape[0]),
        _const_spec((KV_RANK, D_MODEL)), _const_spec((D_MODEL, KV_RANK)),
        _const_spec((1, NOPE_DIM))])
    out_specs = [_row_spec(D_MODEL), _row_spec(KV_RANK), _row_spec(ROPE_DIM),
                 _row_spec(2 * N_HEADS * LANES),
                 pl.BlockSpec((N_HEADS * V_ROWS, TM), lambda i: (0, i))]
    out_shape = [jax.ShapeDtypeStruct((t, D_MODEL), F32),
                 jax.ShapeDtypeStruct((t, KV_RANK), F32),
                 jax.ShapeDtypeStruct((t, ROPE_DIM), F32),
                 jax.ShapeDtypeStruct((t, 2 * N_HEADS * LANES), BF16),
                 jax.ShapeDtypeStruct((N_HEADS * V_ROWS, t), BF16)]
    return pl.pallas_call(
        _ffn_kv_kernel, grid=(t // TM,), in_specs=in_specs, out_specs=out_specs,
        out_shape=out_shape, compiler_params=_params(), name="ffn_kv")(
            x, g, wgu, wd, gkv, wdkv, kvn, ga, gb, cos_t, sin_t, wuk, wuvt, gkn)


def _ffn_q_call(x, g, wgu, wd, layer, gmix, wdq, qnorm, wuq, gqn, ga, gb, cos_t, sin_t):
    t = x.shape[0]
    in_specs = ([_row_spec(D_MODEL)] + _ffn_specs(layer) + [
        _const_spec((1, D_MODEL)), _const_spec((D_MODEL, Q_RANK)),
        _const_spec((1, Q_RANK)), _const_spec((Q_RANK, 3 * N_HEADS * LANES)),
        _const_spec((1, NOPE_DIM)), _const_spec((1, LANES)), _const_spec((1, LANES)),
        _table_spec(cos_t.shape[0]), _table_spec(sin_t.shape[0])])
    return pl.pallas_call(
        _ffn_q_kernel, grid=(t // TM,), in_specs=in_specs,
        out_specs=[_row_spec(D_MODEL), _row_spec(2 * N_HEADS * LANES)],
        out_shape=[jax.ShapeDtypeStruct((t, D_MODEL), F32),
                   jax.ShapeDtypeStruct((t, 2 * N_HEADS * LANES), BF16)],
        compiler_params=_params(), name="ffn_q")(
            x, g, wgu, wd, gmix, wdq, qnorm, wuq, gqn, ga, gb, cos_t, sin_t)


def _out_ffn_call(x, a, wo, g, wgu, wd, layer):
    t = x.shape[0]
    return pl.pallas_call(
        _out_ffn_kernel, grid=(t // TM,),
        in_specs=[_row_spec(D_MODEL), _row_spec(D_MODEL),
                  _const_spec((D_MODEL, D_MODEL))] + _ffn_specs(layer),
        out_specs=_row_spec(D_MODEL),
        out_shape=jax.ShapeDtypeStruct((t, D_MODEL), F32),
        compiler_params=_params(), name="out_ffn")(x, a, wo, g, wgu, wd)


def _prompt_attn_call(q, k, vt, batch, seq):
    nq = seq // TQ
    hps = HEADS_PER_STEP
    return pl.pallas_call(
        _prompt_attn_kernel, grid=(batch, N_HEADS // hps, nq),
        in_specs=[pl.BlockSpec((TQ, hps * 2 * LANES), lambda b, h, i: (b * nq + i, h)),
                  pl.BlockSpec((seq, hps * 2 * LANES), lambda b, h, i: (b, h)),
                  pl.BlockSpec((hps * V_ROWS, seq), lambda b, h, i: (h, b))],
        out_specs=pl.BlockSpec((TQ, hps * V_DIM), lambda b, h, i: (b * nq + i, h)),
        out_shape=jax.ShapeDtypeStruct((batch * seq, N_HEADS * V_DIM), BF16),
        scratch_shapes=[pltpu.VMEM((hps, TQ, TQ), F32), pltpu.VMEM((hps, TQ, TQ), F32),
                        pltpu.VMEM((hps, 1, TQ), F32), pltpu.VMEM((hps, 1, TQ), F32),
                        pltpu.VMEM((hps, 1, TQ), F32), pltpu.VMEM((hps, V_ROWS, TQ), F32)],
        compiler_params=_params(3), name="prompt_attn")(q, k, vt)


def _sample_attn_call(page_table, qa, qn, kn_new, lat_new, kpe_new, wukt,
                      cache_latent, cache_kpe):
    n_seq, n_pages = page_table.shape
    n_q = qa.shape[1]
    n_keys = n_pages * PAGE_SIZE
    grid_spec = pltpu.PrefetchScalarGridSpec(
        num_scalar_prefetch=1, grid=(n_seq,),
        in_specs=[pl.BlockSpec((1, n_q, KV_RANK + LANES), lambda b, pt: (b, 0, 0)),
                  pl.BlockSpec((1, n_q, NOPE_DIM), lambda b, pt: (b, 0, 0)),
                  pl.BlockSpec((1, kn_new.shape[1], NOPE_DIM), lambda b, pt: (b, 0, 0)),
                  pl.BlockSpec((1, SUBLANES, KV_RANK), lambda b, pt: (b, 0, 0)),
                  pl.BlockSpec((1, SUBLANES, ROPE_DIM), lambda b, pt: (b, 0, 0)),
                  pl.BlockSpec((N_HEADS * NOPE_DIM, KV_RANK), lambda b, pt: (0, 0),
                               pipeline_mode=pl.Buffered(1)),
                  pl.BlockSpec(memory_space=pl.ANY),
                  pl.BlockSpec(memory_space=pl.ANY)],
        out_specs=pl.BlockSpec((1, n_q, KV_RANK), lambda b, pt: (b, 0, 0)),
        scratch_shapes=[pltpu.VMEM((2, n_keys, KV_RANK), F32),
                        pltpu.VMEM((2, ROPE_DIM, n_keys), F32),
                        pltpu.VMEM((N_HEADS * NOPE_DIM + n_q, KV_RANK), BF16),
                        pltpu.SemaphoreType.DMA((2, 2))])
    return pl.pallas_call(
        functools.partial(_sample_attn_kernel, n_pages), grid_spec=grid_spec,
        out_shape=jax.ShapeDtypeStruct((n_seq, n_q, KV_RANK), F32),
        compiler_params=_params(), name="sample_attn")(
            page_table.reshape(-1), qa, qn, kn_new, lat_new, kpe_new, wukt, cache_latent,
            jnp.swapaxes(cache_kpe, 1, 2))


def _absorb_call(q, gkn, wukt):
    t = q.shape[0]
    return pl.pallas_call(
        _absorb_kernel, grid=(t // TM,),
        in_specs=[_row_spec(2 * N_HEADS * LANES), _const_spec((1, NOPE_DIM)),
                  _const_spec((N_HEADS * NOPE_DIM, KV_RANK))],
        out_specs=_row_spec(N_HEADS * (KV_RANK + LANES)),
        out_shape=jax.ShapeDtypeStruct((t, N_HEADS * (KV_RANK + LANES)), BF16),
        compiler_params=_params(), name="absorb")(q, gkn, wukt)


def _ctx_out_call(ctx, wuv):
    t = ctx.shape[0]
    return pl.pallas_call(
        _ctx_out_kernel, grid=(t // TM,),
        in_specs=[_row_spec(N_HEADS * KV_RANK),
                  _const_spec((KV_RANK, N_HEADS * V_DIM))],
        out_specs=_row_spec(N_HEADS * V_DIM),
        out_shape=jax.ShapeDtypeStruct((t, N_HEADS * V_DIM), BF16),
        compiler_params=_params(), name="ctx_out")(ctx, wuv)


def _rope_tables(pos):
    half = ROPE_DIM // 2
    freqs = ROPE_THETA ** (-jnp.arange(half, dtype=F32) / half)
    ang = pos.astype(F32)[:, None] * freqs
    cos = jnp.cos(ang)
    sin = jnp.sin(ang)
    zero = jnp.zeros((pos.shape[0], LANES - ROPE_DIM), F32)
    return (jnp.concatenate([cos, cos, zero], axis=1),
            jnp.concatenate([-sin, sin, zero], axis=1))


def _swap_halves(a):
    half = a.shape[-1] // 2
    return jnp.concatenate([a[..., half:], a[..., :half]], axis=-1)


def _pad_lanes(a):
    pad = [(0, 0)] * (a.ndim - 1) + [(0, LANES - a.shape[-1])]
    return jnp.pad(a, pad)


def kernel(x_prompt, x_sample, cache_latent, cache_kpe, state_conv, page_table, norm_ffn1, w_ffn1_gu, w_ffn1_down, norm_mix, norm_ffn2, w_ffn2_gu, w_ffn2_down, w_conv_in, conv_w, w_conv_out, w_dq, q_norm, w_uq, q_nope_norm, q_pe_norm, w_o, norm_kv_in, w_dkv, kv_norm, k_pe_norm, w_uk, w_uv, k_nope_norm):
    batch, seq, _ = x_prompt.shape
    n_dec, dec_seq, _ = x_sample.shape
    n_pages = page_table.shape[1]
    assert seq % TM == 0 and (n_dec * dec_seq) % TM == 0 and dec_seq == 4

    row = lambda a: a.reshape(1, -1)
    ffn1 = (w_ffn1_gu.astype(BF16), w_ffn1_down.astype(BF16))
    ffn2 = (w_ffn2_gu.astype(BF16), w_ffn2_down.astype(BF16))
    win = w_conv_in[0].astype(BF16)
    wout = w_conv_out[0].astype(BF16)
    w_pe = w_dkv[:, KV_RANK:]
    wdkv = jnp.concatenate([w_dkv[:, :KV_RANK], _pad_lanes(w_pe),
                            _pad_lanes(_swap_halves(w_pe))], axis=1).astype(BF16)
    wuk = w_uk.astype(BF16)
    wuvt = w_uv.T.astype(BF16)
    wukt = w_uk.T.astype(BF16)
    wuv = w_uv.astype(BF16)
    wdq = w_dq[0].astype(BF16)
    wuq3 = w_uq[0].reshape(Q_RANK, N_HEADS, NOPE_DIM + ROPE_DIM)
    wq_pe = wuq3[:, :, NOPE_DIM:]
    wuq = jnp.concatenate([
        wuq3[:, :, :NOPE_DIM].reshape(Q_RANK, -1),
        _pad_lanes(wq_pe).reshape(Q_RANK, -1),
        _pad_lanes(_swap_halves(wq_pe)).reshape(Q_RANK, -1)], axis=1).astype(BF16)
    wo = w_o[0].astype(BF16)
    kga = row(_pad_lanes(k_pe_norm))
    kgb = row(_pad_lanes(_swap_halves(k_pe_norm)))
    qga = row(_pad_lanes(q_pe_norm[0]))
    qgb = row(_pad_lanes(_swap_halves(q_pe_norm[0])))

    cos_p, sin_p = _rope_tables(jnp.arange(seq))
    pos_s = n_pages * PAGE_SIZE + jnp.arange(n_dec * dec_seq) % dec_seq
    cos_s, sin_s = _rope_tables(pos_s)

    prev = state_conv[0]
    zrow = jnp.zeros((n_dec, 1, D_MODEL), F32)
    p1 = jnp.concatenate([prev[:, 1:2], zrow, zrow, zrow], axis=1).reshape(-1, D_MODEL)
    p2 = jnp.concatenate([prev[:, 0:1], prev[:, 1:2], zrow, zrow], axis=1).reshape(-1, D_MODEL)

    def dense(x, tiles_per_seq, cos_t, sin_t, prev_rows):
        x = _ffn_call(x, row(norm_ffn1[0]), *ffn1, 0)
        x, u = _mixer_call(x, row(norm_mix[0]), win, conv_w[0], wout, tiles_per_seq, prev_rows)
        x, lat, kpe, k, vt = _ffn_kv_call(
            x, row(norm_ffn2[0]), *ffn2, 0, row(norm_kv_in), wdkv, row(kv_norm),
            kga, kgb, cos_t, sin_t, wuk, wuvt, row(k_nope_norm))
        x, q = _ffn_q_call(
            x, row(norm_ffn1[1]), *ffn1, 1, row(norm_mix[1]), wdq, row(q_norm[0]), wuq,
            row(q_nope_norm[0]), qga, qgb, cos_t, sin_t)
        return x, u, lat, kpe, k, vt, q

    xp, up, lat_p, kpe_p, k_p, vt_p, q_p = dense(
        x_prompt.reshape(batch * seq, D_MODEL), seq // TM, cos_p, sin_p, None)
    o_p = _prompt_attn_call(q_p, k_p, vt_p, batch, seq)
    y_p = _out_ffn_call(xp, o_p, wo, row(norm_ffn2[1]), *ffn2, 1)
    conv_p = up.reshape(batch, seq // TM, SUBLANES, D_MODEL)[:, -1, SUBLANES - 2:][None]

    xs, us, lat_s, kpe_s, k_s, _, q_s = dense(
        x_sample.reshape(n_dec * dec_seq, D_MODEL), 1, cos_s, sin_s, (p1, p2))
    qa = _absorb_call(q_s, row(k_nope_norm), wukt)
    qa = qa.reshape(n_dec, dec_seq * N_HEADS, KV_RANK + LANES)
    pad_rows = ((0, 0), (0, SUBLANES - dec_seq), (0, 0))
    lat_new = jnp.pad(lat_s.reshape(n_dec, dec_seq, KV_RANK), pad_rows)
    kpe_new = jnp.pad(kpe_s.reshape(n_dec, dec_seq, ROPE_DIM), pad_rows)
    qn_s = q_s.reshape(n_dec, dec_seq * N_HEADS, 2 * LANES)[:, :, :NOPE_DIM]
    kn_new = k_s.reshape(n_dec, dec_seq * N_HEADS, 2 * LANES)[:, :, :NOPE_DIM]
    ctx = _sample_attn_call(page_table, qa, qn_s, kn_new, lat_new, kpe_new, wukt,
                            cache_latent, cache_kpe)
    o_s = _ctx_out_call(ctx.reshape(n_dec * dec_seq, N_HEADS * KV_RANK), wuv)
    y_s = _out_ffn_call(xs, o_s, wo, row(norm_ffn2[1]), *ffn2, 1)
    conv_s = us.reshape(n_dec, dec_seq, D_MODEL)[:, dec_seq - (CONV_W - 1):][None]

    return (y_p.reshape(batch, seq, D_MODEL),
            y_s.reshape(n_dec, dec_seq, D_MODEL),
            conv_p, conv_s,
            lat_p.reshape(batch, seq, KV_RANK),
            kpe_p.reshape(batch, seq, ROPE_DIM),
            lat_s.reshape(n_dec, dec_seq, KV_RANK),
            kpe_s.reshape(n_dec, dec_seq, ROPE_DIM))
```

```python
import functools

import jax
import jax.numpy as jnp
from jax import lax
from jax.experimental import pallas as pl
from jax.experimental.pallas import tpu as pltpu

F32 = jnp.float32
BF16 = jnp.bfloat16

D_MODEL = 1024
D_FF = 2816
N_HEADS = 8
NOPE_DIM = 128
ROPE_DIM = 64
V_DIM = 128
Q_RANK = 512
KV_RANK = 256
CONV_W = 3
PAGE_SIZE = 128
ROPE_THETA = 10000.0
EPS = 1e-6
SCALE = (NOPE_DIM + ROPE_DIM) ** -0.5
LOG2E = 1.4426950408889634
Q_SCALE = SCALE * LOG2E
V_ROWS = V_DIM + 16
NEG = -1e30

LANES = 128
SUBLANES = 8
TM = 512
FF_CHUNKS = 1
FC = D_FF // FF_CHUNKS
TQ = 1024
HEADS_PER_STEP = 2
KEY_CHUNK = 2048
VMEM_LIMIT = 56 * 1024 * 1024

_NT = (((1,), (1,)), ((), ()))


def _const_spec(shape):
    zeros = (0,) * len(shape)
    return pl.BlockSpec(shape, lambda *_: zeros, pipeline_mode=pl.Buffered(1))


def _rms(x, g):
    ms = jnp.mean(x * x, axis=-1, keepdims=True)
    return x * lax.rsqrt(ms + EPS) * g


def _dot(a, b):
    return jnp.dot(a, b, preferred_element_type=F32)


def _ffn(x, g, wgu_ref, wd_ref):
    xn = _rms(x, g).astype(BF16)
    acc = None
    for c in range(FF_CHUNKS):
        gate = _dot(xn, wgu_ref[:, c * FC:(c + 1) * FC])
        up = _dot(xn, wgu_ref[:, D_FF + c * FC:D_FF + (c + 1) * FC])
        act = (gate * jax.nn.sigmoid(gate) * up).astype(BF16)
        part = _dot(act, wd_ref[c * FC:(c + 1) * FC, :])
        acc = part if acc is None else acc + part
    return x + 0.5 * acc


def _rope_norm(a, b, ga, gb, cos_t, sin_t):
    ms = jnp.sum(a * a, axis=-1, keepdims=True) * (1.0 / ROPE_DIM)
    return (a * ga * cos_t + b * gb * sin_t) * lax.rsqrt(ms + EPS)


def _ffn_kernel(x_ref, g_ref, wgu_ref, wd_ref, o_ref):
    o_ref[...] = _ffn(x_ref[...], g_ref[...], wgu_ref, wd_ref)


def _mixer_kernel(tiles_per_seq, is_sample, *refs):
    if is_sample:
        (x_ref, g_ref, win_ref, cw_ref, wout_ref, p1_ref, p2_ref,
         o_ref, u_ref, ubuf) = refs
    else:
        x_ref, g_ref, win_ref, cw_ref, wout_ref, o_ref, u_ref, ubuf = refs
    i = pl.program_id(0)
    x = x_ref[...]
    xn = _rms(x, g_ref[...]).astype(BF16)
    bch = _dot(xn, win_ref[...])
    b = bch[:, :D_MODEL]
    u = bch[:, D_MODEL:2 * D_MODEL] * bch[:, 2 * D_MODEL:]

    @pl.when(i % tiles_per_seq == 0)
    def _():
        ubuf[0:SUBLANES, :] = jnp.zeros((SUBLANES, D_MODEL), F32)

    ubuf[SUBLANES:SUBLANES + TM, :] = u
    s1 = ubuf[SUBLANES - 1:SUBLANES - 1 + TM, :]
    s2 = ubuf[SUBLANES - 2:SUBLANES - 2 + TM, :]
    if is_sample:
        t = lax.broadcasted_iota(jnp.int32, (TM, 1), 0) % 4
        s1 = jnp.where(t >= 1, s1, p1_ref[...])
        s2 = jnp.where(t >= 2, s2, p2_ref[...])
    cw = cw_ref[...]
    y = cw[0:1, :] * s2 + cw[1:2, :] * s1 + cw[2:3, :] * u
    o_ref[...] = x + _dot((b * y).astype(BF16), wout_ref[...])
    if is_sample:
        u_ref[...] = u
    else:
        ubuf[0:SUBLANES, :] = u[TM - SUBLANES:, :]
        u_ref[0] = u[TM - SUBLANES:, :]


def _ffn_kv_kernel(x_ref, g_ref, wgu_ref, wd_ref, gkv_ref, wdkv_ref, kvn_ref,
                   ga_ref, gb_ref, cos_ref, sin_ref, wuk_ref, wuvt_ref, gkn_ref,
                   o_ref, lat_ref, kpe_ref, k_ref, vt_ref):
    x = _ffn(x_ref[...], g_ref[...], wgu_ref, wd_ref)
    o_ref[...] = x
    hk = _rms(x, gkv_ref[...]).astype(BF16)
    ckv = _dot(hk, wdkv_ref[...])
    lat = _rms(ckv[:, :KV_RANK], kvn_ref[...])
    lat_ref[...] = lat
    kpe = _rope_norm(ckv[:, KV_RANK:KV_RANK + LANES], ckv[:, KV_RANK + LANES:],
                     ga_ref[...], gb_ref[...], cos_ref[...], sin_ref[...])
    kpe_ref[...] = kpe[:, :ROPE_DIM]
    kpeb = kpe.astype(BF16)
    latb = lat.astype(BF16)
    ek = _dot(latb, wuk_ref[...])
    gkn = gkn_ref[...]
    for h in range(N_HEADS):
        e = ek[:, h * NOPE_DIM:(h + 1) * NOPE_DIM]
        k_ref[:, 2 * h * LANES:(2 * h + 1) * LANES] = _rms(e, gkn).astype(BF16)
        k_ref[:, (2 * h + 1) * LANES:(2 * h + 2) * LANES] = kpeb
    vt = lax.dot_general(wuvt_ref[...], latb, _NT, preferred_element_type=F32).astype(BF16)
    ones = jnp.ones((V_ROWS - V_DIM, vt.shape[1]), BF16)
    for h in range(N_HEADS):
        vt_ref[h * V_ROWS:h * V_ROWS + V_DIM, :] = vt[h * V_DIM:(h + 1) * V_DIM, :]
        vt_ref[h * V_ROWS + V_DIM:(h + 1) * V_ROWS, :] = ones


def _ffn_q_kernel(x_ref, g_ref, wgu_ref, wd_ref, gmix_ref, wdq_ref, qn_ref,
                  wuq_ref, gqn_ref, ga_ref, gb_ref, cos_ref, sin_ref,
                  o_ref, q_ref):
    x = _ffn(x_ref[...], g_ref[...], wgu_ref, wd_ref)
    o_ref[...] = x
    xn = _rms(x, gmix_ref[...]).astype(BF16)
    cq = _rms(_dot(xn, wdq_ref[...]), qn_ref[...]).astype(BF16)
    q = _dot(cq, wuq_ref[...])
    gqn = gqn_ref[...]
    ga = ga_ref[...]
    gb = gb_ref[...]
    cos_t = cos_ref[...]
    sin_t = sin_ref[...]
    hw = N_HEADS * LANES
    for h in range(N_HEADS):
        qn = _rms(q[:, h * LANES:(h + 1) * LANES], gqn) * Q_SCALE
        qp = _rope_norm(q[:, hw + h * LANES:hw + (h + 1) * LANES],
                        q[:, 2 * hw + h * LANES:2 * hw + (h + 1) * LANES],
                        ga, gb, cos_t, sin_t) * Q_SCALE
        q_ref[:, 2 * h * LANES:(2 * h + 1) * LANES] = qn.astype(BF16)
        q_ref[:, (2 * h + 1) * LANES:(2 * h + 2) * LANES] = qp.astype(BF16)


def _out_ffn_kernel(x_ref, a_ref, wo_ref, g_ref, wgu_ref, wd_ref, o_ref):
    x = x_ref[...] + _dot(a_ref[...], wo_ref[...])
    o_ref[...] = _ffn(x, g_ref[...], wgu_ref, wd_ref)


def _prompt_attn_kernel(q_ref, k_ref, vt_ref, o_ref,
                        sa_ref, sb_ref, smaxa_ref, smaxb_ref, m_scr, acc_scr):
    qi = pl.program_id(2)

    def scores(slot, ki, j, masked=False):
        s_ref, smax_ref = slot
        off = pl.multiple_of(ki * TQ, TQ)
        s = lax.dot_general(k_ref[pl.ds(off, TQ), 2 * j * LANES:2 * (j + 1) * LANES],
                            q_ref[:, 2 * j * LANES:2 * (j + 1) * LANES], _NT,
                            preferred_element_type=F32)
        if masked:
            key = lax.broadcasted_iota(jnp.int32, (TQ, TQ), 0)
            qry = lax.broadcasted_iota(jnp.int32, (TQ, TQ), 1)
            s = jnp.where(qry >= key, s, NEG)
        s_ref[j] = s
        smax_ref[j] = jnp.max(s, axis=0, keepdims=True)

    def accumulate(slot, ki, j):
        s_ref, smax_ref = slot
        off = pl.multiple_of(ki * TQ, TQ)
        m = m_scr[j]
        vt = vt_ref[j * V_ROWS:(j + 1) * V_ROWS, pl.ds(off, TQ)]
        m_new = jnp.maximum(m, smax_ref[j])
        alpha = jnp.exp2(m - m_new)
        p = jnp.exp2(s_ref[j] - m_new).astype(BF16)
        acc_scr[j] = alpha * acc_scr[j] + _dot(vt, p)
        m_scr[j] = m_new

    def advance(slot_next, ki_next, slot_cur, ki_cur):
        for j in range(HEADS_PER_STEP):
            scores(slot_next, ki_next, j)
            accumulate(slot_cur, ki_cur, j)

    for j in range(HEADS_PER_STEP):
        m_scr[j] = jnp.full((1, TQ), NEG, F32)
        acc_scr[j] = jnp.zeros((V_ROWS, TQ), F32)

    slot_a = (sa_ref, smaxa_ref)
    slot_b = (sb_ref, smaxb_ref)
    n_pairs = qi // 2
    for j in range(HEADS_PER_STEP):
        scores(slot_a, qi, j, masked=True)

    def pair(t, _):
        advance(slot_b, 2 * t, slot_a, jnp.where(t == 0, qi, 2 * t - 1))
        advance(slot_a, 2 * t + 1, slot_b, 2 * t)
        return 0

    lax.fori_loop(0, n_pairs, pair, 0)
    in_a = jnp.where(n_pairs == 0, qi, 2 * n_pairs - 1)

    @pl.when(qi % 2 == 1)
    def _():
        advance(slot_b, qi - 1, slot_a, in_a)
        for j in range(HEADS_PER_STEP):
            accumulate(slot_b, qi - 1, j)

    @pl.when(qi % 2 == 0)
    def _():
        for j in range(HEADS_PER_STEP):
            accumulate(slot_a, in_a, j)

    for j in range(HEADS_PER_STEP):
        acc = acc_scr[j]
        o = acc[:V_DIM] / acc[V_DIM:V_DIM + 1]
        o_ref[:, j * V_DIM:(j + 1) * V_DIM] = o.T.astype(BF16)


def _sample_attn_kernel(n_pages, pt_ref, qa_ref, qn_ref, knn_ref, latn_ref, kpen_ref,
                        wukt_ref, lat_hbm, kpet_hbm, ctx_ref,
                        latbuf, kpebuf, wq, sems):
    n_seq = pl.num_programs(0)
    b = pl.program_id(0)
    slot = b % 2
    n_past = n_pages * PAGE_SIZE
    n_chunks = n_past // KEY_CHUNK
    pages_per_chunk = n_pages // n_chunks
    n_w = N_HEADS * NOPE_DIM

    def page_copies(seq, slot_, j):
        pg = pt_ref[seq * n_pages + j]
        row = pl.multiple_of(j * PAGE_SIZE, PAGE_SIZE)
        return (
            pltpu.make_async_copy(lat_hbm.at[pg], latbuf.at[slot_, pl.ds(row, PAGE_SIZE)],
                                  sems.at[0, slot_]),
            pltpu.make_async_copy(kpet_hbm.at[pg], kpebuf.at[slot_, :, pl.ds(row, PAGE_SIZE)],
                                  sems.at[1, slot_]),
        )

    def start_pages(seq, slot_, first, count):
        for j in range(count):
            for cp in page_copies(seq, slot_, first + j):
                cp.start()

    def wait_fetch(slot_):
        pltpu.make_async_copy(latbuf.at[1 - slot_], latbuf.at[slot_], sems.at[0, slot_]).wait()
        pltpu.make_async_copy(kpebuf.at[1 - slot_], kpebuf.at[slot_], sems.at[1, slot_]).wait()

    @pl.when(b == 0)
    def _():
        def first_fetch(c, _):
            start_pages(0, 0, c * pages_per_chunk, pages_per_chunk)
            return 0
        lax.fori_loop(0, n_chunks, first_fetch, 0)
        wq[0:n_w, :] = wukt_ref[...]

    wait_fetch(slot)

    qa = qa_ref[0]
    n_q = qa.shape[0]
    wq[n_w:n_w + n_q, :] = qa[:, :KV_RANK]
    q_pe = qa[:, KV_RANK:KV_RANK + ROPE_DIM]
    wq_all = wq[...]

    def scores(latc, pe):
        n = latc.shape[0]
        e = lax.dot_general(wq_all, latc, _NT, preferred_element_type=F32)
        e3 = e[:n_w].reshape(N_HEADS, NOPE_DIM, n)
        r = lax.rsqrt(jnp.sum(e3 * e3, axis=1) * (1.0 / NOPE_DIM) + EPS)
        raw3 = e[n_w:].reshape(n_q // N_HEADS, N_HEADS, n) * r[None]
        return raw3.reshape(n_q, n) + pe

    def new_key_scores():
        kpec = jnp.concatenate(
            [kpen_ref[0], jnp.zeros((LANES - SUBLANES, ROPE_DIM), F32)], axis=0).astype(BF16)
        s_new = lax.dot_general(q_pe, kpec, _NT, preferred_element_type=F32)
        cross = lax.dot_general(qn_ref[0], knn_ref[0], _NT, preferred_element_type=F32)
        n_new = knn_ref.shape[1] // N_HEADS
        row = lax.broadcasted_iota(jnp.int32, cross.shape, 0)
        col = lax.broadcasted_iota(jnp.int32, cross.shape, 1)
        same_head = row % N_HEADS == col % N_HEADS
        t_q = lax.broadcasted_iota(jnp.int32, (n_q, LANES), 0) // N_HEADS
        j_k = lax.broadcasted_iota(jnp.int32, (n_q, LANES), 1)
        for j in range(n_new):
            pick = jnp.where(same_head & (col // N_HEADS == j), cross, 0.0)
            s_new = s_new + jnp.where(j_k == j, jnp.sum(pick, axis=-1, keepdims=True), 0.0)
        return jnp.where(j_k <= t_q, s_new, NEG)

    m = jnp.full((n_q, 1), NEG, F32)
    l = jnp.zeros((n_q, 1), F32)
    acc = jnp.zeros((n_q, KV_RANK), F32)
    pending = None
    for c in range(n_chunks):
        keys = slice(c * KEY_CHUNK, (c + 1) * KEY_CHUNK)
        latc = latbuf[slot, keys, :].astype(BF16)
        kpec = kpebuf[slot, :, keys].astype(BF16)
        s = scores(latc, _dot(q_pe, kpec))
        first = 0 if c == 0 else (c + 1) * pages_per_chunk
        count = 2 * pages_per_chunk if c == 0 else (pages_per_chunk if c + 1 < n_chunks else 0)
        start_pages((b + 1) % n_seq, 1 - slot, first, count)
        if c == 0:
            s_new = new_key_scores()
        if pending is not None:
            alpha, p_prev, lat_prev = pending
            acc = alpha * acc + _dot(p_prev, lat_prev)
        m_new = jnp.maximum(m, jnp.max(s, axis=-1, keepdims=True))
        alpha = jnp.exp2(m - m_new)
        p = jnp.exp2(s - m_new)
        l = alpha * l + jnp.sum(p, axis=-1, keepdims=True)
        m = m_new
        pending = (alpha, p.astype(BF16), latc)
    alpha, p_prev, lat_prev = pending
    acc = alpha * acc + _dot(p_prev, lat_prev)

    lat_new = jnp.concatenate(
        [latn_ref[0], jnp.zeros((LANES - SUBLANES, KV_RANK), F32)], axis=0).astype(BF16)
    m_new = jnp.maximum(m, jnp.max(s_new, axis=-1, keepdims=True))
    alpha = jnp.exp2(m - m_new)
    p = jnp.exp2(s_new - m_new)
    l = alpha * l + jnp.sum(p, axis=-1, keepdims=True)
    acc = alpha * acc + _dot(p.astype(BF16), lat_new)
    ctx_ref[0] = acc / l

    @pl.when(b == n_seq - 1)
    def _():
        wait_fetch(1 - slot)


def _absorb_kernel(q_ref, gkn_ref, wukt_ref, o_ref):
    gkn = gkn_ref[...]
    w = KV_RANK + LANES
    for h in range(N_HEADS):
        qn = (q_ref[:, 2 * h * LANES:(2 * h + 1) * LANES].astype(F32) * gkn).astype(BF16)
        o_ref[:, h * w:h * w + KV_RANK] = _dot(
            qn, wukt_ref[h * NOPE_DIM:(h + 1) * NOPE_DIM, :]).astype(BF16)
        o_ref[:, h * w + KV_RANK:(h + 1) * w] = q_ref[:, (2 * h + 1) * LANES:(2 * h + 2) * LANES]


def _ctx_out_kernel(c_ref, wuv_ref, o_ref):
    for h in range(N_HEADS):
        c = c_ref[:, h * KV_RANK:(h + 1) * KV_RANK].astype(BF16)
        o_ref[:, h * V_DIM:(h + 1) * V_DIM] = _dot(
            c, wuv_ref[:, h * V_DIM:(h + 1) * V_DIM]).astype(BF16)


def _params(n_axes=1):
    return pltpu.CompilerParams(dimension_semantics=("arbitrary",) * n_axes,
                                vmem_limit_bytes=VMEM_LIMIT)


def _row_spec(width):
    return pl.BlockSpec((TM, width), lambda i: (i, 0))


def _table_spec(n_rows):
    tiles = n_rows // TM
    return pl.BlockSpec((TM, LANES), lambda i: (i % tiles, 0))


def _ffn_specs(layer):
    pick = lambda *_: (layer, 0, 0)
    return [_const_spec((1, D_MODEL)),
            pl.BlockSpec((None, D_MODEL, 2 * D_FF), pick, pipeline_mode=pl.Buffered(1)),
            pl.BlockSpec((None, D_FF, D_MODEL), pick, pipeline_mode=pl.Buffered(1))]


def _ffn_call(x, g, wgu, wd, layer):
    t = x.shape[0]
    return pl.pallas_call(
        _ffn_kernel, grid=(t // TM,),
        in_specs=[_row_spec(D_MODEL)] + _ffn_specs(layer),
        out_specs=_row_spec(D_MODEL),
        out_shape=jax.ShapeDtypeStruct((t, D_MODEL), F32),
        compiler_params=_params(), name="ffn")(x, g, wgu, wd)


def _mixer_call(x, g, win, cw, wout, tiles_per_seq, prev=None):
    t = x.shape[0]
    n_tiles = t // TM
    is_sample = prev is not None
    in_specs = [_row_spec(D_MODEL), _const_spec((1, D_MODEL)),
                _const_spec((D_MODEL, 3 * D_MODEL)), _const_spec((CONV_W, D_MODEL)),
                _const_spec((D_MODEL, D_MODEL))]
    args = [x, g, win, cw, wout]
    if is_sample:
        in_specs += [_row_spec(D_MODEL), _row_spec(D_MODEL)]
        args += list(prev)
        u_spec = _row_spec(D_MODEL)
        u_shape = jax.ShapeDtypeStruct((t, D_MODEL), F32)
    else:
        u_spec = pl.BlockSpec((1, SUBLANES, D_MODEL), lambda i: (i, 0, 0))
        u_shape = jax.ShapeDtypeStruct((n_tiles, SUBLANES, D_MODEL), F32)
    return pl.pallas_call(
        functools.partial(_mixer_kernel, tiles_per_seq, is_sample), grid=(n_tiles,),
        in_specs=in_specs,
        out_specs=[_row_spec(D_MODEL), u_spec],
        out_shape=[jax.ShapeDtypeStruct((t, D_MODEL), F32), u_shape],
        scratch_shapes=[pltpu.VMEM((TM + SUBLANES, D_MODEL), F32)],
        compiler_params=_params(), name="mixer")(*args)


def _ffn_kv_call(x, g, wgu, wd, layer, gkv, wdkv, kvn, ga, gb, cos_t, sin_t, wuk, wuvt, gkn):
    t = x.shape[0]
    in_specs = ([_row_spec(D_MODEL)] + _ffn_specs(layer) + [
        _const_spec((1, D_MODEL)), _const_spec((D_MODEL, KV_RANK + 2 * LANES)),
        _const_spec((1, KV_RANK)), _const_spec((1, LANES)), _const_spec((1, LANES)),
        _table_spec(cos_t.shape[0]), _table_spec(sin_t.shape[0]),
        _const_spec((KV_RANK, D_MODEL)), _const_spec((D_MODEL, KV_RANK)),
        _const_spec((1, NOPE_DIM))])
    out_specs = [_row_spec(D_MODEL), _row_spec(KV_RANK), _row_spec(ROPE_DIM),
                 _row_spec(2 * N_HEADS * LANES),
                 pl.BlockSpec((N_HEADS * V_ROWS, TM), lambda i: (0, i))]
    out_shape = [jax.ShapeDtypeStruct((t, D_MODEL), F32),
                 jax.ShapeDtypeStruct((t, KV_RANK), F32),
                 jax.ShapeDtypeStruct((t, ROPE_DIM), F32),
                 jax.ShapeDtypeStruct((t, 2 * N_HEADS * LANES), BF16),
                 jax.ShapeDtypeStruct((N_HEADS * V_ROWS, t), BF16)]
    return pl.pallas_call(
        _ffn_kv_kernel, grid=(t // TM,), in_specs=in_specs, out_specs=out_specs,
        out_shape=out_shape, compiler_params=_params(), name="ffn_kv")(
            x, g, wgu, wd, gkv, wdkv, kvn, ga, gb, cos_t, sin_t, wuk, wuvt, gkn)


def _ffn_q_call(x, g, wgu, wd, layer, gmix, wdq, qnorm, wuq, gqn, ga, gb, cos_t, sin_t):
    t = x.shape[0]
    in_specs = ([_row_spec(D_MODEL)] + _ffn_specs(layer) + [
        _const_spec((1, D_MODEL)), _const_spec((D_MODEL, Q_RANK)),
        _const_spec((1, Q_RANK)), _const_spec((Q_RANK, 3 * N_HEADS * LANES)),
        _const_spec((1, NOPE_DIM)), _const_spec((1, LANES)), _const_spec((1, LANES)),
        _table_spec(cos_t.shape[0]), _table_spec(sin_t.shape[0])])
    return pl.pallas_call(
        _ffn_q_kernel, grid=(t // TM,), in_specs=in_specs,
        out_specs=[_row_spec(D_MODEL), _row_spec(2 * N_HEADS * LANES)],
        out_shape=[jax.ShapeDtypeStruct((t, D_MODEL), F32),
                   jax.ShapeDtypeStruct((t, 2 * N_HEADS * LANES), BF16)],
        compiler_params=_params(), name="ffn_q")(
            x, g, wgu, wd, gmix, wdq, qnorm, wuq, gqn, ga, gb, cos_t, sin_t)


def _out_ffn_call(x, a, wo, g, wgu, wd, layer):
    t = x.shape[0]
    return pl.pallas_call(
        _out_ffn_kernel, grid=(t // TM,),
        in_specs=[_row_spec(D_MODEL), _row_spec(D_MODEL),
                  _const_spec((D_MODEL, D_MODEL))] + _ffn_specs(layer),
        out_specs=_row_spec(D_MODEL),
        out_shape=jax.ShapeDtypeStruct((t, D_MODEL), F32),
        compiler_params=_params(), name="out_ffn")(x, a, wo, g, wgu, wd)


def _prompt_attn_call(q, k, vt, batch, seq):
    nq = seq // TQ
    hps = HEADS_PER_STEP
    return pl.pallas_call(
        _prompt_attn_kernel, grid=(batch, N_HEADS // hps, nq),
        in_specs=[pl.BlockSpec((TQ, hps * 2 * LANES), lambda b, h, i: (b * nq + i, h)),
                  pl.BlockSpec((seq, hps * 2 * LANES), lambda b, h, i: (b, h)),
                  pl.BlockSpec((hps * V_ROWS, seq), lambda b, h, i: (h, b))],
        out_specs=pl.BlockSpec((TQ, hps * V_DIM), lambda b, h, i: (b * nq + i, h)),
        out_shape=jax.ShapeDtypeStruct((batch * seq, N_HEADS * V_DIM), BF16),
        scratch_shapes=[pltpu.VMEM((hps, TQ, TQ), F32), pltpu.VMEM((hps, TQ, TQ), F32),
                        pltpu.VMEM((hps, 1, TQ), F32), pltpu.VMEM((hps, 1, TQ), F32),
                        pltpu.VMEM((hps, 1, TQ), F32), pltpu.VMEM((hps, V_ROWS, TQ), F32)],
        compiler_params=_params(3), name="prompt_attn")(q, k, vt)


def _sample_attn_call(page_table, qa, qn, kn_new, lat_new, kpe_new, wukt,
                      cache_latent, cache_kpe):
    n_seq, n_pages = page_table.shape
    n_q = qa.shape[1]
    n_keys = n_pages * PAGE_SIZE
    grid_spec = pltpu.PrefetchScalarGridSpec(
        num_scalar_prefetch=1, grid=(n_seq,),
        in_specs=[pl.BlockSpec((1, n_q, KV_RANK + LANES), lambda b, pt: (b, 0, 0)),
                  pl.BlockSpec((1, n_q, NOPE_DIM), lambda b, pt: (b, 0, 0)),
                  pl.BlockSpec((1, kn_new.shape[1], NOPE_DIM), lambda b, pt: (b, 0, 0)),
                  pl.BlockSpec((1, SUBLANES, KV_RANK), lambda b, pt: (b, 0, 0)),
                  pl.BlockSpec((1, SUBLANES, ROPE_DIM), lambda b, pt: (b, 0, 0)),
                  pl.BlockSpec((N_HEADS * NOPE_DIM, KV_RANK), lambda b, pt: (0, 0),
                               pipeline_mode=pl.Buffered(1)),
                  pl.BlockSpec(memory_space=pl.ANY),
                  pl.BlockSpec(memory_space=pl.ANY)],
        out_specs=pl.BlockSpec((1, n_q, KV_RANK), lambda b, pt: (b, 0, 0)),
        scratch_shapes=[pltpu.VMEM((2, n_keys, KV_RANK), F32),
                        pltpu.VMEM((2, ROPE_DIM, n_keys), F32),
                        pltpu.VMEM((N_HEADS * NOPE_DIM + n_q, KV_RANK), BF16),
                        pltpu.SemaphoreType.DMA((2, 2))])
    return pl.pallas_call(
        functools.partial(_sample_attn_kernel, n_pages), grid_spec=grid_spec,
        out_shape=jax.ShapeDtypeStruct((n_seq, n_q, KV_RANK), F32),
        compiler_params=_params(), name="sample_attn")(
            page_table.reshape(-1), qa, qn, kn_new, lat_new, kpe_new, wukt, cache_latent,
            jnp.swapaxes(cache_kpe, 1, 2))


def _absorb_call(q, gkn, wukt):
    t = q.shape[0]
    return pl.pallas_call(
        _absorb_kernel, grid=(t // TM,),
        in_specs=[_row_spec(2 * N_HEADS * LANES), _const_spec((1, NOPE_DIM)),
                  _const_spec((N_HEADS * NOPE_DIM, KV_RANK))],
        out_specs=_row_spec(N_HEADS * (KV_RANK + LANES)),
        out_shape=jax.ShapeDtypeStruct((t, N_HEADS * (KV_RANK + LANES)), BF16),
        compiler_params=_params(), name="absorb")(q, gkn, wukt)


def _ctx_out_call(ctx, wuv):
    t = ctx.shape[0]
    return pl.pallas_call(
        _ctx_out_kernel, grid=(t // TM,),
        in_specs=[_row_spec(N_HEADS * KV_RANK),
                  _const_spec((KV_RANK, N_HEADS * V_DIM))],
        out_specs=_row_spec(N_HEADS * V_DIM),
        out_shape=jax.ShapeDtypeStruct((t, N_HEADS * V_DIM), BF16),
        compiler_params=_params(), name="ctx_out")(ctx, wuv)


def _rope_tables(pos):
    half = ROPE_DIM // 2
    freqs = ROPE_THETA ** (-jnp.arange(half, dtype=F32) / half)
    ang = pos.astype(F32)[:, None] * freqs
    cos = jnp.cos(ang)
    sin = jnp.sin(ang)
    zero = jnp.zeros((pos.shape[0], LANES - ROPE_DIM), F32)
    return (jnp.concatenate([cos, cos, zero], axis=1),
            jnp.concatenate([-sin, sin, zero], axis=1))


def _swap_halves(a):
    half = a.shape[-1] // 2
    return jnp.concatenate([a[..., half:], a[..., :half]], axis=-1)


def _pad_lanes(a):
    pad = [(0, 0)] * (a.ndim - 1) + [(0, LANES - a.shape[-1])]
    return jnp.pad(a, pad)


def kernel(x_prompt, x_sample, cache_latent, cache_kpe, state_conv, page_table, norm_ffn1, w_ffn1_gu, w_ffn1_down, norm_mix, norm_ffn2, w_ffn2_gu, w_ffn2_down, w_conv_in, conv_w, w_conv_out, w_dq, q_norm, w_uq, q_nope_norm, q_pe_norm, w_o, norm_kv_in, w_dkv, kv_norm, k_pe_norm, w_uk, w_uv, k_nope_norm):
    batch, seq, _ = x_prompt.shape
    n_dec, dec_seq, _ = x_sample.shape
    n_pages = page_table.shape[1]
    assert seq % TM == 0 and (n_dec * dec_seq) % TM == 0 and dec_seq == 4

    row = lambda a: a.reshape(1, -1)
    ffn1 = (w_ffn1_gu.astype(BF16), w_ffn1_down.astype(BF16))
    ffn2 = (w_ffn2_gu.astype(BF16), w_ffn2_down.astype(BF16))
    win = w_conv_in[0].astype(BF16)
    wout = w_conv_out[0].astype(BF16)
    w_pe = w_dkv[:, KV_RANK:]
    wdkv = jnp.concatenate([w_dkv[:, :KV_RANK], _pad_lanes(w_pe),
                            _pad_lanes(_swap_halves(w_pe))], axis=1).astype(BF16)
    wuk = w_uk.astype(BF16)
    wuvt = w_uv.T.astype(BF16)
    wukt = w_uk.T.astype(BF16)
    wuv = w_uv.astype(BF16)
    wdq = w_dq[0].astype(BF16)
    wuq3 = w_uq[0].reshape(Q_RANK, N_HEADS, NOPE_DIM + ROPE_DIM)
    wq_pe = wuq3[:, :, NOPE_DIM:]
    wuq = jnp.concatenate([
        wuq3[:, :, :NOPE_DIM].reshape(Q_RANK, -1),
        _pad_lanes(wq_pe).reshape(Q_RANK, -1),
        _pad_lanes(_swap_halves(wq_pe)).reshape(Q_RANK, -1)], axis=1).astype(BF16)
    wo = w_o[0].astype(BF16)
    kga = row(_pad_lanes(k_pe_norm))
    kgb = row(_pad_lanes(_swap_halves(k_pe_norm)))
    qga = row(_pad_lanes(q_pe_norm[0]))
    qgb = row(_pad_lanes(_swap_halves(q_pe_norm[0])))

    cos_p, sin_p = _rope_tables(jnp.arange(seq))
    pos_s = n_pages * PAGE_SIZE + jnp.arange(n_dec * dec_seq) % dec_seq
    cos_s, sin_s = _rope_tables(pos_s)

    prev = state_conv[0]
    zrow = jnp.zeros((n_dec, 1, D_MODEL), F32)
    p1 = jnp.concatenate([prev[:, 1:2], zrow, zrow, zrow], axis=1).reshape(-1, D_MODEL)
    p2 = jnp.concatenate([prev[:, 0:1], prev[:, 1:2], zrow, zrow], axis=1).reshape(-1, D_MODEL)

    def dense(x, tiles_per_seq, cos_t, sin_t, prev_rows):
        x = _ffn_call(x, row(norm_ffn1[0]), *ffn1, 0)
        x, u = _mixer_call(x, row(norm_mix[0]), win, conv_w[0], wout, tiles_per_seq, prev_rows)
        x, lat, kpe, k, vt = _ffn_kv_call(
            x, row(norm_ffn2[0]), *ffn2, 0, row(norm_kv_in), wdkv, row(kv_norm),
            kga, kgb, cos_t, sin_t, wuk, wuvt, row(k_nope_norm))
        x, q = _ffn_q_call(
            x, row(norm_ffn1[1]), *ffn1, 1, row(norm_mix[1]), wdq, row(q_norm[0]), wuq,
            row(q_nope_norm[0]), qga, qgb, cos_t, sin_t)
        return x, u, lat, kpe, k, vt, q

    xp, up, lat_p, kpe_p, k_p, vt_p, q_p = dense(
        x_prompt.reshape(batch * seq, D_MODEL), seq // TM, cos_p, sin_p, None)
    o_p = _prompt_attn_call(q_p, k_p, vt_p, batch, seq)
    y_p = _out_ffn_call(xp, o_p, wo, row(norm_ffn2[1]), *ffn2, 1)
    conv_p = up.reshape(batch, seq // TM, SUBLANES, D_MODEL)[:, -1, SUBLANES - 2:][None]

    xs, us, lat_s, kpe_s, k_s, _, q_s = dense(
        x_sample.reshape(n_dec * dec_seq, D_MODEL), 1, cos_s, sin_s, (p1, p2))
    qa = _absorb_call(q_s, row(k_nope_norm), wukt)
    qa = qa.reshape(n_dec, dec_seq * N_HEADS, KV_RANK + LANES)
    pad_rows = ((0, 0), (0, SUBLANES - dec_seq), (0, 0))
    lat_new = jnp.pad(lat_s.reshape(n_dec, dec_seq, KV_RANK), pad_rows)
    kpe_new = jnp.pad(kpe_s.reshape(n_dec, dec_seq, ROPE_DIM), pad_rows)
    qn_s = q_s.reshape(n_dec, dec_seq * N_HEADS, 2 * LANES)[:, :, :NOPE_DIM]
    kn_new = k_s.reshape(n_dec, dec_seq * N_HEADS, 2 * LANES)[:, :, :NOPE_DIM]
    ctx = _sample_attn_call(page_table, qa, qn_s, kn_new, lat_new, kpe_new, wukt,
                            cache_latent, cache_kpe)
    o_s = _ctx_out_call(ctx.reshape(n_dec * dec_seq, N_HEADS * KV_RANK), wuv)
    y_s = _out_ffn_call(xs, o_s, wo, row(norm_ffn2[1]), *ffn2, 1)
    conv_s = us.reshape(n_dec, dec_seq, D_MODEL)[:, dec_seq - (CONV_W - 1):][None]

    return (y_p.reshape(batch, seq, D_MODEL),
            y_s.reshape(n_dec, dec_seq, D_MODEL),
            conv_p, conv_s,
            lat_p.reshape(batch, seq, KV_RANK),
            kpe_p.reshape(batch, seq, ROPE_DIM),
            lat_s.reshape(n_dec, dec_seq, KV_RANK),
            kpe_s.reshape(n_dec, dec_seq, ROPE_DIM))
```

```python
import functools

import jax
import jax.numpy as jnp
from jax import lax
from jax.experimental import pallas as pl
from jax.experimental.pallas import tpu as pltpu

F32 = jnp.float32
BF16 = jnp.bfloat16

D_MODEL = 1024
D_FF = 2816
N_HEADS = 8
NOPE_DIM = 128
ROPE_DIM = 64
V_DIM = 128
Q_RANK = 512
KV_RANK = 256
CONV_W = 3
PAGE_SIZE = 128
ROPE_THETA = 10000.0
EPS = 1e-6
SCALE = (NOPE_DIM + ROPE_DIM) ** -0.5
LOG2E = 1.4426950408889634
Q_SCALE = SCALE * LOG2E
NEG = -1e30
DEC_SEQ = 4

LANES = 128
SUBLANES = 8
BF16_SUBLANES = 16
V_ROWS = V_DIM + BF16_SUBLANES
TM = 512
TQ = 1024
HEADS_PER_STEP = 2
KEY_CHUNK = 2048
VMEM_LIMIT = 56 * 1024 * 1024

_NT = (((1,), (1,)), ((), ()))


def _const_spec(shape):
    zeros = (0,) * len(shape)
    return pl.BlockSpec(shape, lambda *_: zeros, pipeline_mode=pl.Buffered(1))


def _rms(x, g):
    ms = jnp.mean(x * x, axis=-1, keepdims=True)
    return x * lax.rsqrt(ms + EPS) * g


def _dot(a, b):
    return jnp.dot(a, b, preferred_element_type=F32)


def _ffn(x, g, wgu_ref, wd_ref):
    xn = _rms(x, g).astype(BF16)
    gate = _dot(xn, wgu_ref[:, :D_FF])
    up = _dot(xn, wgu_ref[:, D_FF:])
    act = (gate * jax.nn.sigmoid(gate) * up).astype(BF16)
    return x + 0.5 * _dot(act, wd_ref[...])


def _rope_norm(a, b, ga, gb, cos_t, sin_t):
    ms = jnp.sum(a * a, axis=-1, keepdims=True) * (1.0 / ROPE_DIM)
    return (a * ga * cos_t + b * gb * sin_t) * lax.rsqrt(ms + EPS)


def _ffn_kernel(x_ref, g_ref, wgu_ref, wd_ref, o_ref):
    o_ref[...] = _ffn(x_ref[...], g_ref[...], wgu_ref, wd_ref)


def _mixer_kernel(tiles_per_seq, is_sample, *refs):
    if is_sample:
        (x_ref, g_ref, win_ref, cw_ref, wout_ref, p1_ref, p2_ref,
         o_ref, u_ref, ubuf) = refs
    else:
        x_ref, g_ref, win_ref, cw_ref, wout_ref, o_ref, u_ref, ubuf = refs
    i = pl.program_id(0)
    x = x_ref[...]
    xn = _rms(x, g_ref[...]).astype(BF16)
    bch = _dot(xn, win_ref[...])
    b = bch[:, :D_MODEL]
    u = bch[:, D_MODEL:2 * D_MODEL] * bch[:, 2 * D_MODEL:]

    @pl.when(i % tiles_per_seq == 0)
    def _():
        ubuf[0:SUBLANES, :] = jnp.zeros((SUBLANES, D_MODEL), F32)

    ubuf[SUBLANES:SUBLANES + TM, :] = u
    s1 = ubuf[SUBLANES - 1:SUBLANES - 1 + TM, :]
    s2 = ubuf[SUBLANES - 2:SUBLANES - 2 + TM, :]
    if is_sample:
        t = lax.broadcasted_iota(jnp.int32, (TM, 1), 0) % DEC_SEQ
        s1 = jnp.where(t >= 1, s1, p1_ref[...])
        s2 = jnp.where(t >= 2, s2, p2_ref[...])
    cw = cw_ref[...]
    y = cw[0:1, :] * s2 + cw[1:2, :] * s1 + cw[2:3, :] * u
    o_ref[...] = x + _dot((b * y).astype(BF16), wout_ref[...])
    if is_sample:
        u_ref[...] = u
    else:
        ubuf[0:SUBLANES, :] = u[TM - SUBLANES:, :]
        u_ref[0] = u[TM - SUBLANES:, :]


def _ffn_kv_kernel(x_ref, g_ref, wgu_ref, wd_ref, gkv_ref, wdkv_ref, kvn_ref,
                   ga_ref, gb_ref, cos_ref, sin_ref, wuk_ref, wuvt_ref, gkn_ref,
                   o_ref, lat_ref, kpe_ref, k_ref, vt_ref):
    x = _ffn(x_ref[...], g_ref[...], wgu_ref, wd_ref)
    o_ref[...] = x
    hk = _rms(x, gkv_ref[...]).astype(BF16)
    ckv = _dot(hk, wdkv_ref[...])
    lat = _rms(ckv[:, :KV_RANK], kvn_ref[...])
    lat_ref[...] = lat
    kpe = _rope_norm(ckv[:, KV_RANK:KV_RANK + LANES], ckv[:, KV_RANK + LANES:],
                     ga_ref[...], gb_ref[...], cos_ref[...], sin_ref[...])
    kpe_ref[...] = kpe[:, :ROPE_DIM]
    kpeb = kpe.astype(BF16)
    latb = lat.astype(BF16)
    ek = _dot(latb, wuk_ref[...])
    gkn = gkn_ref[...]
    for h in range(N_HEADS):
        e = ek[:, h * NOPE_DIM:(h + 1) * NOPE_DIM]
        k_ref[:, 2 * h * LANES:(2 * h + 1) * LANES] = _rms(e, gkn).astype(BF16)
        k_ref[:, (2 * h + 1) * LANES:(2 * h + 2) * LANES] = kpeb
    vt = lax.dot_general(wuvt_ref[...], latb, _NT, preferred_element_type=F32).astype(BF16)
    ones = jnp.ones((V_ROWS - V_DIM, vt.shape[1]), BF16)
    for h in range(N_HEADS):
        vt_ref[h * V_ROWS:h * V_ROWS + V_DIM, :] = vt[h * V_DIM:(h + 1) * V_DIM, :]
        vt_ref[h * V_ROWS + V_DIM:(h + 1) * V_ROWS, :] = ones


def _ffn_q_kernel(x_ref, g_ref, wgu_ref, wd_ref, gmix_ref, wdq_ref, qn_ref,
                  wuq_ref, gqn_ref, ga_ref, gb_ref, cos_ref, sin_ref,
                  o_ref, q_ref):
    x = _ffn(x_ref[...], g_ref[...], wgu_ref, wd_ref)
    o_ref[...] = x
    xn = _rms(x, gmix_ref[...]).astype(BF16)
    cq = _rms(_dot(xn, wdq_ref[...]), qn_ref[...]).astype(BF16)
    q = _dot(cq, wuq_ref[...])
    gqn = gqn_ref[...]
    ga = ga_ref[...]
    gb = gb_ref[...]
    cos_t = cos_ref[...]
    sin_t = sin_ref[...]
    hw = N_HEADS * LANES
    for h in range(N_HEADS):
        qn = _rms(q[:, h * LANES:(h + 1) * LANES], gqn) * Q_SCALE
        qp = _rope_norm(q[:, hw + h * LANES:hw + (h + 1) * LANES],
                        q[:, 2 * hw + h * LANES:2 * hw + (h + 1) * LANES],
                        ga, gb, cos_t, sin_t) * Q_SCALE
        q_ref[:, 2 * h * LANES:(2 * h + 1) * LANES] = qn.astype(BF16)
        q_ref[:, (2 * h + 1) * LANES:(2 * h + 2) * LANES] = qp.astype(BF16)


def _out_ffn_kernel(x_ref, a_ref, wo_ref, g_ref, wgu_ref, wd_ref, o_ref):
    x = x_ref[...] + _dot(a_ref[...], wo_ref[...])
    o_ref[...] = _ffn(x, g_ref[...], wgu_ref, wd_ref)


def _prompt_attn_kernel(q_ref, k_ref, vt_ref, o_ref,
                        sa_ref, sb_ref, smaxa_ref, smaxb_ref, m_scr, acc_scr):
    qi = pl.program_id(2)

    def scores(slot, ki, j, masked=False):
        s_ref, smax_ref = slot
        off = pl.multiple_of(ki * TQ, TQ)
        s = lax.dot_general(k_ref[pl.ds(off, TQ), 2 * j * LANES:2 * (j + 1) * LANES],
                            q_ref[:, 2 * j * LANES:2 * (j + 1) * LANES], _NT,
                            preferred_element_type=F32)
        if masked:
            key = lax.broadcasted_iota(jnp.int32, (TQ, TQ), 0)
            qry = lax.broadcasted_iota(jnp.int32, (TQ, TQ), 1)
            s = jnp.where(qry >= key, s, NEG)
        s_ref[j] = s
        smax_ref[j] = jnp.max(s, axis=0, keepdims=True)

    def accumulate(slot, ki, j):
        s_ref, smax_ref = slot
        off = pl.multiple_of(ki * TQ, TQ)
        m = m_scr[j]
        vt = vt_ref[j * V_ROWS:(j + 1) * V_ROWS, pl.ds(off, TQ)]
        m_new = jnp.maximum(m, smax_ref[j])
        alpha = jnp.exp2(m - m_new)
        p = jnp.exp2(s_ref[j] - m_new).astype(BF16)
        acc_scr[j] = alpha * acc_scr[j] + _dot(vt, p)
        m_scr[j] = m_new

    def advance(slot_next, ki_next, slot_cur, ki_cur):
        for j in range(HEADS_PER_STEP):
            scores(slot_next, ki_next, j)
            accumulate(slot_cur, ki_cur, j)

    for j in range(HEADS_PER_STEP):
        m_scr[j] = jnp.full((1, TQ), NEG, F32)
        acc_scr[j] = jnp.zeros((V_ROWS, TQ), F32)

    slot_a = (sa_ref, smaxa_ref)
    slot_b = (sb_ref, smaxb_ref)
    n_pairs = qi // 2
    for j in range(HEADS_PER_STEP):
        scores(slot_a, qi, j, masked=True)

    def pair(t, _):
        advance(slot_b, 2 * t, slot_a, jnp.where(t == 0, qi, 2 * t - 1))
        advance(slot_a, 2 * t + 1, slot_b, 2 * t)
        return 0

    lax.fori_loop(0, n_pairs, pair, 0)
    in_a = jnp.where(n_pairs == 0, qi, 2 * n_pairs - 1)

    @pl.when(qi % 2 == 1)
    def _():
        advance(slot_b, qi - 1, slot_a, in_a)
        for j in range(HEADS_PER_STEP):
            accumulate(slot_b, qi - 1, j)

    @pl.when(qi % 2 == 0)
    def _():
        for j in range(HEADS_PER_STEP):
            accumulate(slot_a, in_a, j)

    for j in range(HEADS_PER_STEP):
        acc = acc_scr[j]
        o = acc[:V_DIM] / acc[V_DIM:V_DIM + 1]
        o_ref[:, j * V_DIM:(j + 1) * V_DIM] = o.T.astype(BF16)


def _sample_attn_kernel(n_pages, pt_ref, qa_ref, qn_ref, knn_ref, latn_ref, kpen_ref,
                        wukt_ref, lat_hbm, kpet_hbm, ctx_ref,
                        latbuf, kpebuf, wq, sems):
    n_seq = pl.num_programs(0)
    b = pl.program_id(0)
    slot = b % 2
    n_past = n_pages * PAGE_SIZE
    n_chunks = n_past // KEY_CHUNK
    pages_per_chunk = n_pages // n_chunks
    n_w = N_HEADS * NOPE_DIM

    def page_copies(seq, slot_, j):
        pg = pt_ref[seq * n_pages + j]
        row = pl.multiple_of(j * PAGE_SIZE, PAGE_SIZE)
        return (
            pltpu.make_async_copy(lat_hbm.at[pg], latbuf.at[slot_, pl.ds(row, PAGE_SIZE)],
                                  sems.at[0, slot_]),
            pltpu.make_async_copy(kpet_hbm.at[pg], kpebuf.at[slot_, :, pl.ds(row, PAGE_SIZE)],
                                  sems.at[1, slot_]),
        )

    def start_pages(seq, slot_, first, count):
        for j in range(count):
            for cp in page_copies(seq, slot_, first + j):
                cp.start()

    def wait_fetch(slot_):
        pltpu.make_async_copy(latbuf.at[1 - slot_], latbuf.at[slot_], sems.at[0, slot_]).wait()
        pltpu.make_async_copy(kpebuf.at[1 - slot_], kpebuf.at[slot_], sems.at[1, slot_]).wait()

    @pl.when(b == 0)
    def _():
        def first_fetch(c, _):
            start_pages(0, 0, c * pages_per_chunk, pages_per_chunk)
            return 0
        lax.fori_loop(0, n_chunks, first_fetch, 0)
        wq[0:n_w, :] = wukt_ref[...]

    wait_fetch(slot)

    qa = qa_ref[0]
    n_q = qa.shape[0]
    wq[n_w:n_w + n_q, :] = qa[:, :KV_RANK]
    q_pe = qa[:, KV_RANK:KV_RANK + ROPE_DIM]
    wq_all = wq[...]

    def scores(latc, pe):
        n = latc.shape[0]
        e = lax.dot_general(wq_all, latc, _NT, preferred_element_type=F32)
        e3 = e[:n_w].reshape(N_HEADS, NOPE_DIM, n)
        r = lax.rsqrt(jnp.sum(e3 * e3, axis=1) * (1.0 / NOPE_DIM) + EPS)
        raw3 = e[n_w:].reshape(n_q // N_HEADS, N_HEADS, n) * r[None]
        return raw3.reshape(n_q, n) + pe

    def new_key_scores():
        kpec = jnp.concatenate(
            [kpen_ref[0], jnp.zeros((LANES - SUBLANES, ROPE_DIM), F32)], axis=0).astype(BF16)
        s_new = lax.dot_general(q_pe, kpec, _NT, preferred_element_type=F32)
        cross = lax.dot_general(qn_ref[0], knn_ref[0], _NT, preferred_element_type=F32)
        n_new = knn_ref.shape[1] // N_HEADS
        row = lax.broadcasted_iota(jnp.int32, cross.shape, 0)
        col = lax.broadcasted_iota(jnp.int32, cross.shape, 1)
        same_head = row % N_HEADS == col % N_HEADS
        t_q = lax.broadcasted_iota(jnp.int32, (n_q, LANES), 0) // N_HEADS
        j_k = lax.broadcasted_iota(jnp.int32, (n_q, LANES), 1)
        for j in range(n_new):
            pick = jnp.where(same_head & (col // N_HEADS == j), cross, 0.0)
            s_new = s_new + jnp.where(j_k == j, jnp.sum(pick, axis=-1, keepdims=True), 0.0)
        return jnp.where(j_k <= t_q, s_new, NEG)

    m = jnp.full((n_q, 1), NEG, F32)
    l = jnp.zeros((n_q, 1), F32)
    acc = jnp.zeros((n_q, KV_RANK), F32)
    pending = None
    for c in range(n_chunks):
        keys = slice(c * KEY_CHUNK, (c + 1) * KEY_CHUNK)
        latc = latbuf[slot, keys, :].astype(BF16)
        kpec = kpebuf[slot, :, keys].astype(BF16)
        s = scores(latc, _dot(q_pe, kpec))
        first = 0 if c == 0 else (c + 1) * pages_per_chunk
        count = 2 * pages_per_chunk if c == 0 else (pages_per_chunk if c + 1 < n_chunks else 0)
        start_pages((b + 1) % n_seq, 1 - slot, first, count)
        if c == 0:
            s_new = new_key_scores()
        if pending is not None:
            alpha, p_prev, lat_prev = pending
            acc = alpha * acc + _dot(p_prev, lat_prev)
        m_new = jnp.maximum(m, jnp.max(s, axis=-1, keepdims=True))
        alpha = jnp.exp2(m - m_new)
        p = jnp.exp2(s - m_new)
        l = alpha * l + jnp.sum(p, axis=-1, keepdims=True)
        m = m_new
        pending = (alpha, p.astype(BF16), latc)
    alpha, p_prev, lat_prev = pending
    acc = alpha * acc + _dot(p_prev, lat_prev)

    lat_new = jnp.concatenate(
        [latn_ref[0], jnp.zeros((LANES - SUBLANES, KV_RANK), F32)], axis=0).astype(BF16)
    m_new = jnp.maximum(m, jnp.max(s_new, axis=-1, keepdims=True))
    alpha = jnp.exp2(m - m_new)
    p = jnp.exp2(s_new - m_new)
    l = alpha * l + jnp.sum(p, axis=-1, keepdims=True)
    acc = alpha * acc + _dot(p.astype(BF16), lat_new)
    ctx_ref[0] = acc / l

    @pl.when(b == n_seq - 1)
    def _():
        wait_fetch(1 - slot)


def _absorb_kernel(q_ref, gkn_ref, wukt_ref, o_ref):
    gkn = gkn_ref[...]
    w = KV_RANK + LANES
    for h in range(N_HEADS):
        qn = (q_ref[:, 2 * h * LANES:(2 * h + 1) * LANES].astype(F32) * gkn).astype(BF16)
        o_ref[:, h * w:h * w + KV_RANK] = _dot(
            qn, wukt_ref[h * NOPE_DIM:(h + 1) * NOPE_DIM, :]).astype(BF16)
        o_ref[:, h * w + KV_RANK:(h + 1) * w] = q_ref[:, (2 * h + 1) * LANES:(2 * h + 2) * LANES]


def _ctx_out_kernel(c_ref, wuv_ref, o_ref):
    for h in range(N_HEADS):
        c = c_ref[:, h * KV_RANK:(h + 1) * KV_RANK].astype(BF16)
        o_ref[:, h * V_DIM:(h + 1) * V_DIM] = _dot(
            c, wuv_ref[:, h * V_DIM:(h + 1) * V_DIM]).astype(BF16)


def _params(n_axes=1):
    return pltpu.CompilerParams(dimension_semantics=("arbitrary",) * n_axes,
                                vmem_limit_bytes=VMEM_LIMIT)


def _row_spec(width):
    return pl.BlockSpec((TM, width), lambda i: (i, 0))


def _table_spec(n_rows):
    tiles = n_rows // TM
    return pl.BlockSpec((TM, LANES), lambda i: (i % tiles, 0))


def _ffn_specs(layer):
    pick = lambda *_: (layer, 0, 0)
    return [_const_spec((1, D_MODEL)),
            pl.BlockSpec((None, D_MODEL, 2 * D_FF), pick, pipeline_mode=pl.Buffered(1)),
            pl.BlockSpec((None, D_FF, D_MODEL), pick, pipeline_mode=pl.Buffered(1))]


def _ffn_call(x, g, wgu, wd, layer):
    t = x.shape[0]
    return pl.pallas_call(
        _ffn_kernel, grid=(t // TM,),
        in_specs=[_row_spec(D_MODEL)] + _ffn_specs(layer),
        out_specs=_row_spec(D_MODEL),
        out_shape=jax.ShapeDtypeStruct((t, D_MODEL), F32),
        compiler_params=_params(), name="ffn")(x, g, wgu, wd)


def _mixer_call(x, g, win, cw, wout, tiles_per_seq, prev=None):
    t = x.shape[0]
    n_tiles = t // TM
    is_sample = prev is not None
    in_specs = [_row_spec(D_MODEL), _const_spec((1, D_MODEL)),
                _const_spec((D_MODEL, 3 * D_MODEL)), _const_spec((CONV_W, D_MODEL)),
                _const_spec((D_MODEL, D_MODEL))]
    args = [x, g, win, cw, wout]
    if is_sample:
        in_specs += [_row_spec(D_MODEL), _row_spec(D_MODEL)]
        args += list(prev)
        u_spec = _row_spec(D_MODEL)
        u_shape = jax.ShapeDtypeStruct((t, D_MODEL), F32)
    else:
        u_spec = pl.BlockSpec((1, SUBLANES, D_MODEL), lambda i: (i, 0, 0))
        u_shape = jax.ShapeDtypeStruct((n_tiles, SUBLANES, D_MODEL), F32)
    return pl.pallas_call(
        functools.partial(_mixer_kernel, tiles_per_seq, is_sample), grid=(n_tiles,),
        in_specs=in_specs,
        out_specs=[_row_spec(D_MODEL), u_spec],
        out_shape=[jax.ShapeDtypeStruct((t, D_MODEL), F32), u_shape],
        scratch_shapes=[pltpu.VMEM((TM + SUBLANES, D_MODEL), F32)],
        compiler_params=_params(), name="mixer")(*args)


def _ffn_kv_call(x, g, wgu, wd, layer, gkv, wdkv, kvn, ga, gb, cos_t, sin_t, wuk, wuvt, gkn):
    t = x.shape[0]
    in_specs = ([_row_spec(D_MODEL)] + _ffn_specs(layer) + [
        _const_spec((1, D_MODEL)), _const_spec((D_MODEL, KV_RANK + 2 * LANES)),
        _const_spec((1, KV_RANK)), _const_spec((1, LANES)), _const_spec((1, LANES)),
        _table_spec(cos_t.shape[0]), _table_spec(sin_t.shape[0]),
        _const_spec((KV_RANK, D_MODEL)), _const_spec((D_MODEL, KV_RANK)),
        _const_spec((1, NOPE_DIM))])
    out_specs = [_row_spec(D_MODEL), _row_spec(KV_RANK), _row_spec(ROPE_DIM),
                 _row_spec(2 * N_HEADS * LANES),
                 pl.BlockSpec((N_HEADS * V_ROWS, TM), lambda i: (0, i))]
    out_shape = [jax.ShapeDtypeStruct((t, D_MODEL), F32),
                 jax.ShapeDtypeStruct((t, KV_RANK), F32),
                 jax.ShapeDtypeStruct((t, ROPE_DIM), F32),
                 jax.ShapeDtypeStruct((t, 2 * N_HEADS * LANES), BF16),
                 jax.ShapeDtypeStruct((N_HEADS * V_ROWS, t), BF16)]
    return pl.pallas_call(
        _ffn_kv_kernel, grid=(t // TM,), in_specs=in_specs, out_specs=out_specs,
        out_shape=out_shape, compiler_params=_params(), name="ffn_kv")(
            x, g, wgu, wd, gkv, wdkv, kvn, ga, gb, cos_t, sin_t, wuk, wuvt, gkn)


def _ffn_q_call(x, g, wgu, wd, layer, gmix, wdq, qnorm, wuq, gqn, ga, gb, cos_t, sin_t):
    t = x.shape[0]
    in_specs = ([_row_spec(D_MODEL)] + _ffn_specs(layer) + [
        _const_spec((1, D_MODEL)), _const_spec((D_MODEL, Q_RANK)),
        _const_spec((1, Q_RANK)), _const_spec((Q_RANK, 3 * N_HEADS * LANES)),
        _const_spec((1, NOPE_DIM)), _const_spec((1, LANES)), _const_spec((1, LANES)),
        _table_spec(cos_t.shape[0]), _table_spec(sin_t.shape[0])])
    return pl.pallas_call(
        _ffn_q_kernel, grid=(t // TM,), in_specs=in_specs,
        out_specs=[_row_spec(D_MODEL), _row_spec(2 * N_HEADS * LANES)],
        out_shape=[jax.ShapeDtypeStruct((t, D_MODEL), F32),
                   jax.ShapeDtypeStruct((t, 2 * N_HEADS * LANES), BF16)],
        compiler_params=_params(), name="ffn_q")(
            x, g, wgu, wd, gmix, wdq, qnorm, wuq, gqn, ga, gb, cos_t, sin_t)


def _out_ffn_call(x, a, wo, g, wgu, wd, layer):
    t = x.shape[0]
    return pl.pallas_call(
        _out_ffn_kernel, grid=(t // TM,),
        in_specs=[_row_spec(D_MODEL), _row_spec(D_MODEL),
                  _const_spec((D_MODEL, D_MODEL))] + _ffn_specs(layer),
        out_specs=_row_spec(D_MODEL),
        out_shape=jax.ShapeDtypeStruct((t, D_MODEL), F32),
        compiler_params=_params(), name="out_ffn")(x, a, wo, g, wgu, wd)


def _prompt_attn_call(q, k, vt, batch, seq):
    nq = seq // TQ
    hps = HEADS_PER_STEP
    return pl.pallas_call(
        _prompt_attn_kernel, grid=(batch, N_HEADS // hps, nq),
        in_specs=[pl.BlockSpec((TQ, hps * 2 * LANES), lambda b, h, i: (b * nq + i, h)),
                  pl.BlockSpec((seq, hps * 2 * LANES), lambda b, h, i: (b, h)),
                  pl.BlockSpec((hps * V_ROWS, seq), lambda b, h, i: (h, b))],
        out_specs=pl.BlockSpec((TQ, hps * V_DIM), lambda b, h, i: (b * nq + i, h)),
        out_shape=jax.ShapeDtypeStruct((batch * seq, N_HEADS * V_DIM), BF16),
        scratch_shapes=[pltpu.VMEM((hps, TQ, TQ), F32), pltpu.VMEM((hps, TQ, TQ), F32),
                        pltpu.VMEM((hps, 1, TQ), F32), pltpu.VMEM((hps, 1, TQ), F32),
                        pltpu.VMEM((hps, 1, TQ), F32), pltpu.VMEM((hps, V_ROWS, TQ), F32)],
        compiler_params=_params(3), name="prompt_attn")(q, k, vt)


def _sample_attn_call(page_table, qa, qn, kn_new, lat_new, kpe_new, wukt,
                      cache_latent, cache_kpe):
    n_seq, n_pages = page_table.shape
    n_q = qa.shape[1]
    n_keys = n_pages * PAGE_SIZE
    grid_spec = pltpu.PrefetchScalarGridSpec(
        num_scalar_prefetch=1, grid=(n_seq,),
        in_specs=[pl.BlockSpec((1, n_q, KV_RANK + LANES), lambda b, pt: (b, 0, 0)),
                  pl.BlockSpec((1, n_q, NOPE_DIM), lambda b, pt: (b, 0, 0)),
                  pl.BlockSpec((1, kn_new.shape[1], NOPE_DIM), lambda b, pt: (b, 0, 0)),
                  pl.BlockSpec((1, SUBLANES, KV_RANK), lambda b, pt: (b, 0, 0)),
                  pl.BlockSpec((1, SUBLANES, ROPE_DIM), lambda b, pt: (b, 0, 0)),
                  pl.BlockSpec((N_HEADS * NOPE_DIM, KV_RANK), lambda b, pt: (0, 0),
                               pipeline_mode=pl.Buffered(1)),
                  pl.BlockSpec(memory_space=pl.ANY),
                  pl.BlockSpec(memory_space=pl.ANY)],
        out_specs=pl.BlockSpec((1, n_q, KV_RANK), lambda b, pt: (b, 0, 0)),
        scratch_shapes=[pltpu.VMEM((2, n_keys, KV_RANK), F32),
                        pltpu.VMEM((2, ROPE_DIM, n_keys), F32),
                        pltpu.VMEM((N_HEADS * NOPE_DIM + n_q, KV_RANK), BF16),
                        pltpu.SemaphoreType.DMA((2, 2))])
    return pl.pallas_call(
        functools.partial(_sample_attn_kernel, n_pages), grid_spec=grid_spec,
        out_shape=jax.ShapeDtypeStruct((n_seq, n_q, KV_RANK), F32),
        compiler_params=_params(), name="sample_attn")(
            page_table.reshape(-1), qa, qn, kn_new, lat_new, kpe_new, wukt, cache_latent,
            jnp.swapaxes(cache_kpe, 1, 2))


def _absorb_call(q, gkn, wukt):
    t = q.shape[0]
    return pl.pallas_call(
        _absorb_kernel, grid=(t // TM,),
        in_specs=[_row_spec(2 * N_HEADS * LANES), _const_spec((1, NOPE_DIM)),
                  _const_spec((N_HEADS * NOPE_DIM, KV_RANK))],
        out_specs=_row_spec(N_HEADS * (KV_RANK + LANES)),
        out_shape=jax.ShapeDtypeStruct((t, N_HEADS * (KV_RANK + LANES)), BF16),
        compiler_params=_params(), name="absorb")(q, gkn, wukt)


def _ctx_out_call(ctx, wuv):
    t = ctx.shape[0]
    return pl.pallas_call(
        _ctx_out_kernel, grid=(t // TM,),
        in_specs=[_row_spec(N_HEADS * KV_RANK),
                  _const_spec((KV_RANK, N_HEADS * V_DIM))],
        out_specs=_row_spec(N_HEADS * V_DIM),
        out_shape=jax.ShapeDtypeStruct((t, N_HEADS * V_DIM), BF16),
        compiler_params=_params(), name="ctx_out")(ctx, wuv)


def _rope_tables(pos):
    half = ROPE_DIM // 2
    freqs = ROPE_THETA ** (-jnp.arange(half, dtype=F32) / half)
    ang = pos.astype(F32)[:, None] * freqs
    cos = jnp.cos(ang)
    sin = jnp.sin(ang)
    zero = jnp.zeros((pos.shape[0], LANES - ROPE_DIM), F32)
    return (jnp.concatenate([cos, cos, zero], axis=1),
            jnp.concatenate([-sin, sin, zero], axis=1))


def _swap_halves(a):
    half = a.shape[-1] // 2
    return jnp.concatenate([a[..., half:], a[..., :half]], axis=-1)


def _pad_lanes(a):
    pad = [(0, 0)] * (a.ndim - 1) + [(0, LANES - a.shape[-1])]
    return jnp.pad(a, pad)


def kernel(x_prompt, x_sample, cache_latent, cache_kpe, state_conv, page_table, norm_ffn1, w_ffn1_gu, w_ffn1_down, norm_mix, norm_ffn2, w_ffn2_gu, w_ffn2_down, w_conv_in, conv_w, w_conv_out, w_dq, q_norm, w_uq, q_nope_norm, q_pe_norm, w_o, norm_kv_in, w_dkv, kv_norm, k_pe_norm, w_uk, w_uv, k_nope_norm):
    batch, seq, _ = x_prompt.shape
    n_dec, dec_seq, _ = x_sample.shape
    n_pages = page_table.shape[1]
    assert seq % TQ == 0 and (n_dec * dec_seq) % TM == 0 and dec_seq == DEC_SEQ
    assert N_HEADS % HEADS_PER_STEP == 0
    assert (n_pages * PAGE_SIZE) % KEY_CHUNK == 0 and n_pages * PAGE_SIZE >= 2 * KEY_CHUNK

    row = lambda a: a.reshape(1, -1)
    ffn1 = (w_ffn1_gu.astype(BF16), w_ffn1_down.astype(BF16))
    ffn2 = (w_ffn2_gu.astype(BF16), w_ffn2_down.astype(BF16))
    win = w_conv_in[0].astype(BF16)
    wout = w_conv_out[0].astype(BF16)
    w_pe = w_dkv[:, KV_RANK:]
    wdkv = jnp.concatenate([w_dkv[:, :KV_RANK], _pad_lanes(w_pe),
                            _pad_lanes(_swap_halves(w_pe))], axis=1).astype(BF16)
    wuk = w_uk.astype(BF16)
    wuvt = w_uv.T.astype(BF16)
    wukt = w_uk.T.astype(BF16)
    wuv = w_uv.astype(BF16)
    wdq = w_dq[0].astype(BF16)
    wuq3 = w_uq[0].reshape(Q_RANK, N_HEADS, NOPE_DIM + ROPE_DIM)
    wq_pe = wuq3[:, :, NOPE_DIM:]
    wuq = jnp.concatenate([
        wuq3[:, :, :NOPE_DIM].reshape(Q_RANK, -1),
        _pad_lanes(wq_pe).reshape(Q_RANK, -1),
        _pad_lanes(_swap_halves(wq_pe)).reshape(Q_RANK, -1)], axis=1).astype(BF16)
    wo = w_o[0].astype(BF16)
    kga = row(_pad_lanes(k_pe_norm))
    kgb = row(_pad_lanes(_swap_halves(k_pe_norm)))
    qga = row(_pad_lanes(q_pe_norm[0]))
    qgb = row(_pad_lanes(_swap_halves(q_pe_norm[0])))

    cos_p, sin_p = _rope_tables(jnp.arange(seq))
    pos_s = n_pages * PAGE_SIZE + jnp.arange(n_dec * dec_seq) % dec_seq
    cos_s, sin_s = _rope_tables(pos_s)

    prev = state_conv[0]
    zrow = jnp.zeros((n_dec, 1, D_MODEL), F32)
    p1 = jnp.concatenate([prev[:, 1:2], zrow, zrow, zrow], axis=1).reshape(-1, D_MODEL)
    p2 = jnp.concatenate([prev[:, 0:1], prev[:, 1:2], zrow, zrow], axis=1).reshape(-1, D_MODEL)

    def dense(x, tiles_per_seq, cos_t, sin_t, prev_rows):
        x = _ffn_call(x, row(norm_ffn1[0]), *ffn1, 0)
        x, u = _mixer_call(x, row(norm_mix[0]), win, conv_w[0], wout, tiles_per_seq, prev_rows)
        x, lat, kpe, k, vt = _ffn_kv_call(
            x, row(norm_ffn2[0]), *ffn2, 0, row(norm_kv_in), wdkv, row(kv_norm),
            kga, kgb, cos_t, sin_t, wuk, wuvt, row(k_nope_norm))
        x, q = _ffn_q_call(
            x, row(norm_ffn1[1]), *ffn1, 1, row(norm_mix[1]), wdq, row(q_norm[0]), wuq,
            row(q_nope_norm[0]), qga, qgb, cos_t, sin_t)
        return x, u, lat, kpe, k, vt, q

    xp, up, lat_p, kpe_p, k_p, vt_p, q_p = dense(
        x_prompt.reshape(batch * seq, D_MODEL), seq // TM, cos_p, sin_p, None)
    o_p = _prompt_attn_call(q_p, k_p, vt_p, batch, seq)
    y_p = _out_ffn_call(xp, o_p, wo, row(norm_ffn2[1]), *ffn2, 1)
    conv_p = up.reshape(batch, seq // TM, SUBLANES, D_MODEL)[:, -1, SUBLANES - 2:][None]

    xs, us, lat_s, kpe_s, k_s, _, q_s = dense(
        x_sample.reshape(n_dec * dec_seq, D_MODEL), 1, cos_s, sin_s, (p1, p2))
    qa = _absorb_call(q_s, row(k_nope_norm), wukt)
    qa = qa.reshape(n_dec, dec_seq * N_HEADS, KV_RANK + LANES)
    pad_rows = ((0, 0), (0, SUBLANES - dec_seq), (0, 0))
    lat_new = jnp.pad(lat_s.reshape(n_dec, dec_seq, KV_RANK), pad_rows)
    kpe_new = jnp.pad(kpe_s.reshape(n_dec, dec_seq, ROPE_DIM), pad_rows)
    qn_s = q_s.reshape(n_dec, dec_seq * N_HEADS, 2 * LANES)[:, :, :NOPE_DIM]
    kn_new = k_s.reshape(n_dec, dec_seq * N_HEADS, 2 * LANES)[:, :, :NOPE_DIM]
    ctx = _sample_attn_call(page_table, qa, qn_s, kn_new, lat_new, kpe_new, wukt,
                            cache_latent, cache_kpe)
    o_s = _ctx_out_call(ctx.reshape(n_dec * dec_seq, N_HEADS * KV_RANK), wuv)
    y_s = _out_ffn_call(xs, o_s, wo, row(norm_ffn2[1]), *ffn2, 1)
    conv_s = us.reshape(n_dec, dec_seq, D_MODEL)[:, dec_seq - (CONV_W - 1):][None]

    return (y_p.reshape(batch, seq, D_MODEL),
            y_s.reshape(n_dec, dec_seq, D_MODEL),
            conv_p, conv_s,
            lat_p.reshape(batch, seq, KV_RANK),
            kpe_p.reshape(batch, seq, ROPE_DIM),
            lat_s.reshape(n_dec, dec_seq, KV_RANK),
            kpe_s.reshape(n_dec, dec_seq, ROPE_DIM))
```

```python
import functools

import jax
import jax.numpy as jnp
from jax import lax
from jax.experimental import pallas as pl
from jax.experimental.pallas import tpu as pltpu

F32 = jnp.float32
BF16 = jnp.bfloat16

D_MODEL = 1024
D_FF = 2816
N_HEADS = 8
NOPE_DIM = 128
ROPE_DIM = 64
V_DIM = 128
Q_RANK = 512
KV_RANK = 256
CONV_W = 3
PAGE_SIZE = 128
ROPE_THETA = 10000.0
EPS = 1e-6
SCALE = (NOPE_DIM + ROPE_DIM) ** -0.5
LOG2E = 1.4426950408889634
Q_SCALE = SCALE * LOG2E
NEG = -1e30
DEC_SEQ = 4

LANES = 128
SUBLANES = 8
BF16_SUBLANES = 16
V_ROWS = V_DIM + BF16_SUBLANES
TM = 512
TQ = 1024
HEADS_PER_STEP = 2
KEY_CHUNK = 2048
VMEM_LIMIT = 56 * 1024 * 1024

_NT = (((1,), (1,)), ((), ()))


def _const_spec(shape):
    zeros = (0,) * len(shape)
    return pl.BlockSpec(shape, lambda *_: zeros, pipeline_mode=pl.Buffered(1))


def _rms(x, g):
    ms = jnp.mean(x * x, axis=-1, keepdims=True)
    return x * lax.rsqrt(ms + EPS) * g


def _dot(a, b):
    return jnp.dot(a, b, preferred_element_type=F32)


def _ffn(x, g, wgu_ref, wd_ref):
    xn = _rms(x, g).astype(BF16)
    gate = _dot(xn, wgu_ref[:, :D_FF])
    up = _dot(xn, wgu_ref[:, D_FF:])
    act = (gate * jax.nn.sigmoid(gate) * up).astype(BF16)
    return x + 0.5 * _dot(act, wd_ref[...])


def _rope_norm(a, b, ga, gb, cos_t, sin_t):
    ms = jnp.sum(a * a, axis=-1, keepdims=True) * (1.0 / ROPE_DIM)
    return (a * ga * cos_t + b * gb * sin_t) * lax.rsqrt(ms + EPS)


def _ffn_kernel(x_ref, g_ref, wgu_ref, wd_ref, o_ref):
    o_ref[...] = _ffn(x_ref[...], g_ref[...], wgu_ref, wd_ref)


def _mixer_kernel(tiles_per_seq, is_sample, *refs):
    if is_sample:
        (x_ref, g_ref, win_ref, cw_ref, wout_ref, p1_ref, p2_ref,
         o_ref, u_ref, ubuf) = refs
    else:
        x_ref, g_ref, win_ref, cw_ref, wout_ref, o_ref, u_ref, ubuf = refs
    i = pl.program_id(0)
    x = x_ref[...]
    xn = _rms(x, g_ref[...]).astype(BF16)
    bch = _dot(xn, win_ref[...])
    b = bch[:, :D_MODEL]
    u = bch[:, D_MODEL:2 * D_MODEL] * bch[:, 2 * D_MODEL:]

    @pl.when(i % tiles_per_seq == 0)
    def _():
        ubuf[0:SUBLANES, :] = jnp.zeros((SUBLANES, D_MODEL), F32)

    ubuf[SUBLANES:SUBLANES + TM, :] = u
    s1 = ubuf[SUBLANES - 1:SUBLANES - 1 + TM, :]
    s2 = ubuf[SUBLANES - 2:SUBLANES - 2 + TM, :]
    if is_sample:
        t = lax.broadcasted_iota(jnp.int32, (TM, 1), 0) % DEC_SEQ
        s1 = jnp.where(t >= 1, s1, p1_ref[...])
        s2 = jnp.where(t >= 2, s2, p2_ref[...])
    cw = cw_ref[...]
    y = cw[0:1, :] * s2 + cw[1:2, :] * s1 + cw[2:3, :] * u
    o_ref[...] = x + _dot((b * y).astype(BF16), wout_ref[...])
    if is_sample:
        u_ref[...] = u
    else:
        ubuf[0:SUBLANES, :] = u[TM - SUBLANES:, :]
        u_ref[0] = u[TM - SUBLANES:, :]


def _ffn_kv_kernel(x_ref, g_ref, wgu_ref, wd_ref, gkv_ref, wdkv_ref, kvn_ref,
                   ga_ref, gb_ref, cos_ref, sin_ref, wuk_ref, wuvt_ref, gkn_ref,
                   o_ref, lat_ref, kpe_ref, k_ref, vt_ref):
    x = _ffn(x_ref[...], g_ref[...], wgu_ref, wd_ref)
    o_ref[...] = x
    hk = _rms(x, gkv_ref[...]).astype(BF16)
    ckv = _dot(hk, wdkv_ref[...])
    lat = _rms(ckv[:, :KV_RANK], kvn_ref[...])
    lat_ref[...] = lat
    kpe = _rope_norm(ckv[:, KV_RANK:KV_RANK + LANES], ckv[:, KV_RANK + LANES:],
                     ga_ref[...], gb_ref[...], cos_ref[...], sin_ref[...])
    kpe_ref[...] = kpe[:, :ROPE_DIM]
    kpeb = kpe.astype(BF16)
    latb = lat.astype(BF16)
    ek = _dot(latb, wuk_ref[...])
    gkn = gkn_ref[...]
    for h in range(N_HEADS):
        e = ek[:, h * NOPE_DIM:(h + 1) * NOPE_DIM]
        k_ref[:, 2 * h * LANES:(2 * h + 1) * LANES] = _rms(e, gkn).astype(BF16)
        k_ref[:, (2 * h + 1) * LANES:(2 * h + 2) * LANES] = kpeb
    vt = lax.dot_general(wuvt_ref[...], latb, _NT, preferred_element_type=F32).astype(BF16)
    ones = jnp.ones((V_ROWS - V_DIM, vt.shape[1]), BF16)
    for h in range(N_HEADS):
        vt_ref[h * V_ROWS:h * V_ROWS + V_DIM, :] = vt[h * V_DIM:(h + 1) * V_DIM, :]
        vt_ref[h * V_ROWS + V_DIM:(h + 1) * V_ROWS, :] = ones


def _ffn_q_kernel(x_ref, g_ref, wgu_ref, wd_ref, gmix_ref, wdq_ref, qn_ref,
                  wuq_ref, gqn_ref, ga_ref, gb_ref, cos_ref, sin_ref,
                  o_ref, q_ref):
    x = _ffn(x_ref[...], g_ref[...], wgu_ref, wd_ref)
    o_ref[...] = x
    xn = _rms(x, gmix_ref[...]).astype(BF16)
    cq = _rms(_dot(xn, wdq_ref[...]), qn_ref[...]).astype(BF16)
    q = _dot(cq, wuq_ref[...])
    gqn = gqn_ref[...]
    ga = ga_ref[...]
    gb = gb_ref[...]
    cos_t = cos_ref[...]
    sin_t = sin_ref[...]
    hw = N_HEADS * LANES
    for h in range(N_HEADS):
        qn = _rms(q[:, h * LANES:(h + 1) * LANES], gqn) * Q_SCALE
        qp = _rope_norm(q[:, hw + h * LANES:hw + (h + 1) * LANES],
                        q[:, 2 * hw + h * LANES:2 * hw + (h + 1) * LANES],
                        ga, gb, cos_t, sin_t) * Q_SCALE
        q_ref[:, 2 * h * LANES:(2 * h + 1) * LANES] = qn.astype(BF16)
        q_ref[:, (2 * h + 1) * LANES:(2 * h + 2) * LANES] = qp.astype(BF16)


def _out_ffn_kernel(x_ref, a_ref, wo_ref, g_ref, wgu_ref, wd_ref, o_ref):
    x = x_ref[...] + _dot(a_ref[...], wo_ref[...])
    o_ref[...] = _ffn(x, g_ref[...], wgu_ref, wd_ref)


def _prompt_attn_kernel(q_ref, k_ref, vt_ref, o_ref,
                        sa_ref, sb_ref, smaxa_ref, smaxb_ref, m_scr, acc_scr):
    qi = pl.program_id(2)

    def scores(slot, ki, j, masked=False):
        s_ref, smax_ref = slot
        off = pl.multiple_of(ki * TQ, TQ)
        cols = slice(2 * j * LANES, 2 * (j + 1) * LANES)
        if not masked:
            s = lax.dot_general(k_ref[pl.ds(off, TQ), cols], q_ref[:, cols], _NT,
                                preferred_element_type=F32)
            s_ref[j] = s
            smax_ref[j] = jnp.max(s, axis=0, keepdims=True)
            return
        h = TQ // 2
        key = lax.broadcasted_iota(jnp.int32, (h, TQ), 0)
        qry = lax.broadcasted_iota(jnp.int32, (h, TQ), 1)
        top = lax.dot_general(k_ref[pl.ds(off, h), cols], q_ref[:, cols], _NT,
                              preferred_element_type=F32)
        top = jnp.where(qry >= key, top, NEG)
        bot = lax.dot_general(k_ref[pl.ds(pl.multiple_of(off + h, h), h), cols],
                              q_ref[h:, cols], _NT, preferred_element_type=F32)
        key_b = lax.broadcasted_iota(jnp.int32, (h, h), 0)
        qry_b = lax.broadcasted_iota(jnp.int32, (h, h), 1)
        bot = jnp.where(qry_b >= key_b, bot, NEG)
        s_ref[j, 0:h, :] = top
        s_ref[j, h:, 0:h] = jnp.full((h, h), NEG, F32)
        s_ref[j, h:, h:] = bot
        top_max = jnp.max(top, axis=0, keepdims=True)
        smax_ref[j, :, 0:h] = top_max[:, :h]
        smax_ref[j, :, h:] = jnp.maximum(top_max[:, h:], jnp.max(bot, axis=0, keepdims=True))

    def accumulate(slot, ki, j):
        s_ref, smax_ref = slot
        off = pl.multiple_of(ki * TQ, TQ)
        m = m_scr[j]
        vt = vt_ref[j * V_ROWS:(j + 1) * V_ROWS, pl.ds(off, TQ)]
        m_new = jnp.maximum(m, smax_ref[j])
        alpha = jnp.exp2(m - m_new)
        p = jnp.exp2(s_ref[j] - m_new).astype(BF16)
        acc_scr[j] = alpha * acc_scr[j] + _dot(vt, p)
        m_scr[j] = m_new

    def advance(slot_next, ki_next, slot_cur, ki_cur):
        for j in range(HEADS_PER_STEP):
            scores(slot_next, ki_next, j)
            accumulate(slot_cur, ki_cur, j)

    for j in range(HEADS_PER_STEP):
        m_scr[j] = jnp.full((1, TQ), NEG, F32)
        acc_scr[j] = jnp.zeros((V_ROWS, TQ), F32)

    slot_a = (sa_ref, smaxa_ref)
    slot_b = (sb_ref, smaxb_ref)
    n_pairs = qi // 2
    for j in range(HEADS_PER_STEP):
        scores(slot_a, qi, j, masked=True)

    def pair(t, _):
        advance(slot_b, 2 * t, slot_a, jnp.where(t == 0, qi, 2 * t - 1))
        advance(slot_a, 2 * t + 1, slot_b, 2 * t)
        return 0

    lax.fori_loop(0, n_pairs, pair, 0)
    in_a = jnp.where(n_pairs == 0, qi, 2 * n_pairs - 1)

    @pl.when(qi % 2 == 1)
    def _():
        advance(slot_b, qi - 1, slot_a, in_a)
        for j in range(HEADS_PER_STEP):
            accumulate(slot_b, qi - 1, j)

    @pl.when(qi % 2 == 0)
    def _():
        for j in range(HEADS_PER_STEP):
            accumulate(slot_a, in_a, j)

    for j in range(HEADS_PER_STEP):
        acc = acc_scr[j]
        o = acc[:V_DIM] / acc[V_DIM:V_DIM + 1]
        o_ref[:, j * V_DIM:(j + 1) * V_DIM] = o.T.astype(BF16)


def _sample_attn_kernel(n_pages, pt_ref, qa_ref, qn_ref, knn_ref, latn_ref, kpen_ref,
                        wukt_ref, lat_hbm, kpet_hbm, ctx_ref,
                        latbuf, kpebuf, wq, sems):
    n_seq = pl.num_programs(0)
    b = pl.program_id(0)
    slot = b % 2
    n_past = n_pages * PAGE_SIZE
    n_chunks = n_past // KEY_CHUNK
    pages_per_chunk = n_pages // n_chunks
    n_w = N_HEADS * NOPE_DIM

    def page_copies(seq, slot_, j):
        pg = pt_ref[seq * n_pages + j]
        row = pl.multiple_of(j * PAGE_SIZE, PAGE_SIZE)
        return (
            pltpu.make_async_copy(lat_hbm.at[pg], latbuf.at[slot_, pl.ds(row, PAGE_SIZE)],
                                  sems.at[0, slot_]),
            pltpu.make_async_copy(kpet_hbm.at[pg], kpebuf.at[slot_, :, pl.ds(row, PAGE_SIZE)],
                                  sems.at[1, slot_]),
        )

    def start_pages(seq, slot_, first, count):
        for j in range(count):
            for cp in page_copies(seq, slot_, first + j):
                cp.start()

    def wait_fetch(slot_):
        pltpu.make_async_copy(latbuf.at[1 - slot_], latbuf.at[slot_], sems.at[0, slot_]).wait()
        pltpu.make_async_copy(kpebuf.at[1 - slot_], kpebuf.at[slot_], sems.at[1, slot_]).wait()

    @pl.when(b == 0)
    def _():
        def first_fetch(c, _):
            start_pages(0, 0, c * pages_per_chunk, pages_per_chunk)
            return 0
        lax.fori_loop(0, n_chunks, first_fetch, 0)
        wq[0:n_w, :] = wukt_ref[...]

    wait_fetch(slot)

    qa = qa_ref[0]
    n_q = qa.shape[0]
    wq[n_w:n_w + n_q, :] = qa[:, :KV_RANK]
    q_pe = qa[:, KV_RANK:KV_RANK + ROPE_DIM]
    wq_all = wq[...]

    def scores(latc, pe):
        n = latc.shape[0]
        e = lax.dot_general(wq_all, latc, _NT, preferred_element_type=F32)
        e3 = e[:n_w].reshape(N_HEADS, NOPE_DIM, n)
        r = lax.rsqrt(jnp.sum(e3 * e3, axis=1) * (1.0 / NOPE_DIM) + EPS)
        raw3 = e[n_w:].reshape(n_q // N_HEADS, N_HEADS, n) * r[None]
        return raw3.reshape(n_q, n) + pe

    def new_key_scores():
        kpec = jnp.concatenate(
            [kpen_ref[0], jnp.zeros((LANES - SUBLANES, ROPE_DIM), F32)], axis=0).astype(BF16)
        s_new = lax.dot_general(q_pe, kpec, _NT, preferred_element_type=F32)
        cross = lax.dot_general(qn_ref[0], knn_ref[0], _NT, preferred_element_type=F32)
        n_new = knn_ref.shape[1] // N_HEADS
        row = lax.broadcasted_iota(jnp.int32, cross.shape, 0)
        col = lax.broadcasted_iota(jnp.int32, cross.shape, 1)
        same_head = row % N_HEADS == col % N_HEADS
        t_q = lax.broadcasted_iota(jnp.int32, (n_q, LANES), 0) // N_HEADS
        j_k = lax.broadcasted_iota(jnp.int32, (n_q, LANES), 1)
        for j in range(n_new):
            pick = jnp.where(same_head & (col // N_HEADS == j), cross, 0.0)
            s_new = s_new + jnp.where(j_k == j, jnp.sum(pick, axis=-1, keepdims=True), 0.0)
        return jnp.where(j_k <= t_q, s_new, NEG)

    m = jnp.full((n_q, 1), NEG, F32)
    l = jnp.zeros((n_q, 1), F32)
    acc = jnp.zeros((n_q, KV_RANK), F32)
    pending = None
    for c in range(n_chunks):
        keys = slice(c * KEY_CHUNK, (c + 1) * KEY_CHUNK)
        latc = latbuf[slot, keys, :].astype(BF16)
        kpec = kpebuf[slot, :, keys].astype(BF16)
        s = scores(latc, _dot(q_pe, kpec))
        first = 0 if c == 0 else (c + 1) * pages_per_chunk
        count = 2 * pages_per_chunk if c == 0 else (pages_per_chunk if c + 1 < n_chunks else 0)
        start_pages((b + 1) % n_seq, 1 - slot, first, count)
        if c == 0:
            s_new = new_key_scores()
        if pending is not None:
            alpha, p_prev, lat_prev = pending
            acc = alpha * acc + _dot(p_prev, lat_prev)
        m_new = jnp.maximum(m, jnp.max(s, axis=-1, keepdims=True))
        alpha = jnp.exp2(m - m_new)
        p = jnp.exp2(s - m_new)
        l = alpha * l + jnp.sum(p, axis=-1, keepdims=True)
        m = m_new
        pending = (alpha, p.astype(BF16), latc)
    alpha, p_prev, lat_prev = pending
    acc = alpha * acc + _dot(p_prev, lat_prev)

    lat_new = jnp.concatenate(
        [latn_ref[0], jnp.zeros((LANES - SUBLANES, KV_RANK), F32)], axis=0).astype(BF16)
    m_new = jnp.maximum(m, jnp.max(s_new, axis=-1, keepdims=True))
    alpha = jnp.exp2(m - m_new)
    p = jnp.exp2(s_new - m_new)
    l = alpha * l + jnp.sum(p, axis=-1, keepdims=True)
    acc = alpha * acc + _dot(p.astype(BF16), lat_new)
    ctx_ref[0] = acc / l

    @pl.when(b == n_seq - 1)
    def _():
        wait_fetch(1 - slot)


def _absorb_kernel(q_ref, gkn_ref, wukt_ref, o_ref):
    gkn = gkn_ref[...]
    w = KV_RANK + LANES
    for h in range(N_HEADS):
        qn = (q_ref[:, 2 * h * LANES:(2 * h + 1) * LANES].astype(F32) * gkn).astype(BF16)
        o_ref[:, h * w:h * w + KV_RANK] = _dot(
            qn, wukt_ref[h * NOPE_DIM:(h + 1) * NOPE_DIM, :]).astype(BF16)
        o_ref[:, h * w + KV_RANK:(h + 1) * w] = q_ref[:, (2 * h + 1) * LANES:(2 * h + 2) * LANES]


def _ctx_out_kernel(c_ref, wuv_ref, o_ref):
    for h in range(N_HEADS):
        c = c_ref[:, h * KV_RANK:(h + 1) * KV_RANK].astype(BF16)
        o_ref[:, h * V_DIM:(h + 1) * V_DIM] = _dot(
            c, wuv_ref[:, h * V_DIM:(h + 1) * V_DIM]).astype(BF16)


def _params(n_axes=1):
    return pltpu.CompilerParams(dimension_semantics=("arbitrary",) * n_axes,
                                vmem_limit_bytes=VMEM_LIMIT)


def _row_spec(width):
    return pl.BlockSpec((TM, width), lambda i: (i, 0))


def _table_spec(n_rows):
    tiles = n_rows // TM
    return pl.BlockSpec((TM, LANES), lambda i: (i % tiles, 0))


def _ffn_specs(layer):
    pick = lambda *_: (layer, 0, 0)
    return [_const_spec((1, D_MODEL)),
            pl.BlockSpec((None, D_MODEL, 2 * D_FF), pick, pipeline_mode=pl.Buffered(1)),
            pl.BlockSpec((None, D_FF, D_MODEL), pick, pipeline_mode=pl.Buffered(1))]


def _ffn_call(x, g, wgu, wd, layer):
    t = x.shape[0]
    return pl.pallas_call(
        _ffn_kernel, grid=(t // TM,),
        in_specs=[_row_spec(D_MODEL)] + _ffn_specs(layer),
        out_specs=_row_spec(D_MODEL),
        out_shape=jax.ShapeDtypeStruct((t, D_MODEL), F32),
        compiler_params=_params(), name="ffn")(x, g, wgu, wd)


def _mixer_call(x, g, win, cw, wout, tiles_per_seq, prev=None):
    t = x.shape[0]
    n_tiles = t // TM
    is_sample = prev is not None
    in_specs = [_row_spec(D_MODEL), _const_spec((1, D_MODEL)),
                _const_spec((D_MODEL, 3 * D_MODEL)), _const_spec((CONV_W, D_MODEL)),
                _const_spec((D_MODEL, D_MODEL))]
    args = [x, g, win, cw, wout]
    if is_sample:
        in_specs += [_row_spec(D_MODEL), _row_spec(D_MODEL)]
        args += list(prev)
        u_spec = _row_spec(D_MODEL)
        u_shape = jax.ShapeDtypeStruct((t, D_MODEL), F32)
    else:
        u_spec = pl.BlockSpec((1, SUBLANES, D_MODEL), lambda i: (i, 0, 0))
        u_shape = jax.ShapeDtypeStruct((n_tiles, SUBLANES, D_MODEL), F32)
    return pl.pallas_call(
        functools.partial(_mixer_kernel, tiles_per_seq, is_sample), grid=(n_tiles,),
        in_specs=in_specs,
        out_specs=[_row_spec(D_MODEL), u_spec],
        out_shape=[jax.ShapeDtypeStruct((t, D_MODEL), F32), u_shape],
        scratch_shapes=[pltpu.VMEM((TM + SUBLANES, D_MODEL), F32)],
        compiler_params=_params(), name="mixer")(*args)


def _ffn_kv_call(x, g, wgu, wd, layer, gkv, wdkv, kvn, ga, gb, cos_t, sin_t, wuk, wuvt, gkn):
    t = x.shape[0]
    in_specs = ([_row_spec(D_MODEL)] + _ffn_specs(layer) + [
        _const_spec((1, D_MODEL)), _const_spec((D_MODEL, KV_RANK + 2 * LANES)),
        _const_spec((1, KV_RANK)), _const_spec((1, LANES)), _const_spec((1, LANES)),
        _table_spec(cos_t.shape[0]), _table_spec(sin_t.shape[0]),
        _const_spec((KV_RANK, D_MODEL)), _const_spec((D_MODEL, KV_RANK)),
        _const_spec((1, NOPE_DIM))])
    out_specs = [_row_spec(D_MODEL), _row_spec(KV_RANK), _row_spec(ROPE_DIM),
                 _row_spec(2 * N_HEADS * LANES),
                 pl.BlockSpec((N_HEADS * V_ROWS, TM), lambda i: (0, i))]
    out_shape = [jax.ShapeDtypeStruct((t, D_MODEL), F32),
                 jax.ShapeDtypeStruct((t, KV_RANK), F32),
                 jax.ShapeDtypeStruct((t, ROPE_DIM), F32),
                 jax.ShapeDtypeStruct((t, 2 * N_HEADS * LANES), BF16),
                 jax.ShapeDtypeStruct((N_HEADS * V_ROWS, t), BF16)]
    return pl.pallas_call(
        _ffn_kv_kernel, grid=(t // TM,), in_specs=in_specs, out_specs=out_specs,
        out_shape=out_shape, compiler_params=_params(), name="ffn_kv")(
            x, g, wgu, wd, gkv, wdkv, kvn, ga, gb, cos_t, sin_t, wuk, wuvt, gkn)


def _ffn_q_call(x, g, wgu, wd, layer, gmix, wdq, qnorm, wuq, gqn, ga, gb, cos_t, sin_t):
    t = x.shape[0]
    in_specs = ([_row_spec(D_MODEL)] + _ffn_specs(layer) + [
        _const_spec((1, D_MODEL)), _const_spec((D_MODEL, Q_RANK)),
        _const_spec((1, Q_RANK)), _const_spec((Q_RANK, 3 * N_HEADS * LANES)),
        _const_spec((1, NOPE_DIM)), _const_spec((1, LANES)), _const_spec((1, LANES)),
        _table_spec(cos_t.shape[0]), _table_spec(sin_t.shape[0])])
    return pl.pallas_call(
        _ffn_q_kernel, grid=(t // TM,), in_specs=in_specs,
        out_specs=[_row_spec(D_MODEL), _row_spec(2 * N_HEADS * LANES)],
        out_shape=[jax.ShapeDtypeStruct((t, D_MODEL), F32),
                   jax.ShapeDtypeStruct((t, 2 * N_HEADS * LANES), BF16)],
        compiler_params=_params(), name="ffn_q")(
            x, g, wgu, wd, gmix, wdq, qnorm, wuq, gqn, ga, gb, cos_t, sin_t)


def _out_ffn_call(x, a, wo, g, wgu, wd, layer):
    t = x.shape[0]
    return pl.pallas_call(
        _out_ffn_kernel, grid=(t // TM,),
        in_specs=[_row_spec(D_MODEL), _row_spec(D_MODEL),
                  _const_spec((D_MODEL, D_MODEL))] + _ffn_specs(layer),
        out_specs=_row_spec(D_MODEL),
        out_shape=jax.ShapeDtypeStruct((t, D_MODEL), F32),
        compiler_params=_params(), name="out_ffn")(x, a, wo, g, wgu, wd)


def _prompt_attn_call(q, k, vt, batch, seq):
    nq = seq // TQ
    hps = HEADS_PER_STEP
    return pl.pallas_call(
        _prompt_attn_kernel, grid=(batch, N_HEADS // hps, nq),
        in_specs=[pl.BlockSpec((TQ, hps * 2 * LANES), lambda b, h, i: (b * nq + i, h)),
                  pl.BlockSpec((seq, hps * 2 * LANES), lambda b, h, i: (b, h)),
                  pl.BlockSpec((hps * V_ROWS, seq), lambda b, h, i: (h, b))],
        out_specs=pl.BlockSpec((TQ, hps * V_DIM), lambda b, h, i: (b * nq + i, h)),
        out_shape=jax.ShapeDtypeStruct((batch * seq, N_HEADS * V_DIM), BF16),
        scratch_shapes=[pltpu.VMEM((hps, TQ, TQ), F32), pltpu.VMEM((hps, TQ, TQ), F32),
                        pltpu.VMEM((hps, 1, TQ), F32), pltpu.VMEM((hps, 1, TQ), F32),
                        pltpu.VMEM((hps, 1, TQ), F32), pltpu.VMEM((hps, V_ROWS, TQ), F32)],
        compiler_params=_params(3), name="prompt_attn")(q, k, vt)


def _sample_attn_call(page_table, qa, qn, kn_new, lat_new, kpe_new, wukt,
                      cache_latent, cache_kpe):
    n_seq, n_pages = page_table.shape
    n_q = qa.shape[1]
    n_keys = n_pages * PAGE_SIZE
    grid_spec = pltpu.PrefetchScalarGridSpec(
        num_scalar_prefetch=1, grid=(n_seq,),
        in_specs=[pl.BlockSpec((1, n_q, KV_RANK + LANES), lambda b, pt: (b, 0, 0)),
                  pl.BlockSpec((1, n_q, NOPE_DIM), lambda b, pt: (b, 0, 0)),
                  pl.BlockSpec((1, kn_new.shape[1], NOPE_DIM), lambda b, pt: (b, 0, 0)),
                  pl.BlockSpec((1, SUBLANES, KV_RANK), lambda b, pt: (b, 0, 0)),
                  pl.BlockSpec((1, SUBLANES, ROPE_DIM), lambda b, pt: (b, 0, 0)),
                  pl.BlockSpec((N_HEADS * NOPE_DIM, KV_RANK), lambda b, pt: (0, 0),
                               pipeline_mode=pl.Buffered(1)),
                  pl.BlockSpec(memory_space=pl.ANY),
                  pl.BlockSpec(memory_space=pl.ANY)],
        out_specs=pl.BlockSpec((1, n_q, KV_RANK), lambda b, pt: (b, 0, 0)),
        scratch_shapes=[pltpu.VMEM((2, n_keys, KV_RANK), F32),
                        pltpu.VMEM((2, ROPE_DIM, n_keys), F32),
                        pltpu.VMEM((N_HEADS * NOPE_DIM + n_q, KV_RANK), BF16),
                        pltpu.SemaphoreType.DMA((2, 2))])
    return pl.pallas_call(
        functools.partial(_sample_attn_kernel, n_pages), grid_spec=grid_spec,
        out_shape=jax.ShapeDtypeStruct((n_seq, n_q, KV_RANK), F32),
        compiler_params=_params(), name="sample_attn")(
            page_table.reshape(-1), qa, qn, kn_new, lat_new, kpe_new, wukt, cache_latent,
            jnp.swapaxes(cache_kpe, 1, 2))


def _absorb_call(q, gkn, wukt):
    t = q.shape[0]
    return pl.pallas_call(
        _absorb_kernel, grid=(t // TM,),
        in_specs=[_row_spec(2 * N_HEADS * LANES), _const_spec((1, NOPE_DIM)),
                  _const_spec((N_HEADS * NOPE_DIM, KV_RANK))],
        out_specs=_row_spec(N_HEADS * (KV_RANK + LANES)),
        out_shape=jax.ShapeDtypeStruct((t, N_HEADS * (KV_RANK + LANES)), BF16),
        compiler_params=_params(), name="absorb")(q, gkn, wukt)


def _ctx_out_call(ctx, wuv):
    t = ctx.shape[0]
    return pl.pallas_call(
        _ctx_out_kernel, grid=(t // TM,),
        in_specs=[_row_spec(N_HEADS * KV_RANK),
                  _const_spec((KV_RANK, N_HEADS * V_DIM))],
        out_specs=_row_spec(N_HEADS * V_DIM),
        out_shape=jax.ShapeDtypeStruct((t, N_HEADS * V_DIM), BF16),
        compiler_params=_params(), name="ctx_out")(ctx, wuv)


def _rope_tables(pos):
    half = ROPE_DIM // 2
    freqs = ROPE_THETA ** (-jnp.arange(half, dtype=F32) / half)
    ang = pos.astype(F32)[:, None] * freqs
    cos = jnp.cos(ang)
    sin = jnp.sin(ang)
    zero = jnp.zeros((pos.shape[0], LANES - ROPE_DIM), F32)
    return (jnp.concatenate([cos, cos, zero], axis=1),
            jnp.concatenate([-sin, sin, zero], axis=1))


def _swap_halves(a):
    half = a.shape[-1] // 2
    return jnp.concatenate([a[..., half:], a[..., :half]], axis=-1)


def _pad_lanes(a):
    pad = [(0, 0)] * (a.ndim - 1) + [(0, LANES - a.shape[-1])]
    return jnp.pad(a, pad)


def kernel(x_prompt, x_sample, cache_latent, cache_kpe, state_conv, page_table, norm_ffn1, w_ffn1_gu, w_ffn1_down, norm_mix, norm_ffn2, w_ffn2_gu, w_ffn2_down, w_conv_in, conv_w, w_conv_out, w_dq, q_norm, w_uq, q_nope_norm, q_pe_norm, w_o, norm_kv_in, w_dkv, kv_norm, k_pe_norm, w_uk, w_uv, k_nope_norm):
    batch, seq, _ = x_prompt.shape
    n_dec, dec_seq, _ = x_sample.shape
    n_pages = page_table.shape[1]
    assert seq % TQ == 0 and (n_dec * dec_seq) % TM == 0 and dec_seq == DEC_SEQ
    assert N_HEADS % HEADS_PER_STEP == 0
    assert (n_pages * PAGE_SIZE) % KEY_CHUNK == 0 and n_pages * PAGE_SIZE >= 2 * KEY_CHUNK

    row = lambda a: a.reshape(1, -1)
    ffn1 = (w_ffn1_gu.astype(BF16), w_ffn1_down.astype(BF16))
    ffn2 = (w_ffn2_gu.astype(BF16), w_ffn2_down.astype(BF16))
    win = w_conv_in[0].astype(BF16)
    wout = w_conv_out[0].astype(BF16)
    w_pe = w_dkv[:, KV_RANK:]
    wdkv = jnp.concatenate([w_dkv[:, :KV_RANK], _pad_lanes(w_pe),
                            _pad_lanes(_swap_halves(w_pe))], axis=1).astype(BF16)
    wuk = w_uk.astype(BF16)
    wuvt = w_uv.T.astype(BF16)
    wukt = w_uk.T.astype(BF16)
    wuv = w_uv.astype(BF16)
    wdq = w_dq[0].astype(BF16)
    wuq3 = w_uq[0].reshape(Q_RANK, N_HEADS, NOPE_DIM + ROPE_DIM)
    wq_pe = wuq3[:, :, NOPE_DIM:]
    wuq = jnp.concatenate([
        wuq3[:, :, :NOPE_DIM].reshape(Q_RANK, -1),
        _pad_lanes(wq_pe).reshape(Q_RANK, -1),
        _pad_lanes(_swap_halves(wq_pe)).reshape(Q_RANK, -1)], axis=1).astype(BF16)
    wo = w_o[0].astype(BF16)
    kga = row(_pad_lanes(k_pe_norm))
    kgb = row(_pad_lanes(_swap_halves(k_pe_norm)))
    qga = row(_pad_lanes(q_pe_norm[0]))
    qgb = row(_pad_lanes(_swap_halves(q_pe_norm[0])))

    cos_p, sin_p = _rope_tables(jnp.arange(seq))
    pos_s = n_pages * PAGE_SIZE + jnp.arange(n_dec * dec_seq) % dec_seq
    cos_s, sin_s = _rope_tables(pos_s)

    prev = state_conv[0]
    zrow = jnp.zeros((n_dec, 1, D_MODEL), F32)
    p1 = jnp.concatenate([prev[:, 1:2], zrow, zrow, zrow], axis=1).reshape(-1, D_MODEL)
    p2 = jnp.concatenate([prev[:, 0:1], prev[:, 1:2], zrow, zrow], axis=1).reshape(-1, D_MODEL)

    def dense(x, tiles_per_seq, cos_t, sin_t, prev_rows):
        x = _ffn_call(x, row(norm_ffn1[0]), *ffn1, 0)
        x, u = _mixer_call(x, row(norm_mix[0]), win, conv_w[0], wout, tiles_per_seq, prev_rows)
        x, lat, kpe, k, vt = _ffn_kv_call(
            x, row(norm_ffn2[0]), *ffn2, 0, row(norm_kv_in), wdkv, row(kv_norm),
            kga, kgb, cos_t, sin_t, wuk, wuvt, row(k_nope_norm))
        x, q = _ffn_q_call(
            x, row(norm_ffn1[1]), *ffn1, 1, row(norm_mix[1]), wdq, row(q_norm[0]), wuq,
            row(q_nope_norm[0]), qga, qgb, cos_t, sin_t)
        return x, u, lat, kpe, k, vt, q

    xp, up, lat_p, kpe_p, k_p, vt_p, q_p = dense(
        x_prompt.reshape(batch * seq, D_MODEL), seq // TM, cos_p, sin_p, None)
    o_p = _prompt_attn_call(q_p, k_p, vt_p, batch, seq)
    y_p = _out_ffn_call(xp, o_p, wo, row(norm_ffn2[1]), *ffn2, 1)
    conv_p = up.reshape(batch, seq // TM, SUBLANES, D_MODEL)[:, -1, SUBLANES - 2:][None]

    xs, us, lat_s, kpe_s, k_s, _, q_s = dense(
        x_sample.reshape(n_dec * dec_seq, D_MODEL), 1, cos_s, sin_s, (p1, p2))
    qa = _absorb_call(q_s, row(k_nope_norm), wukt)
    qa = qa.reshape(n_dec, dec_seq * N_HEADS, KV_RANK + LANES)
    pad_rows = ((0, 0), (0, SUBLANES - dec_seq), (0, 0))
    lat_new = jnp.pad(lat_s.reshape(n_dec, dec_seq, KV_RANK), pad_rows)
    kpe_new = jnp.pad(kpe_s.reshape(n_dec, dec_seq, ROPE_DIM), pad_rows)
    qn_s = q_s.reshape(n_dec, dec_seq * N_HEADS, 2 * LANES)[:, :, :NOPE_DIM]
    kn_new = k_s.reshape(n_dec, dec_seq * N_HEADS, 2 * LANES)[:, :, :NOPE_DIM]
    ctx = _sample_attn_call(page_table, qa, qn_s, kn_new, lat_new, kpe_new, wukt,
                            cache_latent, cache_kpe)
    o_s = _ctx_out_call(ctx.reshape(n_dec * dec_seq, N_HEADS * KV_RANK), wuv)
    y_s = _out_ffn_call(xs, o_s, wo, row(norm_ffn2[1]), *ffn2, 1)
    conv_s = us.reshape(n_dec, dec_seq, D_MODEL)[:, dec_seq - (CONV_W - 1):][None]

    return (y_p.reshape(batch, seq, D_MODEL),
            y_s.reshape(n_dec, dec_seq, D_MODEL),
            conv_p, conv_s,
            lat_p.reshape(batch, seq, KV_RANK),
            kpe_p.reshape(batch, seq, ROPE_DIM),
            lat_s.reshape(n_dec, dec_seq, KV_RANK),
            kpe_s.reshape(n_dec, dec_seq, ROPE_DIM))
```
